```python
import math
import jax, jax.numpy as jnp
from jax import lax
import numpy as np

D_MODEL = 1024
BATCH = 4
SEQ = 4096
DEPTH = 4

HEAD_DIM = 64
BLOCK = 128
A_HEADS = 8
A_KV_HEADS = 2
A_WINDOW = 128
B_HEADS = 8
B_CONFIGS = ((128, 1), (512, 4), (2048, 16))
NUM_BUCKETS = 32
MAX_DISTANCE = 2048
C_HEADS = 16
C_Q_RANK = 256
C_KV_RANK = 128
C_NOPE = 64
C_ROPE = 32
C_V = 64
ROPE_THETA = 10000.0
EPS = 1e-6

A_WIDTH = A_HEADS * HEAD_DIM
A_KV_WIDTH = A_KV_HEADS * HEAD_DIM
B_WIDTH = B_HEADS * HEAD_DIM
AB_WIDTH = A_WIDTH + B_WIDTH
AB_IN = A_WIDTH + 2 * A_KV_WIDTH + 3 * B_WIDTH + AB_WIDTH
C_WIDTH = C_HEADS * C_V
C_IN = C_Q_RANK + C_KV_RANK + C_ROPE + C_WIDTH
N_AB = (DEPTH + 1) // 2
N_C = DEPTH // 2

kernel_name = "hybrid_swa_dilated_mla_gated"


def rmsnorm(x, w):
    xf = x.astype(jnp.float32)
    y = xf * lax.rsqrt(jnp.mean(xf * xf, axis=-1, keepdims=True) + EPS)
    return (y * w.astype(jnp.float32)).astype(x.dtype)


def t5_bucket(dist):
    max_exact = NUM_BUCKETS // 2
    d = jnp.maximum(dist, 1).astype(jnp.float32)
    large = max_exact + (jnp.log(d / max_exact) / math.log(MAX_DISTANCE / max_exact)
                         * (NUM_BUCKETS - max_exact)).astype(jnp.int32)
    large = jnp.minimum(large, NUM_BUCKETS - 1)
    return jnp.where(dist < max_exact, dist, large)


def band_bias(rel_bias_h, dilation, window):
    i = jnp.arange(BLOCK)[:, None]
    j = jnp.arange(2 * BLOCK)[None, :]
    dist = i + BLOCK - j
    in_band = (dist >= 0) & (dist <= window)
    bucket = t5_bucket(jnp.maximum(dist, 0) * dilation)
    bias = jnp.transpose(rel_bias_h[bucket].astype(jnp.float32), (2, 0, 1))
    return bias, in_band


def banded_attention(q, k, v, bias, in_band, sink=None):
    n, length, hq, dh = q.shape
    hkv = k.shape[2]
    g = hq // hkv
    nb = -(-length // BLOCK)
    padw = ((0, 0), (0, nb * BLOCK - length), (0, 0), (0, 0))
    qb = jnp.pad(q, padw).reshape(n, nb, BLOCK, hkv, g, dh)
    kb = jnp.pad(k, padw).reshape(n, nb, BLOCK, hkv, dh)
    vb = jnp.pad(v, padw).reshape(n, nb, BLOCK, hkv, dh)

    def with_prev(t):
        prev = jnp.pad(t, ((0, 0), (1, 0), (0, 0), (0, 0), (0, 0)))[:, :-1]
        return jnp.concatenate([prev, t], axis=2)

    kk, vv = with_prev(kb), with_prev(vb)
    s = jnp.einsum('nbqhgd,nbkhd->nbhgqk', qb, kk).astype(jnp.float32) * (dh ** -0.5)
    s = s + bias.reshape(hkv, g, BLOCK, 2 * BLOCK)
    kpos = (jnp.arange(nb)[:, None] - 1) * BLOCK + jnp.arange(2 * BLOCK)[None, :]
    mask = in_band[None] & (kpos >= 0)[:, None, :]
    s = jnp.where(mask[None, :, None, None], s, -jnp.inf)
    m = jnp.max(s, axis=-1, keepdims=True)
    if sink is not None:
        sk = sink.astype(jnp.float32).reshape(1, 1, hkv, g, 1, 1)
        m = jnp.maximum(m, sk)
    p = jnp.exp(s - m)
    denom = jnp.sum(p, axis=-1, keepdims=True)
    if sink is not None:
        denom = denom + jnp.exp(sk - m)
    o = jnp.einsum('nbhgqk,nbkhd->nbqhgd', (p / denom).astype(v.dtype), vv)
    o = o.reshape(n, nb * BLOCK, hq, dh)[:, :length]
    lse = (m + jnp.log(denom))[..., 0]
    lse = jnp.transpose(lse, (0, 1, 4, 2, 3)).reshape(n, nb * BLOCK, hq)[:, :length]
    return o, lse


def to_strided(t, d):
    b, s = t.shape[:2]
    sp = -(-s // d) * d
    t = jnp.pad(t, ((0, 0), (0, sp - s)) + ((0, 0),) * (t.ndim - 2))
    t = jnp.moveaxis(t.reshape((b, sp // d, d) + t.shape[2:]), 2, 1)
    return t.reshape((b * d, sp // d) + t.shape[3:])


def from_strided(t, b, s, d):
    length = t.shape[1]
    t = jnp.moveaxis(t.reshape((b, d, length) + t.shape[2:]), 1, 2)
    return t.reshape((b, length * d) + t.shape[3:])[:, :s]


def swa_dilated_layer(h, w_in, sinks, w_out, rel_bias):
    b, s, _ = h.shape
    splits = list(np.cumsum([A_WIDTH, A_KV_WIDTH, A_KV_WIDTH, B_WIDTH, B_WIDTH, B_WIDTH]))
    qa, ka, va, qb, kb, vb, z = jnp.split(h @ w_in, splits, axis=-1)
    qa = qa.reshape(b, s, A_HEADS, HEAD_DIM)
    ka = ka.reshape(b, s, A_KV_HEADS, HEAD_DIM)
    va = va.reshape(b, s, A_KV_HEADS, HEAD_DIM)
    qb, kb, vb = (t.reshape(b, s, B_HEADS, HEAD_DIM) for t in (qb, kb, vb))

    bias_a, band_a = band_bias(rel_bias[:, :A_HEADS], 1, A_WINDOW)
    oa, _ = banded_attention(qa, ka, va, bias_a, band_a, sinks)

    outs, lses = [], []
    for window, dil in B_CONFIGS:
        bias_b, band_b = band_bias(rel_bias[:, A_HEADS:], dil, window // dil)
        o, lse = banded_attention(to_strided(qb, dil), to_strided(kb, dil),
                                  to_strided(vb, dil), bias_b, band_b)
        outs.append(from_strided(o, b, s, dil))
        lses.append(from_strided(lse, b, s, dil))
    wts = jax.nn.softmax(jnp.stack(lses), axis=0)
    ob = jnp.sum(wts[..., None] * jnp.stack(outs).astype(jnp.float32), axis=0).astype(h.dtype)

    o = jnp.concatenate([oa.reshape(b, s, A_WIDTH), ob.reshape(b, s, B_WIDTH)], axis=-1)
    return (o * jax.nn.silu(z)) @ w_out


def rope(t, cos, sin):
    half = t.shape[-1] // 2
    t1, t2 = t[..., :half], t[..., half:]
    return jnp.concatenate([t1 * cos - t2 * sin, t1 * sin + t2 * cos], axis=-1).astype(t.dtype)


def mla_layer(h, positions, w_in, q_norm, w_qb, kv_norm, w_kvb, w_out):
    b, s, _ = h.shape
    splits = list(np.cumsum([C_Q_RANK, C_KV_RANK, C_ROPE]))
    c_q, c_kv, k_pe, z = jnp.split(h @ w_in, splits, axis=-1)
    q = (rmsnorm(c_q, q_norm) @ w_qb).reshape(b, s, C_HEADS, C_NOPE + C_ROPE)
    q_nope, q_pe = q[..., :C_NOPE], q[..., C_NOPE:]
    kv = (rmsnorm(c_kv, kv_norm) @ w_kvb).reshape(b, s, C_HEADS, C_NOPE + C_V)
    k_nope, v = kv[..., :C_NOPE], kv[..., C_NOPE:]

    inv_freq = ROPE_THETA ** (-jnp.arange(0, C_ROPE, 2, dtype=jnp.float32) / C_ROPE)
    ang = positions.astype(jnp.float32)[..., None] * inv_freq
    cos, sin = jnp.cos(ang)[:, :, None], jnp.sin(ang)[:, :, None]
    q_pe = rope(q_pe, cos, sin)
    k_pe = rope(k_pe[:, :, None], cos, sin)[:, :, 0]

    scale = (C_NOPE + C_ROPE) ** -0.5
    nb = s // BLOCK
    kpos = jnp.arange(s)

    def to_blocks(t):
        return jnp.moveaxis(t.reshape((b, nb, BLOCK) + t.shape[2:]), 1, 0)

    def attend_block(args):
        qn, qp, blk = args
        sc = (jnp.einsum('bqhd,bkhd->bhqk', qn, k_nope)
              + jnp.einsum('bqhr,bkr->bhqk', qp, k_pe)).astype(jnp.float32) * scale
        qpos = blk * BLOCK + jnp.arange(BLOCK)
        sc = jnp.where(kpos[None, :] <= qpos[:, None], sc, -jnp.inf)
        p = jax.nn.softmax(sc, axis=-1).astype(v.dtype)
        return jnp.einsum('bhqk,bkhd->bqhd', p, v)

    o = lax.map(attend_block, (to_blocks(q_nope), to_blocks(q_pe), jnp.arange(nb)))
    o = jnp.moveaxis(o, 0, 1).reshape(b, s, C_WIDTH)
    return (o * jax.nn.silu(z)) @ w_out


def setup_inputs(seed: int = 0) -> dict:
    key = jax.random.key(seed)
    ks = jax.random.split(key, 16)
    f32 = jnp.float32

    def dense(k, shape):
        return jax.random.normal(k, shape, f32) * shape[-2] ** -0.5

    def gain(k, shape):
        return 1.0 + 0.02 * jax.random.normal(k, shape, f32)

    return {
        "x": jax.random.normal(ks[0], (BATCH, SEQ, D_MODEL), f32),
        "positions": jnp.tile(jnp.arange(SEQ, dtype=jnp.int32)[None, :], (BATCH, 1)),
        "norm_w": gain(ks[1], (DEPTH, D_MODEL)),
        "rel_bias": 0.5 * jax.random.normal(ks[2], (NUM_BUCKETS, A_HEADS + B_HEADS), f32),
        "ab_w_in": dense(ks[3], (N_AB, D_MODEL, AB_IN)),
        "ab_sinks": jax.random.normal(ks[4], (N_AB, A_HEADS), f32),
        "ab_w_out": dense(ks[5], (N_AB, AB_WIDTH, D_MODEL)),
        "c_w_in": dense(ks[6], (N_C, D_MODEL, C_IN)),
        "c_q_norm": gain(ks[7], (N_C, C_Q_RANK)),
        "c_w_qb": dense(ks[8], (N_C, C_Q_RANK, C_HEADS * (C_NOPE + C_ROPE))),
        "c_kv_norm": gain(ks[9], (N_C, C_KV_RANK)),
        "c_w_kvb": dense(ks[10], (N_C, C_KV_RANK, C_HEADS * (C_NOPE + C_V))),
        "c_w_out": dense(ks[11], (N_C, C_WIDTH, D_MODEL)),
        "final_norm": gain(ks[12], (D_MODEL,)),
    }


def reference(x, positions, norm_w, rel_bias, ab_w_in, ab_sinks, ab_w_out,
              c_w_in, c_q_norm, c_w_qb, c_kv_norm, c_w_kvb, c_w_out, final_norm):
    for layer in range(DEPTH):
        h = rmsnorm(x, norm_w[layer])
        i = layer // 2
        if layer % 2 == 0:
            y = swa_dilated_layer(h, ab_w_in[i], ab_sinks[i], ab_w_out[i], rel_bias)
        else:
            y = mla_layer(h, positions, c_w_in[i], c_q_norm[i], c_w_qb[i],
                          c_kv_norm[i], c_w_kvb[i], c_w_out[i])
        x = x + y
    return rmsnorm(x, final_norm)
```

```python
import functools
import math

import numpy as np
import jax
import jax.numpy as jnp
from jax import lax
from jax.experimental import pallas as pl
from jax.experimental.pallas import tpu as pltpu

D_MODEL = 1024
BATCH = 4
SEQ = 4096
DEPTH = 4
HEAD_DIM = 64
BLOCK = 128
A_HEADS = 8
A_KV_HEADS = 2
A_WINDOW = 128
B_HEADS = 8
B_CONFIGS = ((128, 1), (512, 4), (2048, 16))
NUM_BUCKETS = 32
MAX_DISTANCE = 2048
C_HEADS = 16
C_Q_RANK = 256
C_KV_RANK = 128
C_NOPE = 64
C_ROPE = 32
C_V = 64
ROPE_THETA = 10000.0
EPS = 1e-6

A_WIDTH = A_HEADS * HEAD_DIM
A_KV_WIDTH = A_KV_HEADS * HEAD_DIM
B_WIDTH = B_HEADS * HEAD_DIM
AB_WIDTH = A_WIDTH + B_WIDTH
C_WIDTH = C_HEADS * C_V

LANES = 128
F32 = jnp.float32
BF16 = jnp.bfloat16
NEG_INF = float("-inf")

ROW_TILE = 512
FLASH_TILE = 512
VMEM_LIMIT = 48 * 1024 * 1024

QA_SLOT, KA_SLOT, VA_SLOT, QB_SLOT, KB_SLOT, VB_SLOT, N_QKV_SLOTS = 0, 4, 5, 6, 10, 14, 18
N_PAIRS = 4
A_PAIR_HEADS = [h for p in range(N_PAIRS) for h in (p, p + N_PAIRS)]


def _params(n_axes):
    return pltpu.CompilerParams(dimension_semantics=("arbitrary",) * n_axes,
                                vmem_limit_bytes=VMEM_LIMIT)


def _t5_bucket_np(dist):
    max_exact = NUM_BUCKETS // 2
    d = np.maximum(dist, 1).astype(np.float64)
    large = max_exact + (np.log(d / max_exact) / math.log(MAX_DISTANCE / max_exact)
                         * (NUM_BUCKETS - max_exact)).astype(np.int64)
    large = np.minimum(large, NUM_BUCKETS - 1)
    return np.where(dist < max_exact, dist, large).astype(np.int32)


def _band_tables(dilation, rows, cols, offset):
    dist = np.arange(rows)[:, None] + offset - np.arange(cols)[None, :]
    valid = (dist >= 0) & (dist <= BLOCK)
    return _t5_bucket_np(np.maximum(dist, 0) * dilation), valid


def _bias_tiles(rel_bias, heads, dilation, rows, cols, offset):
    bucket, valid = _band_tables(dilation, rows, cols, offset)
    per_head = jnp.take(rel_bias.astype(F32).T[np.asarray(heads)], jnp.asarray(bucket), axis=1)
    tiles = jnp.where(jnp.asarray(valid)[None], per_head, NEG_INF)
    return tiles.reshape(len(heads) // 2, 2, rows, cols)


def _rms(x, w):
    return x * lax.rsqrt(jnp.mean(x * x, axis=-1, keepdims=True) + EPS) * w


def _rep(stat, width):
    if width == LANES:
        return stat
    return jnp.concatenate([stat] * (width // LANES), axis=1)


def _silu(z):
    return z / (1.0 + jnp.exp(-z))


def _dot_nt(a, b):
    return lax.dot_general(a, b, (((1,), (1,)), ((), ())), preferred_element_type=F32)


def _low_half(rows):
    return lax.broadcasted_iota(jnp.int32, (rows, LANES), 1) < HEAD_DIM


def _split_heads(q, low):
    zero = jnp.zeros_like(q)
    return jnp.where(low, q, zero), jnp.where(low, zero, q)


def _inproj_even_kernel(x_ref, nw_ref, w_ref, qkv_ref, z_ref):
    hb = _rms(x_ref[0], nw_ref[...]).astype(BF16)
    n_chunks = w_ref.shape[1] // (2 * LANES)
    for c in range(n_chunks):
        r = jnp.dot(hb, w_ref[:, c * 2 * LANES:(c + 1) * 2 * LANES], preferred_element_type=F32)
        for t in range(2):
            slot = 2 * c + t
            val = r[:, t * LANES:(t + 1) * LANES].astype(BF16)
            if slot < N_QKV_SLOTS:
                qkv_ref[0, slot] = val
            else:
                z_ref[0, slot - N_QKV_SLOTS] = val


def _inproj_even(x, norm_w, w):
    n_z = AB_WIDTH // LANES
    return pl.pallas_call(
        _inproj_even_kernel,
        grid=(BATCH, SEQ // ROW_TILE),
        in_specs=[
            pl.BlockSpec((1, ROW_TILE, D_MODEL), lambda b, i: (b, i, 0)),
            pl.BlockSpec((1, D_MODEL), lambda b, i: (0, 0)),
            pl.BlockSpec(w.shape, lambda b, i: (0, 0)),
        ],
        out_specs=[
            pl.BlockSpec((1, N_QKV_SLOTS, ROW_TILE, LANES), lambda b, i: (b, 0, i, 0)),
            pl.BlockSpec((1, n_z, ROW_TILE, LANES), lambda b, i: (b, 0, i, 0)),
        ],
        out_shape=[
            jax.ShapeDtypeStruct((BATCH, N_QKV_SLOTS, SEQ, LANES), BF16),
            jax.ShapeDtypeStruct((BATCH, n_z, SEQ, LANES), BF16),
        ],
        compiler_params=_params(2),
        name="inproj_even",
    )(x, norm_w.reshape(1, D_MODEL), w)


def _outproj_kernel(*refs, n_g, final):
    g_refs, (w_ref, x_ref), rest = refs[:n_g], refs[n_g:n_g + 2], refs[n_g + 2:]
    g = jnp.concatenate([g_ref[0, p] for g_ref in g_refs for p in range(g_ref.shape[1])], axis=1)
    y = x_ref[0] + jnp.dot(g, w_ref[...], preferred_element_type=F32)
    if final:
        fw_ref, o_ref = rest
        y = _rms(y, fw_ref[...])
    else:
        (o_ref,) = rest
    o_ref[0] = y


def _outproj(gs, w, x, final_w=None):
    final = final_w is not None
    in_specs = [pl.BlockSpec((1, g.shape[1], ROW_TILE, LANES), lambda b, i: (b, 0, i, 0)) for g in gs] + [
        pl.BlockSpec(w.shape, lambda b, i: (0, 0)),
        pl.BlockSpec((1, ROW_TILE, D_MODEL), lambda b, i: (b, i, 0)),
    ]
    args = [*gs, w, x]
    if final:
        in_specs.append(pl.BlockSpec((1, D_MODEL), lambda b, i: (0, 0)))
        args.append(final_w.reshape(1, D_MODEL))
    return pl.pallas_call(
        functools.partial(_outproj_kernel, n_g=len(gs), final=final),
        grid=(BATCH, SEQ // ROW_TILE),
        in_specs=in_specs,
        out_specs=pl.BlockSpec((1, ROW_TILE, D_MODEL), lambda b, i: (b, i, 0)),
        out_shape=jax.ShapeDtypeStruct((BATCH, SEQ, D_MODEL), F32),
        compiler_params=_params(2),
        name="outproj_final" if final else "outproj",
    )(*args)


def _band_scores(q_heads, k, bias_ref, n_keys):
    return [_dot_nt(qh, k) + bias_ref[0, hh][:, 2 * BLOCK - n_keys:] for hh, qh in enumerate(q_heads)]


def _att_a_kernel(q_ref, k_ref, v_ref, z_ref, bias_ref, sink_ref, o_ref):
    low = _low_half(BLOCK)

    def block(qs, ks, n_keys):
        q_heads = _split_heads(q_ref[0, 0, pl.ds(qs, BLOCK), :], low)
        k = k_ref[0, 0, pl.ds(ks, n_keys), :]
        v = v_ref[0, 0, pl.ds(ks, n_keys), :]
        outs = []
        for hh, s in enumerate(_band_scores(q_heads, k, bias_ref, n_keys)):
            sink = sink_ref[0, hh]
            m = jnp.maximum(jnp.max(s, axis=-1, keepdims=True), sink)
            p = jnp.exp(s - _rep(m, n_keys))
            denom = jnp.sum(p, axis=-1, keepdims=True) + jnp.exp(sink - m)
            outs.append(jnp.dot(p.astype(BF16), v, preferred_element_type=F32) / denom)
        z = z_ref[0, 0, pl.ds(qs, BLOCK), :].astype(F32)
        o_ref[0, 0, pl.ds(qs, BLOCK), :] = (jnp.where(low, outs[0], outs[1]) * _silu(z)).astype(BF16)

    block(0, 0, BLOCK)

    def body(qb, carry):
        block(pl.multiple_of(qb * BLOCK, BLOCK), pl.multiple_of((qb - 1) * BLOCK, BLOCK), 2 * BLOCK)
        return carry

    lax.fori_loop(1, SEQ // BLOCK, body, 0)


def _att_a(qkv, z, bias, sinks):
    seq_spec = lambda slot_of: pl.BlockSpec((1, 1, SEQ, LANES), slot_of)
    return pl.pallas_call(
        _att_a_kernel,
        grid=(BATCH, N_PAIRS),
        in_specs=[
            seq_spec(lambda b, p: (b, QA_SLOT + p, 0, 0)),
            seq_spec(lambda b, p: (b, KA_SLOT, 0, 0)),
            seq_spec(lambda b, p: (b, VA_SLOT, 0, 0)),
            seq_spec(lambda b, p: (b, p, 0, 0)),
            pl.BlockSpec((1, 2, BLOCK, 2 * BLOCK), lambda b, p: (p, 0, 0, 0)),
            pl.BlockSpec((1, 2, 1, LANES), lambda b, p: (p, 0, 0, 0)),
        ],
        out_specs=seq_spec(lambda b, p: (b, p, 0, 0)),
        out_shape=jax.ShapeDtypeStruct((BATCH, N_PAIRS, SEQ, LANES), BF16),
        compiler_params=_params(2),
        name="att_swa",
    )(qkv, qkv, qkv, z, bias, sinks)


def _att_b_kernel(q1_ref, k1_ref, v1_ref, q4_ref, k4_ref, v4_ref, q16_ref, k16_ref, v16_ref,
                  z_ref, b1_ref, b4_ref, b16_ref, o_ref, m0_ref, m1_ref, l0_ref, l1_ref, acc_ref):
    low = _low_half(BLOCK)
    stat_refs = ((m0_ref, l0_ref), (m1_ref, l1_ref))

    def first_pass(rows, q, k, v, bias_ref, n_keys):
        pvs = []
        for hh, s in enumerate(_band_scores(_split_heads(q, low), k, bias_ref, n_keys)):
            m = jnp.max(s, axis=-1, keepdims=True)
            p = jnp.exp(s - m)
            m_ref, l_ref = stat_refs[hh]
            m_ref[rows, :] = jnp.broadcast_to(m, (BLOCK, LANES))
            l_ref[rows, :] = jnp.broadcast_to(jnp.sum(p, axis=-1, keepdims=True), (BLOCK, LANES))
            pvs.append(jnp.dot(p.astype(BF16), v, preferred_element_type=F32))
        acc_ref[rows, :] = jnp.where(low, pvs[0], pvs[1])

    def merge(rows, scores, v, low_rows):
        n_keys = scores[0].shape[1]
        pvs, alphas = [], []
        for hh, s in enumerate(scores):
            m_ref, l_ref = stat_refs[hh]
            m_old = m_ref[rows, :]
            m_new = jnp.maximum(m_old, jnp.max(s, axis=-1, keepdims=True))
            alpha = jnp.exp(m_old - m_new)
            p = jnp.exp(s - _rep(m_new, n_keys))
            m_ref[rows, :] = m_new
            l_ref[rows, :] = alpha * l_ref[rows, :] + jnp.sum(p, axis=-1, keepdims=True)
            pvs.append(jnp.dot(p.astype(BF16), v, preferred_element_type=F32))
            alphas.append(alpha)
        acc_ref[rows, :] = (jnp.where(low_rows, alphas[0], alphas[1]) * acc_ref[rows, :]
                            + jnp.where(low_rows, pvs[0], pvs[1]))

    def pass1(qs, ks, n_keys):
        first_pass(pl.ds(qs, BLOCK), q1_ref[0, 0, pl.ds(qs, BLOCK), :], k1_ref[0, 0, pl.ds(ks, n_keys), :],
                   v1_ref[0, 0, pl.ds(ks, n_keys), :], b1_ref, n_keys)

    pass1(0, 0, BLOCK)

    def body1(qb, carry):
        pass1(pl.multiple_of(qb * BLOCK, BLOCK), pl.multiple_of((qb - 1) * BLOCK, BLOCK), 2 * BLOCK)
        return carry

    lax.fori_loop(1, SEQ // BLOCK, body1, 0)

    dil = 4
    for f in range(dil):
        cols = slice(f * LANES, (f + 1) * LANES)

        def pass4(us, ks, n_keys, cols=cols, f=f):
            rows = pl.ds(us * dil + f, BLOCK, stride=dil)
            q = q4_ref[0, 0, pl.ds(us, BLOCK), cols]
            k = k4_ref[0, 0, pl.ds(ks, n_keys), cols]
            v = v4_ref[0, 0, pl.ds(ks, n_keys), cols]
            merge(rows, _band_scores(_split_heads(q, low), k, b4_ref, n_keys), v, low)

        pass4(0, 0, BLOCK)

        def body4(ub, carry, pass4=pass4):
            pass4(pl.multiple_of(ub * BLOCK, BLOCK), pl.multiple_of((ub - 1) * BLOCK, BLOCK), 2 * BLOCK)
            return carry

        lax.fori_loop(1, SEQ // dil // BLOCK, body4, 0)

    dil = 16
    n_t = SEQ // dil
    low_t = _low_half(n_t)

    def body16(r, carry):
        cols = pl.ds(pl.multiple_of(r * LANES, LANES), LANES)
        q = q16_ref[0, 0, :, cols]
        k = k16_ref[0, 0, :, cols]
        v = v16_ref[0, 0, :, cols]
        scores = [_dot_nt(qh, k) + b16_ref[0, hh] for hh, qh in enumerate(_split_heads(q, low_t))]
        merge(pl.ds(r, n_t, stride=dil), scores, v, low_t)
        return carry

    lax.fori_loop(0, dil, body16, 0)

    def finish(i, carry):
        rows = pl.ds(pl.multiple_of(i * BLOCK, BLOCK), BLOCK)
        o = acc_ref[rows, :] / jnp.where(low, l0_ref[rows, :], l1_ref[rows, :])
        o_ref[0, 0, rows, :] = (o * _silu(z_ref[0, 0, rows, :].astype(F32))).astype(BF16)
        return carry

    lax.fori_loop(0, SEQ // BLOCK, finish, 0)


def _att_b(qkv, z, b1, b4, b16):
    views = []
    for dil in (1, 4, 16):
        view = qkv.reshape(BATCH, N_QKV_SLOTS, SEQ // dil, dil * LANES)
        for slot in (QB_SLOT, KB_SLOT, VB_SLOT):
            views.append((view, pl.BlockSpec((1, 1, SEQ // dil, dil * LANES),
                                             lambda b, p, slot=slot: (b, slot + p, 0, 0))))
    bias_spec = lambda rows, cols: pl.BlockSpec((1, 2, rows, cols), lambda b, p: (p, 0, 0, 0))
    return pl.pallas_call(
        _att_b_kernel,
        grid=(BATCH, N_PAIRS),
        in_specs=[spec for _, spec in views] + [
            pl.BlockSpec((1, 1, SEQ, LANES), lambda b, p: (b, N_PAIRS + p, 0, 0)),
            bias_spec(BLOCK, 2 * BLOCK), bias_spec(BLOCK, 2 * BLOCK), bias_spec(SEQ // 16, SEQ // 16),
        ],
        out_specs=pl.BlockSpec((1, 1, SEQ, LANES), lambda b, p: (b, p, 0, 0)),
        out_shape=jax.ShapeDtypeStruct((BATCH, N_PAIRS, SEQ, LANES), BF16),
        scratch_shapes=[pltpu.VMEM((SEQ, LANES), F32)] * 5,
        compiler_params=_params(2),
        name="att_dilated",
    )(*[view for view, _ in views], z, b1, b4, b16)


def _rope_table_kernel(pos_ref, freq_ref, cos_ref, sin_ref):
    ang = pos_ref[0].astype(F32) * freq_ref[...]
    cos_ref[0] = jnp.cos(ang)
    sin_ref[0] = jnp.sin(ang)


def _rope_tables(positions):
    inv_freq = ROPE_THETA ** (-jnp.arange(0, C_ROPE, 2, dtype=F32) / C_ROPE)
    lane = np.arange(LANES)
    freq = jnp.where(jnp.asarray(lane >= C_NOPE), inv_freq[lane % (C_ROPE // 2)], 0.0).reshape(1, LANES)
    table = jax.ShapeDtypeStruct((BATCH, SEQ, LANES), F32)
    spec = pl.BlockSpec((1, ROW_TILE, LANES), lambda b, i: (b, i, 0))
    return pl.pallas_call(
        _rope_table_kernel,
        grid=(BATCH, SEQ // ROW_TILE),
        in_specs=[pl.BlockSpec((1, ROW_TILE, 1), lambda b, i: (b, i, 0)),
                  pl.BlockSpec((1, LANES), lambda b, i: (0, 0))],
        out_specs=[spec, spec],
        out_shape=[table, table],
        compiler_params=_params(2),
        name="rope_tables",
    )(positions.reshape(BATCH, SEQ, 1), freq)


C_LOWRANK = C_Q_RANK + C_KV_RANK + 2 * LANES


def _inproj_odd_kernel(x_ref, nw_ref, w_ref, qnw_ref, wq_ref, kvnw_ref, wk_ref, wv_ref, cos_ref, sin_ref,
                       q_ref, k_ref, v_ref, z_ref):
    hb = _rms(x_ref[0], nw_ref[...]).astype(BF16)
    c = jnp.dot(hb, w_ref[:, :C_LOWRANK], preferred_element_type=F32)
    for j in range(C_WIDTH // (2 * LANES)):
        lo = C_LOWRANK + j * 2 * LANES
        r = jnp.dot(hb, w_ref[:, lo:lo + 2 * LANES], preferred_element_type=F32)
        for t in range(2):
            z_ref[0, 2 * j + t] = r[:, t * LANES:(t + 1) * LANES].astype(BF16)

    qn = _rms(c[:, :C_Q_RANK], qnw_ref[...]).astype(BF16)
    kvn = _rms(c[:, C_Q_RANK:C_Q_RANK + C_KV_RANK], kvnw_ref[...]).astype(BF16)
    k_pe = c[:, C_Q_RANK + C_KV_RANK:C_Q_RANK + C_KV_RANK + LANES]
    k_pe_rot = c[:, C_Q_RANK + C_KV_RANK + LANES:]
    cos, sin = cos_ref[0], sin_ref[0]
    lane = lax.broadcasted_iota(jnp.int32, cos.shape, 1)
    q_table = jnp.where(lane < C_NOPE + C_ROPE, cos, sin) * ((C_NOPE + C_ROPE) ** -0.5)
    k_rope = k_pe * cos + k_pe_rot * sin

    for j in range(C_HEADS // 2):
        cols = slice(j * 2 * LANES, (j + 1) * 2 * LANES)
        rq = jnp.dot(qn, wq_ref[:, cols], preferred_element_type=F32)
        rk = jnp.dot(kvn, wk_ref[:, cols], preferred_element_type=F32)
        for t in range(2):
            lanes = slice(t * LANES, (t + 1) * LANES)
            q_ref[0, 2 * j + t] = (rq[:, lanes] * q_table).astype(BF16)
            k_ref[0, 2 * j + t] = (rk[:, lanes] + k_rope).astype(BF16)
    for j in range(C_WIDTH // (2 * LANES)):
        rv = jnp.dot(kvn, wv_ref[:, j * 2 * LANES:(j + 1) * 2 * LANES], preferred_element_type=F32)
        for t in range(2):
            v_ref[0, 2 * j + t] = rv[:, t * LANES:(t + 1) * LANES].astype(BF16)


def _inproj_odd(x, norm_w, w, q_norm, wq, kv_norm, wk, wv, cos, sin):
    full = lambda a: pl.BlockSpec(a.shape, lambda b, i: (0,) * a.ndim)
    row = lambda width: pl.BlockSpec((1, ROW_TILE, width), lambda b, i: (b, i, 0))
    slots = lambda n: pl.BlockSpec((1, n, ROW_TILE, LANES), lambda b, i: (b, 0, i, 0))
    out = lambda n: jax.ShapeDtypeStruct((BATCH, n, SEQ, LANES), BF16)
    n_pairs = C_WIDTH // LANES
    vecs = [norm_w.reshape(1, D_MODEL), q_norm.reshape(1, C_Q_RANK), kv_norm.reshape(1, C_KV_RANK)]
    return pl.pallas_call(
        _inproj_odd_kernel,
        grid=(BATCH, SEQ // ROW_TILE),
        in_specs=[row(D_MODEL), full(vecs[0]), full(w), full(vecs[1]), full(wq), full(vecs[2]), full(wk), full(wv),
                  row(LANES), row(LANES)],
        out_specs=[slots(C_HEADS), slots(C_HEADS), slots(n_pairs), slots(n_pairs)],
        out_shape=[out(C_HEADS), out(C_HEADS), out(n_pairs), out(n_pairs)],
        compiler_params=_params(2),
        name="inproj_mla",
    )(x, vecs[0], w, vecs[1], wq, vecs[2], wk, wv, cos, sin)


def _flash_kernel(q_ref, k_ref, v_ref, z_ref, o_ref, m0_ref, m1_ref, l0_ref, l1_ref, acc_ref):
    t = FLASH_TILE
    qi = pl.program_id(2)
    low = _low_half(t)
    stat_refs = ((m0_ref, l0_ref), (m1_ref, l1_ref))
    for m_ref, l_ref in stat_refs:
        m_ref[...] = jnp.full((t, LANES), NEG_INF, F32)
        l_ref[...] = jnp.zeros((t, LANES), F32)
    acc_ref[...] = jnp.zeros((t, LANES), F32)

    def step(j, diagonal):
        ks = pl.multiple_of(j * t, t)
        v = v_ref[0, 0, pl.ds(ks, t), :]
        pvs, alphas = [], []
        for hh, (m_ref, l_ref) in enumerate(stat_refs):
            s = _dot_nt(q_ref[0, hh], k_ref[0, hh, pl.ds(ks, t), :])
            if diagonal:
                causal = lax.broadcasted_iota(jnp.int32, (t, t), 1) <= lax.broadcasted_iota(jnp.int32, (t, t), 0)
                s = jnp.where(causal, s, NEG_INF)
            m_old = m_ref[...]
            m_new = jnp.maximum(m_old, jnp.max(s, axis=-1, keepdims=True))
            alpha = jnp.exp(m_old - m_new)
            p = jnp.exp(s - _rep(m_new, t))
            m_ref[...] = m_new
            l_ref[...] = alpha * l_ref[...] + jnp.sum(p, axis=-1, keepdims=True)
            pvs.append(jnp.dot(p.astype(BF16), v, preferred_element_type=F32))
            alphas.append(alpha)
        acc_ref[...] = jnp.where(low, alphas[0], alphas[1]) * acc_ref[...] + jnp.where(low, pvs[0], pvs[1])

    def body(j, carry):
        step(j, False)
        return carry

    lax.fori_loop(0, qi, body, 0)
    step(qi, True)
    o = acc_ref[...] / jnp.where(low, l0_ref[...], l1_ref[...])
    o_ref[0, 0] = (o * _silu(z_ref[0, 0].astype(F32))).astype(BF16)


def _flash(q, k, v, z):
    t = FLASH_TILE
    n_pairs = C_WIDTH // LANES
    tile = pl.BlockSpec((1, 1, t, LANES), lambda b, p, i: (b, p, i, 0))
    return pl.pallas_call(
        _flash_kernel,
        grid=(BATCH, n_pairs, SEQ // t),
        in_specs=[
            pl.BlockSpec((1, 2, t, LANES), lambda b, p, i: (b, p, i, 0)),
            pl.BlockSpec((1, 2, SEQ, LANES), lambda b, p, i: (b, p, 0, 0)),
            pl.BlockSpec((1, 1, SEQ, LANES), lambda b, p, i: (b, p, 0, 0)),
            tile,
        ],
        out_specs=tile,
        out_shape=jax.ShapeDtypeStruct((BATCH, n_pairs, SEQ, LANES), BF16),
        scratch_shapes=[pltpu.VMEM((t, LANES), F32)] * 5,
        compiler_params=_params(3),
        name="mla_flash",
    )(q, k, v, z)


def _even_weights(w_in, sinks, w_out):
    perm = np.concatenate([np.arange(h * HEAD_DIM, (h + 1) * HEAD_DIM) for h in A_PAIR_HEADS])
    splits = np.cumsum([A_WIDTH, A_KV_WIDTH, A_KV_WIDTH, B_WIDTH, B_WIDTH, B_WIDTH])
    qa, ka, va, qb, kb, vb, z = jnp.split(w_in, splits, axis=1)
    scale = HEAD_DIM ** -0.5
    w = jnp.concatenate([qa[:, perm] * scale, ka, va, qb * scale, kb, vb, z[:, :A_WIDTH][:, perm], z[:, A_WIDTH:]],
                        axis=1).astype(BF16)
    w_o = jnp.concatenate([w_out[:A_WIDTH][perm], w_out[A_WIDTH:]], axis=0).astype(BF16)
    sink_tiles = jnp.broadcast_to(sinks.astype(F32)[np.asarray(A_PAIR_HEADS)].reshape(N_PAIRS, 2, 1, 1),
                                  (N_PAIRS, 2, 1, LANES))
    return w, sink_tiles, w_o


def _rot_half_cols(w):
    half = w.shape[1] // 2
    return jnp.concatenate([-w[:, half:], w[:, :half]], axis=1)


def _odd_weights(w_in, w_qb, w_kvb):
    splits = np.cumsum([C_Q_RANK, C_KV_RANK, C_ROPE])
    w_cq, w_ckv, w_kpe, w_z = jnp.split(w_in, splits, axis=1)
    pad = jnp.zeros((D_MODEL, C_NOPE), w_in.dtype)
    w_kpe_rot = _rot_half_cols(w_kpe)
    w = jnp.concatenate([w_cq, w_ckv, pad, w_kpe, w_kpe, pad, w_kpe_rot, w_kpe_rot, w_z], axis=1).astype(BF16)

    wq = w_qb.reshape(C_Q_RANK, C_HEADS, C_NOPE + C_ROPE)
    wq_pe = wq[:, :, C_NOPE:]
    wq_rot = jnp.concatenate([-wq_pe[:, :, C_ROPE // 2:], wq_pe[:, :, :C_ROPE // 2]], axis=2)
    wq = jnp.concatenate([wq, wq_rot], axis=2).reshape(C_Q_RANK, C_HEADS * LANES).astype(BF16)

    wkv = w_kvb.reshape(C_KV_RANK, C_HEADS, C_NOPE + C_V)
    wk = jnp.concatenate([wkv[:, :, :C_NOPE], jnp.zeros((C_KV_RANK, C_HEADS, LANES - C_NOPE), w_kvb.dtype)], axis=2)
    wk = wk.reshape(C_KV_RANK, C_HEADS * LANES).astype(BF16)
    wv = wkv[:, :, C_NOPE:].reshape(C_KV_RANK, C_WIDTH).astype(BF16)
    return w, wq, wk, wv


def kernel(x, positions, norm_w, rel_bias, ab_w_in, ab_sinks, ab_w_out, c_w_in, c_q_norm, c_w_qb, c_kv_norm,
           c_w_kvb, c_w_out, final_norm):
    b_heads = list(range(A_HEADS, A_HEADS + B_HEADS))
    bias_a = _bias_tiles(rel_bias, A_PAIR_HEADS, 1, BLOCK, 2 * BLOCK, BLOCK)
    bias_b1 = _bias_tiles(rel_bias, b_heads, 1, BLOCK, 2 * BLOCK, BLOCK)
    bias_b4 = _bias_tiles(rel_bias, b_heads, 4, BLOCK, 2 * BLOCK, BLOCK)
    bias_b16 = _bias_tiles(rel_bias, b_heads, 16, SEQ // 16, SEQ // 16, 0)
    cos, sin = _rope_tables(positions)

    for layer in range(DEPTH):
        i = layer // 2
        last = final_norm if layer == DEPTH - 1 else None
        if layer % 2 == 0:
            w, sink_tiles, w_o = _even_weights(ab_w_in[i], ab_sinks[i], ab_w_out[i])
            qkv, z = _inproj_even(x, norm_w[layer], w)
            gs = [_att_a(qkv, z, bias_a, sink_tiles), _att_b(qkv, z, bias_b1, bias_b4, bias_b16)]
        else:
            w, wq, wk, wv = _odd_weights(c_w_in[i], c_w_qb[i], c_w_kvb[i])
            q, k, v, z = _inproj_odd(x, norm_w[layer], w, c_q_norm[i], wq, c_kv_norm[i], wk, wv, cos, sin)
            gs = [_flash(q, k, v, z)]
            w_o = c_w_out[i].astype(BF16)
        x = _outproj(gs, w_o, x, last)
    return x
```

```python
import functools
import math

import numpy as np
import jax
import jax.numpy as jnp
from jax import lax
from jax.experimental import pallas as pl
from jax.experimental.pallas import tpu as pltpu

D_MODEL = 1024
BATCH = 4
SEQ = 4096
DEPTH = 4
HEAD_DIM = 64
BLOCK = 128
A_HEADS = 8
A_KV_HEADS = 2
A_WINDOW = 128
B_HEADS = 8
B_CONFIGS = ((128, 1), (512, 4), (2048, 16))
NUM_BUCKETS = 32
MAX_DISTANCE = 2048
C_HEADS = 16
C_Q_RANK = 256
C_KV_RANK = 128
C_NOPE = 64
C_ROPE = 32
C_V = 64
ROPE_THETA = 10000.0
EPS = 1e-6

A_WIDTH = A_HEADS * HEAD_DIM
A_KV_WIDTH = A_KV_HEADS * HEAD_DIM
B_WIDTH = B_HEADS * HEAD_DIM
AB_WIDTH = A_WIDTH + B_WIDTH
C_WIDTH = C_HEADS * C_V

LANES = 128
F32 = jnp.float32
BF16 = jnp.bfloat16
NEG_INF = float("-inf")
LOG2E = math.log2(math.e)
UNDERFLOW_GUARD = 2.0 ** -100
NORM_FLOOR = 1e-30

ROW_TILE = 512
FLASH_TILE = 512
VMEM_LIMIT = 48 * 1024 * 1024

N_PAIRS = 4
KA_SLOT, VA_SLOT, N_A_SLOTS = 4, 5, 6
QB_SLOT, KB_SLOT, VB_SLOT, N_B_SLOTS = 0, 4, 8, 12
BAND_UNROLL = 4
A_PAIR_HEADS = [h for p in range(N_PAIRS) for h in (p, p + N_PAIRS)]
B_PAIR_HEADS = list(range(A_HEADS, A_HEADS + B_HEADS))


def _params(n_axes):
    return pltpu.CompilerParams(dimension_semantics=("arbitrary",) * n_axes,
                                vmem_limit_bytes=VMEM_LIMIT)


def _t5_bucket_np(dist):
    max_exact = NUM_BUCKETS // 2
    d = np.maximum(dist, 1).astype(np.float64)
    large = max_exact + (np.log(d / max_exact) / math.log(MAX_DISTANCE / max_exact)
                         * (NUM_BUCKETS - max_exact)).astype(np.int64)
    large = np.minimum(large, NUM_BUCKETS - 1)
    return np.where(dist < max_exact, dist, large).astype(np.int32)


def _bias_tiles(rel_bias, heads, dilation, rows, cols, offset):
    buckets = _t5_bucket_np(np.arange(BLOCK + 1) * dilation)
    per_head = jnp.stack([rel_bias[:, h] for h in heads]).astype(F32) * LOG2E
    dist = (lax.broadcasted_iota(jnp.int32, (rows, cols), 0) + offset
            - lax.broadcasted_iota(jnp.int32, (rows, cols), 1))
    tiles = jnp.full((len(heads), rows, cols), NEG_INF, F32)
    for bucket in np.unique(buckets):
        first = int(np.argmax(buckets == bucket))
        tiles = jnp.where((dist >= first)[None], per_head[:, int(bucket)][:, None, None], tiles)
    tiles = jnp.where(((dist >= 0) & (dist <= BLOCK))[None], tiles, NEG_INF)
    return tiles.reshape(len(heads) // 2, 2 * rows, cols)


def _head_stats(rel_bias, heads, sinks=None):
    bias_max = jnp.max(rel_bias.astype(F32), axis=0) * LOG2E
    rows = [jnp.zeros((len(heads),), F32) if sinks is None
            else jnp.stack([sinks[h] for h in heads]).astype(F32) * LOG2E,
            jnp.stack([bias_max[h] for h in heads])]
    rows += [jnp.zeros((len(heads),), F32)] * 6
    stats = jnp.stack(rows, axis=1)
    return jnp.broadcast_to(stats[:, :, None], (len(heads), 8, LANES)).reshape(len(heads) // 2, 2, 8, LANES)


def _rms(x, w):
    return x * lax.rsqrt(jnp.mean(x * x, axis=-1, keepdims=True) + EPS) * w


def _rep(stat, width):
    if width == LANES:
        return stat
    return jnp.concatenate([stat] * (width // LANES), axis=1)


def _silu(z):
    return z / (1.0 + jnp.exp(-z))


def _dot_nt(a, b):
    return lax.dot_general(a, b, (((1,), (1,)), ((), ())), preferred_element_type=F32)


def _low_half(rows):
    return lax.broadcasted_iota(jnp.int32, (rows, LANES), 1) < HEAD_DIM


def _inproj_even_kernel(x_ref, nw_ref, w_ref, a_ref, b_ref, g_ref):
    hb = _rms(x_ref[0], nw_ref[...]).astype(BF16)
    q_scale = HEAD_DIM ** -0.5 * LOG2E
    n_chunks = w_ref.shape[1] // (2 * LANES)
    for c in range(n_chunks):
        r = jnp.dot(hb, w_ref[:, c * 2 * LANES:(c + 1) * 2 * LANES], preferred_element_type=F32)
        for t in range(2):
            slot = 2 * c + t
            val = r[:, t * LANES:(t + 1) * LANES]
            if slot < N_A_SLOTS:
                a_ref[0, slot] = (val * q_scale if slot < N_PAIRS else val).astype(BF16)
            elif slot < N_A_SLOTS + N_B_SLOTS:
                b_ref[0, slot - N_A_SLOTS] = val * q_scale if slot - N_A_SLOTS < N_PAIRS else val
            else:
                g_ref[0, slot - N_A_SLOTS - N_B_SLOTS] = _silu(val).astype(BF16)


def _inproj_even(x, norm_w, w):
    n_z = AB_WIDTH // LANES
    slots = lambda n: pl.BlockSpec((1, n, ROW_TILE, LANES), lambda b, i: (b, 0, i, 0))
    out = lambda n, dtype: jax.ShapeDtypeStruct((BATCH, n, SEQ, LANES), dtype)
    return pl.pallas_call(
        _inproj_even_kernel,
        grid=(BATCH, SEQ // ROW_TILE),
        in_specs=[
            pl.BlockSpec((1, ROW_TILE, D_MODEL), lambda b, i: (b, i, 0)),
            pl.BlockSpec((1, D_MODEL), lambda b, i: (0, 0)),
            pl.BlockSpec(w.shape, lambda b, i: (0, 0)),
        ],
        out_specs=[slots(N_A_SLOTS), slots(N_B_SLOTS), slots(n_z)],
        out_shape=[out(N_A_SLOTS, BF16), out(N_B_SLOTS, F32), out(n_z, BF16)],
        compiler_params=_params(2),
        name="inproj_even",
    )(x, norm_w.reshape(1, D_MODEL), w)


def _outproj_kernel(*refs, n_g, final):
    g_refs, (w_ref, x_ref), rest = refs[:n_g], refs[n_g:n_g + 2], refs[n_g + 2:]
    g = jnp.concatenate([g_ref[0, p] for g_ref in g_refs for p in range(g_ref.shape[1])], axis=1)
    y = x_ref[0] + jnp.dot(g, w_ref[...], preferred_element_type=F32)
    if final:
        fw_ref, o_ref = rest
        y = _rms(y, fw_ref[...])
    else:
        (o_ref,) = rest
    o_ref[0] = y


def _outproj(gs, w, x, final_w=None):
    final = final_w is not None
    in_specs = [pl.BlockSpec((1, g.shape[1], ROW_TILE, LANES), lambda b, i: (b, 0, i, 0)) for g in gs] + [
        pl.BlockSpec(w.shape, lambda b, i: (0, 0)),
        pl.BlockSpec((1, ROW_TILE, D_MODEL), lambda b, i: (b, i, 0)),
    ]
    args = [*gs, w, x]
    if final:
        in_specs.append(pl.BlockSpec((1, D_MODEL), lambda b, i: (0, 0)))
        args.append(final_w.reshape(1, D_MODEL))
    return pl.pallas_call(
        functools.partial(_outproj_kernel, n_g=len(gs), final=final),
        grid=(BATCH, SEQ // ROW_TILE),
        in_specs=in_specs,
        out_specs=pl.BlockSpec((1, ROW_TILE, D_MODEL), lambda b, i: (b, i, 0)),
        out_shape=jax.ShapeDtypeStruct((BATCH, SEQ, D_MODEL), F32),
        compiler_params=_params(2),
        name="outproj_final" if final else "outproj",
    )(*args)


def _stack_heads(q):
    lane = lax.broadcasted_iota(jnp.int32, (1, LANES), 1)
    keep0 = jnp.where(lane < HEAD_DIM, 1.0, 0.0).astype(q.dtype)
    return jnp.concatenate([q * keep0, q * (1.0 - keep0).astype(q.dtype)], axis=0)


def _head_ones():
    row = lax.broadcasted_iota(jnp.int32, (LANES, 2 * LANES), 0)
    col = lax.broadcasted_iota(jnp.int32, (LANES, 2 * LANES), 1)
    return jnp.where((row < HEAD_DIM) == (col < LANES), 1.0, 0.0).astype(BF16)


def _max_key_norms(load_k):
    head_ones = _head_ones()

    def body(c, k2):
        k = load_k(pl.ds(pl.multiple_of(c * ROW_TILE, ROW_TILE), ROW_TILE))
        sq = jnp.dot(k * k, head_ones, preferred_element_type=F32)
        return jnp.maximum(k2, jnp.max(sq, axis=0, keepdims=True))

    return jnp.sqrt(lax.fori_loop(0, SEQ // ROW_TILE, body, jnp.zeros((1, 2 * LANES), F32)))


def _score_bounds(q, k_max, extra):
    sq = jnp.dot(q * q, _head_ones(), preferred_element_type=F32)
    bound = sq * lax.rsqrt(sq + NORM_FLOOR) * k_max + extra
    return jnp.concatenate([bound[:, :LANES], bound[:, LANES:]], axis=0)


def _fast_unit(q, k, v, bias, shift, low):
    r = q.shape[0]
    p = jnp.exp2(_dot_nt(_stack_heads(q), k) + bias - _rep(shift, k.shape[0]))
    sums = jnp.sum(p, axis=-1, keepdims=True)
    pv = jnp.dot(p.astype(BF16), v, preferred_element_type=F32)
    return jnp.where(low, pv[:r], pv[r:]), sums[:r], sums[r:]


def _sweep_blocks(n_blocks, block_fn, carry):
    carry = block_fn(0, 0, BLOCK, carry)

    def body(i, c):
        return block_fn(pl.multiple_of(i * BLOCK, BLOCK), pl.multiple_of((i - 1) * BLOCK, BLOCK), 2 * BLOCK, c)

    return lax.fori_loop(1, n_blocks, body, carry, unroll=BAND_UNROLL)


def _att_a_kernel(q_ref, k_ref, v_ref, g_ref, bias_ref, stat_ref, o_ref):
    low = _low_half(BLOCK)
    sink = jnp.concatenate([jnp.broadcast_to(stat_ref[0, hh, 0:1, :], (BLOCK, LANES)) for hh in range(2)], axis=0)
    bias_max = jnp.concatenate([stat_ref[0, hh, 1:2, :] for hh in range(2)], axis=1)
    k_max = _max_key_norms(lambda rows: k_ref[0, 0, rows, :])

    def operands(qs, ks, n_keys):
        rows, keys = pl.ds(qs, BLOCK), pl.ds(ks, n_keys)
        return rows, q_ref[0, 0, rows, :], k_ref[0, 0, keys, :], v_ref[0, 0, keys, :], \
            bias_ref[0, :, 2 * BLOCK - n_keys:], g_ref[0, 0, rows, :].astype(F32)

    def fast_block(qs, ks, n_keys, l_min):
        rows, q, k, v, bias, gate = operands(qs, ks, n_keys)
        shift = jnp.maximum(_score_bounds(q, k_max, bias_max), sink)
        pv, sum0, sum1 = _fast_unit(q, k, v, bias, shift, low)
        sink_term = jnp.exp2(sink - shift)
        l0, l1 = sum0 + sink_term[:BLOCK], sum1 + sink_term[BLOCK:]
        o_ref[0, 0, rows, :] = (pv / jnp.where(low, l0, l1) * gate).astype(BF16)
        return jnp.minimum(l_min, jnp.minimum(l0, l1))

    def exact_block(qs, ks, n_keys, carry):
        rows, q, k, v, bias, gate = operands(qs, ks, n_keys)
        s = _dot_nt(_stack_heads(q), k) + bias
        m = jnp.maximum(jnp.max(s, axis=-1, keepdims=True), sink)
        p = jnp.exp2(s - _rep(m, n_keys))
        denom = jnp.sum(p, axis=-1, keepdims=True) + jnp.exp2(sink - m)
        o = jnp.dot(p.astype(BF16), v, preferred_element_type=F32) / denom
        o_ref[0, 0, rows, :] = (jnp.where(low, o[:BLOCK], o[BLOCK:]) * gate).astype(BF16)
        return carry

    l_min = _sweep_blocks(SEQ // BLOCK, fast_block, jnp.full((BLOCK, LANES), jnp.inf, F32))

    @pl.when(jnp.logical_not(jnp.min(l_min) >= UNDERFLOW_GUARD))
    def _():
        _sweep_blocks(SEQ // BLOCK, exact_block, 0)


def _att_a(qkv, z, bias, stats):
    seq_spec = lambda slot_of: pl.BlockSpec((1, 1, SEQ, LANES), slot_of)
    return pl.pallas_call(
        _att_a_kernel,
        grid=(BATCH, N_PAIRS),
        in_specs=[
            seq_spec(lambda b, p: (b, p, 0, 0)),
            seq_spec(lambda b, p: (b, KA_SLOT, 0, 0)),
            seq_spec(lambda b, p: (b, VA_SLOT, 0, 0)),
            seq_spec(lambda b, p: (b, p, 0, 0)),
            pl.BlockSpec((1, 2 * BLOCK, 2 * BLOCK), lambda b, p: (p, 0, 0)),
            pl.BlockSpec((1, 2, 8, LANES), lambda b, p: (p, 0, 0, 0)),
        ],
        out_specs=seq_spec(lambda b, p: (b, p, 0, 0)),
        out_shape=jax.ShapeDtypeStruct((BATCH, N_PAIRS, SEQ, LANES), BF16),
        compiler_params=_params(2),
        name="att_swa",
    )(qkv, qkv, qkv, z, bias, stats)


def _att_b_kernel(q_ref, k_ref, v_ref, g_ref, b1_ref, b4_ref, b16_ref, stat_ref, o_ref,
                  m0_ref, m1_ref, l0_ref, l1_ref, acc_ref):
    low = _low_half(BLOCK)
    load = lambda ref, rows: ref[0, 0, rows, :].astype(BF16)
    bias_max = jnp.concatenate([stat_ref[0, hh, 1:2, :] for hh in range(2)], axis=1)
    k_max = _max_key_norms(lambda rows: load(k_ref, rows))

    def sweep(unit):
        def block1(qs, ks, n_keys, carry):
            rows, keys = pl.ds(qs, BLOCK), pl.ds(ks, n_keys)
            unit(rows, load(q_ref, rows), load(k_ref, keys), load(v_ref, keys),
                 b1_ref[0, :, 2 * BLOCK - n_keys:], True)
            return carry

        _sweep_blocks(SEQ // BLOCK, block1, 0)

        def residue4(f, carry):
            def block4(us, ks, n_keys, c):
                rows, keys = pl.ds(us * 4 + f, BLOCK, stride=4), pl.ds(ks * 4 + f, n_keys, stride=4)
                unit(rows, load(q_ref, rows), load(k_ref, keys), load(v_ref, keys),
                     b4_ref[0, :, 2 * BLOCK - n_keys:], False)
                return c

            return _sweep_blocks(SEQ // 4 // BLOCK, block4, carry)

        lax.fori_loop(0, 4, residue4, 0)

        def residue16(r, carry):
            rows = pl.ds(r, SEQ // 16, stride=16)
            unit(rows, load(q_ref, rows), load(k_ref, rows), load(v_ref, rows), b16_ref[0], False)
            return carry

        lax.fori_loop(0, 16, residue16, 0, unroll=2)

    def fast_unit(rows, q, k, v, bias, first):
        r = q.shape[0]
        if first:
            shift = _score_bounds(q, k_max, bias_max)
            m0_ref[rows, :] = shift[:r]
            m1_ref[rows, :] = shift[r:]
        else:
            shift = jnp.concatenate([m0_ref[rows, :], m1_ref[rows, :]], axis=0)
        pv, sum0, sum1 = _fast_unit(q, k, v, bias, shift, low if r == BLOCK else _low_half(r))
        if first:
            acc_ref[rows, :] = pv
            l0_ref[rows, :] = jnp.broadcast_to(sum0, (r, LANES))
            l1_ref[rows, :] = jnp.broadcast_to(sum1, (r, LANES))
        else:
            acc_ref[rows, :] += pv
            l0_ref[rows, :] += sum0
            l1_ref[rows, :] += sum1

    def exact_unit(rows, q, k, v, bias, first):
        r, n_keys = q.shape[0], k.shape[0]
        low_r = low if r == BLOCK else _low_half(r)
        s = _dot_nt(_stack_heads(q), k) + bias
        m_new = jnp.broadcast_to(jnp.max(s, axis=-1, keepdims=True), (2 * r, LANES))
        if not first:
            m_old = jnp.concatenate([m0_ref[rows, :], m1_ref[rows, :]], axis=0)
            m_new = jnp.maximum(m_old, m_new)
            alpha = jnp.exp2(m_old - m_new)
        p = jnp.exp2(s - _rep(m_new, n_keys))
        sums = jnp.sum(p, axis=-1, keepdims=True)
        pv = jnp.dot(p.astype(BF16), v, preferred_element_type=F32)
        pv = jnp.where(low_r, pv[:r], pv[r:])
        m0_ref[rows, :] = m_new[:r]
        m1_ref[rows, :] = m_new[r:]
        if first:
            acc_ref[rows, :] = pv
            l0_ref[rows, :] = jnp.broadcast_to(sums[:r], (r, LANES))
            l1_ref[rows, :] = jnp.broadcast_to(sums[r:], (r, LANES))
        else:
            acc_ref[rows, :] = jnp.where(low_r, alpha[:r], alpha[r:]) * acc_ref[rows, :] + pv
            l0_ref[rows, :] = alpha[:r] * l0_ref[rows, :] + sums[:r]
            l1_ref[rows, :] = alpha[r:] * l1_ref[rows, :] + sums[r:]

    def finish(i, l_min):
        rows = pl.ds(pl.multiple_of(i * BLOCK, BLOCK), BLOCK)
        l0, l1 = l0_ref[rows, :], l1_ref[rows, :]
        gate = g_ref[0, 0, rows, :].astype(F32)
        o_ref[0, 0, rows, :] = (acc_ref[rows, :] / jnp.where(low, l0, l1) * gate).astype(BF16)
        return jnp.minimum(l_min, jnp.minimum(l0, l1))

    no_min = jnp.full((BLOCK, LANES), jnp.inf, F32)
    sweep(fast_unit)
    l_min = lax.fori_loop(0, SEQ // BLOCK, finish, no_min, unroll=BAND_UNROLL)

    @pl.when(jnp.logical_not(jnp.min(l_min) >= UNDERFLOW_GUARD))
    def _():
        sweep(exact_unit)
        lax.fori_loop(0, SEQ // BLOCK, finish, no_min)


def _att_b(qkv, z, b1, b4, b16, stats):
    seq_spec = lambda slot: pl.BlockSpec((1, 1, SEQ, LANES), lambda b, p: (b, slot + p, 0, 0))
    bias_spec = lambda a: pl.BlockSpec((1,) + a.shape[1:], lambda b, p: (p, 0, 0))
    return pl.pallas_call(
        _att_b_kernel,
        grid=(BATCH, N_PAIRS),
        in_specs=[seq_spec(QB_SLOT), seq_spec(KB_SLOT), seq_spec(VB_SLOT), seq_spec(N_PAIRS),
                  bias_spec(b1), bias_spec(b4), bias_spec(b16),
                  pl.BlockSpec((1, 2, 8, LANES), lambda b, p: (p, 0, 0, 0))],
        out_specs=seq_spec(0),
        out_shape=jax.ShapeDtypeStruct((BATCH, N_PAIRS, SEQ, LANES), BF16),
        scratch_shapes=[pltpu.VMEM((SEQ, LANES), F32)] * 5,
        compiler_params=_params(2),
        name="att_dilated",
    )(qkv, qkv, qkv, z, b1, b4, b16, stats)


def _rope_table_kernel(pos_ref, freq_ref, cos_ref, sin_ref):
    ang = pos_ref[0].astype(F32) * freq_ref[...]
    cos_ref[0] = jnp.cos(ang)
    sin_ref[0] = jnp.sin(ang)


def _rope_tables(positions):
    inv_freq = ROPE_THETA ** (-jnp.arange(0, C_ROPE, 2, dtype=F32) / C_ROPE)
    lane = np.arange(LANES)
    freq = jnp.where(jnp.asarray(lane >= C_NOPE), inv_freq[lane % (C_ROPE // 2)], 0.0).reshape(1, LANES)
    table = jax.ShapeDtypeStruct((BATCH, SEQ, LANES), F32)
    spec = pl.BlockSpec((1, ROW_TILE, LANES), lambda b, i: (b, i, 0))
    return pl.pallas_call(
        _rope_table_kernel,
        grid=(BATCH, SEQ // ROW_TILE),
        in_specs=[pl.BlockSpec((1, ROW_TILE, 1), lambda b, i: (b, i, 0)),
                  pl.BlockSpec((1, LANES), lambda b, i: (0, 0))],
        out_specs=[spec, spec],
        out_shape=[table, table],
        compiler_params=_params(2),
        name="rope_tables",
    )(positions.reshape(BATCH, SEQ, 1), freq)


C_LOWRANK = C_Q_RANK + C_KV_RANK + 2 * LANES


def _inproj_odd_kernel(x_ref, nw_ref, w_ref, qnw_ref, wq_ref, kvnw_ref, wk_ref, wv_ref, cos_ref, sin_ref,
                       q_ref, k_ref, v_ref, g_ref):
    hb = _rms(x_ref[0], nw_ref[...]).astype(BF16)
    c = jnp.dot(hb, w_ref[:, :C_LOWRANK], preferred_element_type=F32)
    for j in range(C_WIDTH // (2 * LANES)):
        lo = C_LOWRANK + j * 2 * LANES
        r = jnp.dot(hb, w_ref[:, lo:lo + 2 * LANES], preferred_element_type=F32)
        for t in range(2):
            g_ref[0, 2 * j + t] = _silu(r[:, t * LANES:(t + 1) * LANES]).astype(BF16)

    qn = _rms(c[:, :C_Q_RANK], qnw_ref[...]).astype(BF16)
    kvn = _rms(c[:, C_Q_RANK:C_Q_RANK + C_KV_RANK], kvnw_ref[...]).astype(BF16)
    k_pe = c[:, C_Q_RANK + C_KV_RANK:C_Q_RANK + C_KV_RANK + LANES]
    k_pe_rot = c[:, C_Q_RANK + C_KV_RANK + LANES:]
    cos, sin = cos_ref[0], sin_ref[0]
    lane = lax.broadcasted_iota(jnp.int32, cos.shape, 1)
    q_table = jnp.where(lane < C_NOPE + C_ROPE, cos, sin) * ((C_NOPE + C_ROPE) ** -0.5 * LOG2E)
    k_rope = k_pe * cos + k_pe_rot * sin

    for j in range(C_HEADS // 2):
        cols = slice(j * 2 * LANES, (j + 1) * 2 * LANES)
        rq = jnp.dot(qn, wq_ref[:, cols], preferred_element_type=F32)
        rk = jnp.dot(kvn, wk_ref[:, cols], preferred_element_type=F32)
        for t in range(2):
            lanes = slice(t * LANES, (t + 1) * LANES)
            q_ref[0, 2 * j + t] = (rq[:, lanes] * q_table).astype(BF16)
            k_ref[0, 2 * j + t] = (rk[:, lanes] + k_rope).astype(BF16)
    for j in range(C_WIDTH // (2 * LANES)):
        rv = jnp.dot(kvn, wv_ref[:, j * 2 * LANES:(j + 1) * 2 * LANES], preferred_element_type=F32)
        for t in range(2):
            v_ref[0, 2 * j + t] = rv[:, t * LANES:(t + 1) * LANES].astype(BF16)


def _inproj_odd(x, norm_w, w, q_norm, wq, kv_norm, wk, wv, cos, sin):
    full = lambda a: pl.BlockSpec(a.shape, lambda b, i: (0,) * a.ndim)
    row = lambda width: pl.BlockSpec((1, ROW_TILE, width), lambda b, i: (b, i, 0))
    slots = lambda n: pl.BlockSpec((1, n, ROW_TILE, LANES), lambda b, i: (b, 0, i, 0))
    out = lambda n: jax.ShapeDtypeStruct((BATCH, n, SEQ, LANES), BF16)
    n_pairs = C_WIDTH // LANES
    vecs = [norm_w.reshape(1, D_MODEL), q_norm.reshape(1, C_Q_RANK), kv_norm.reshape(1, C_KV_RANK)]
    return pl.pallas_call(
        _inproj_odd_kernel,
        grid=(BATCH, SEQ // ROW_TILE),
        in_specs=[row(D_MODEL), full(vecs[0]), full(w), full(vecs[1]), full(wq), full(vecs[2]), full(wk), full(wv),
                  row(LANES), row(LANES)],
        out_specs=[slots(C_HEADS), slots(C_HEADS), slots(n_pairs), slots(n_pairs)],
        out_shape=[out(C_HEADS), out(C_HEADS), out(n_pairs), out(n_pairs)],
        compiler_params=_params(2),
        name="inproj_mla",
    )(x, vecs[0], w, vecs[1], wq, vecs[2], wk, wv, cos, sin)


V_ROWS = 80


def _flash_kernel(q_ref, k_ref, v_ref, g_ref, o_ref, vt_ref, kmax_ref, mrow_ref, ot_ref,
                  m0_ref, m1_ref, l0_ref, l1_ref, acc_ref):
    t = FLASH_TILE
    qi = pl.program_id(2)
    low = _low_half(t)
    m_refs, l_refs = (m0_ref, m1_ref), (l0_ref, l1_ref)

    @pl.when(qi == 0)
    def _():
        row = lax.broadcasted_iota(jnp.int32, (V_ROWS - HEAD_DIM, t), 0)
        ones_row = jnp.where(row == 0, 1.0, 0.0).astype(BF16)

        def prep_body(c, carry):
            rows = pl.ds(pl.multiple_of(c * t, t), t)
            v_t = v_ref[0, 0, rows, :].astype(F32).T
            out = []
            for hh in range(2):
                vt_ref[hh, :HEAD_DIM, rows] = v_t[hh * HEAD_DIM:(hh + 1) * HEAD_DIM].astype(BF16)
                vt_ref[hh, HEAD_DIM:, rows] = ones_row
                kf = k_ref[0, hh, rows, :].astype(F32)
                sq = jnp.sum(kf * kf, axis=-1, keepdims=True)
                out.append(jnp.maximum(carry[hh], jnp.max(sq, axis=0, keepdims=True)))
            return tuple(out)

        zero = jnp.zeros((1, 1), F32)
        k2 = lax.fori_loop(0, SEQ // t, prep_body, (zero, zero))
        for hh in range(2):
            kmax_ref[hh] = jnp.broadcast_to(jnp.sqrt(k2[hh]), kmax_ref.shape[1:])

    for hh in range(2):
        qf = q_ref[0, hh].astype(F32)
        q_norm = jnp.sqrt(jnp.sum(qf * qf, axis=-1, keepdims=True))
        bound = jnp.broadcast_to(q_norm * kmax_ref[hh][:1, :], (t, LANES))
        mrow_ref[hh] = bound.T[:8, :]
    ot_ref[...] = jnp.zeros(ot_ref.shape, F32)

    def fast_step(ks, width, diagonal=False):
        for hh in range(2):
            s_t = _dot_nt(k_ref[0, hh, pl.ds(ks, width), :], q_ref[0, hh])
            if diagonal:
                causal = lax.broadcasted_iota(jnp.int32, (t, t), 0) <= lax.broadcasted_iota(jnp.int32, (t, t), 1)
                s_t = jnp.where(causal, s_t, NEG_INF)
            p_t = jnp.exp2(s_t - mrow_ref[hh, :1, :]).astype(BF16)
            ot_ref[hh] += jnp.dot(vt_ref[hh, :, pl.ds(ks, width)], p_t, preferred_element_type=F32)

    def fast_body(j, carry):
        fast_step(pl.multiple_of(j * 2 * t, 2 * t), 2 * t)
        return carry

    lax.fori_loop(0, qi // 2, fast_body, 0)

    @pl.when(qi % 2 == 1)
    def _():
        fast_step(pl.multiple_of((qi - 1) * t, t), t)

    fast_step(pl.multiple_of(qi * t, t), t, diagonal=True)

    sums = [ot_ref[hh, HEAD_DIM:HEAD_DIM + 1, :] for hh in range(2)]
    healthy = jnp.minimum(jnp.min(sums[0]), jnp.min(sums[1])) >= UNDERFLOW_GUARD
    gate = g_ref[0, 0].astype(F32)

    @pl.when(healthy)
    def _():
        o_t = jnp.concatenate([ot_ref[hh, :HEAD_DIM, :] / sums[hh] for hh in range(2)], axis=0)
        o_ref[0, 0] = (o_t.T * gate).astype(BF16)

    def exact_step(j, diagonal):
        ks = pl.multiple_of(j * t, t)
        v = v_ref[0, 0, pl.ds(ks, t), :]
        pvs, alphas = [], []
        for hh in range(2):
            s = _dot_nt(q_ref[0, hh], k_ref[0, hh, pl.ds(ks, t), :])
            if diagonal:
                causal = lax.broadcasted_iota(jnp.int32, (t, t), 1) <= lax.broadcasted_iota(jnp.int32, (t, t), 0)
                s = jnp.where(causal, s, NEG_INF)
            m_old = m_refs[hh][...]
            m_new = jnp.maximum(m_old, jnp.max(s, axis=-1, keepdims=True))
            alpha = jnp.exp2(m_old - m_new)
            p = jnp.exp2(s - _rep(m_new, t))
            m_refs[hh][...] = m_new
            l_refs[hh][...] = alpha * l_refs[hh][...] + jnp.sum(p, axis=-1, keepdims=True)
            pvs.append(jnp.dot(p.astype(BF16), v, preferred_element_type=F32))
            alphas.append(alpha)
        acc_ref[...] = jnp.where(low, alphas[0], alphas[1]) * acc_ref[...] + jnp.where(low, pvs[0], pvs[1])

    @pl.when(jnp.logical_not(healthy))
    def _():
        for hh in range(2):
            m_refs[hh][...] = jnp.full((t, LANES), NEG_INF, F32)
            l_refs[hh][...] = jnp.zeros((t, LANES), F32)
        acc_ref[...] = jnp.zeros((t, LANES), F32)

        def exact_body(j, carry):
            exact_step(j, False)
            return carry

        lax.fori_loop(0, qi, exact_body, 0)
        exact_step(qi, True)
        o = acc_ref[...] / jnp.where(low, l0_ref[...], l1_ref[...])
        o_ref[0, 0] = (o * gate).astype(BF16)


def _flash(q, k, v, z):
    t = FLASH_TILE
    n_pairs = C_WIDTH // LANES
    tile = pl.BlockSpec((1, 1, t, LANES), lambda b, p, i: (b, p, i, 0))
    return pl.pallas_call(
        _flash_kernel,
        grid=(BATCH, n_pairs, SEQ // t),
        in_specs=[
            pl.BlockSpec((1, 2, t, LANES), lambda b, p, i: (b, p, i, 0)),
            pl.BlockSpec((1, 2, SEQ, LANES), lambda b, p, i: (b, p, 0, 0)),
            pl.BlockSpec((1, 1, SEQ, LANES), lambda b, p, i: (b, p, 0, 0)),
            tile,
        ],
        out_specs=tile,
        out_shape=jax.ShapeDtypeStruct((BATCH, n_pairs, SEQ, LANES), BF16),
        scratch_shapes=[pltpu.VMEM((2, V_ROWS, SEQ), BF16), pltpu.VMEM((2, 8, LANES), F32),
                        pltpu.VMEM((2, 8, t), F32), pltpu.VMEM((2, V_ROWS, t), F32)]
        + [pltpu.VMEM((t, LANES), F32)] * 5,
        compiler_params=_params(3),
        name="mla_flash",
    )(q, k, v, z)


def _even_weights(w_in, w_out):
    perm = np.concatenate([np.arange(h * HEAD_DIM, (h + 1) * HEAD_DIM) for h in A_PAIR_HEADS])
    splits = np.cumsum([A_WIDTH, A_KV_WIDTH, A_KV_WIDTH, B_WIDTH, B_WIDTH, B_WIDTH])
    qa, ka, va, qb, kb, vb, z = jnp.split(w_in, splits, axis=1)
    w = jnp.concatenate([qa[:, perm], ka, va, qb, kb, vb, z[:, :A_WIDTH][:, perm], z[:, A_WIDTH:]],
                        axis=1).astype(BF16)
    w_o = jnp.concatenate([w_out[:A_WIDTH][perm], w_out[A_WIDTH:]], axis=0).astype(BF16)
    return w, w_o


def _rot_half_cols(w):
    half = w.shape[1] // 2
    return jnp.concatenate([-w[:, half:], w[:, :half]], axis=1)


def _odd_weights(w_in, w_qb, w_kvb):
    splits = np.cumsum([C_Q_RANK, C_KV_RANK, C_ROPE])
    w_cq, w_ckv, w_kpe, w_z = jnp.split(w_in, splits, axis=1)
    pad = jnp.zeros((D_MODEL, C_NOPE), w_in.dtype)
    w_kpe_rot = _rot_half_cols(w_kpe)
    w = jnp.concatenate([w_cq, w_ckv, pad, w_kpe, w_kpe, pad, w_kpe_rot, w_kpe_rot, w_z], axis=1).astype(BF16)

    wq = w_qb.reshape(C_Q_RANK, C_HEADS, C_NOPE + C_ROPE)
    wq_pe = wq[:, :, C_NOPE:]
    wq_rot = jnp.concatenate([-wq_pe[:, :, C_ROPE // 2:], wq_pe[:, :, :C_ROPE // 2]], axis=2)
    wq = jnp.concatenate([wq, wq_rot], axis=2).reshape(C_Q_RANK, C_HEADS * LANES).astype(BF16)

    wkv = w_kvb.reshape(C_KV_RANK, C_HEADS, C_NOPE + C_V)
    wk = jnp.concatenate([wkv[:, :, :C_NOPE], jnp.zeros((C_KV_RANK, C_HEADS, LANES - C_NOPE), w_kvb.dtype)], axis=2)
    wk = wk.reshape(C_KV_RANK, C_HEADS * LANES).astype(BF16)
    wv = wkv[:, :, C_NOPE:].reshape(C_KV_RANK, C_WIDTH).astype(BF16)
    return w, wq, wk, wv


def kernel(x, positions, norm_w, rel_bias, ab_w_in, ab_sinks, ab_w_out, c_w_in, c_q_norm, c_w_qb, c_kv_norm,
           c_w_kvb, c_w_out, final_norm):
    bias_a = _bias_tiles(rel_bias, A_PAIR_HEADS, 1, BLOCK, 2 * BLOCK, BLOCK)
    bias_b1 = _bias_tiles(rel_bias, B_PAIR_HEADS, 1, BLOCK, 2 * BLOCK, BLOCK)
    bias_b4 = _bias_tiles(rel_bias, B_PAIR_HEADS, 4, BLOCK, 2 * BLOCK, BLOCK)
    bias_b16 = _bias_tiles(rel_bias, B_PAIR_HEADS, 16, SEQ // 16, SEQ // 16, 0)
    stats_b = _head_stats(rel_bias, B_PAIR_HEADS)
    cos, sin = _rope_tables(positions)

    for layer in range(DEPTH):
        i = layer // 2
        last = final_norm if layer == DEPTH - 1 else None
        if layer % 2 == 0:
            w, w_o = _even_weights(ab_w_in[i], ab_w_out[i])
            qkv_a, qkv_b, z = _inproj_even(x, norm_w[layer], w)
            gs = [_att_a(qkv_a, z, bias_a, _head_stats(rel_bias, A_PAIR_HEADS, ab_sinks[i])),
                  _att_b(qkv_b, z, bias_b1, bias_b4, bias_b16, stats_b)]
        else:
            w, wq, wk, wv = _odd_weights(c_w_in[i], c_w_qb[i], c_w_kvb[i])
            q, k, v, z = _inproj_odd(x, norm_w[layer], w, c_q_norm[i], wq, c_kv_norm[i], wk, wv, cos, sin)
            gs = [_flash(q, k, v, z)]
            w_o = c_w_out[i].astype(BF16)
        x = _outproj(gs, w_o, x, last)
    return x
```

```python
import functools
import math

import numpy as np
import jax
import jax.numpy as jnp
from jax import lax
from jax.experimental import pallas as pl
from jax.experimental.pallas import tpu as pltpu

D_MODEL = 1024
BATCH = 4
SEQ = 4096
DEPTH = 4
HEAD_DIM = 64
BLOCK = 128
A_HEADS = 8
A_KV_HEADS = 2
A_WINDOW = 128
B_HEADS = 8
B_CONFIGS = ((128, 1), (512, 4), (2048, 16))
NUM_BUCKETS = 32
MAX_DISTANCE = 2048
C_HEADS = 16
C_Q_RANK = 256
C_KV_RANK = 128
C_NOPE = 64
C_ROPE = 32
C_V = 64
ROPE_THETA = 10000.0
EPS = 1e-6

A_WIDTH = A_HEADS * HEAD_DIM
A_KV_WIDTH = A_KV_HEADS * HEAD_DIM
B_WIDTH = B_HEADS * HEAD_DIM
AB_WIDTH = A_WIDTH + B_WIDTH
C_WIDTH = C_HEADS * C_V

LANES = 128
F32 = jnp.float32
BF16 = jnp.bfloat16
NEG_INF = float("-inf")
LOG2E = math.log2(math.e)
UNDERFLOW_GUARD = 2.0 ** -100
NORM_FLOOR = 1e-30

ROW_TILE = 512
FLASH_TILE = 1024
VMEM_LIMIT = 48 * 1024 * 1024

N_PAIRS = 4
KA_SLOT, VA_SLOT, N_A_SLOTS = 4, 5, 6
QB_SLOT, KB_SLOT, VB_SLOT, N_B_SLOTS = 0, 4, 8, 12
BAND_UNROLL = 8
A_PAIR_HEADS = [h for p in range(N_PAIRS) for h in (p, p + N_PAIRS)]
B_PAIR_HEADS = list(range(A_HEADS, A_HEADS + B_HEADS))


def _params(n_axes):
    return pltpu.CompilerParams(dimension_semantics=("arbitrary",) * n_axes,
                                vmem_limit_bytes=VMEM_LIMIT)


def _t5_bucket_np(dist):
    max_exact = NUM_BUCKETS // 2
    d = np.maximum(dist, 1).astype(np.float64)
    large = max_exact + (np.log(d / max_exact) / math.log(MAX_DISTANCE / max_exact)
                         * (NUM_BUCKETS - max_exact)).astype(np.int64)
    large = np.minimum(large, NUM_BUCKETS - 1)
    return np.where(dist < max_exact, dist, large).astype(np.int32)


def _bias_tiles(rel_bias, heads, dilation, rows, cols, offset):
    buckets = _t5_bucket_np(np.arange(BLOCK + 1) * dilation)
    per_head = jnp.stack([rel_bias[:, h] for h in heads]).astype(F32) * LOG2E
    dist = (lax.broadcasted_iota(jnp.int32, (rows, cols), 0) + offset
            - lax.broadcasted_iota(jnp.int32, (rows, cols), 1))
    tiles = jnp.full((len(heads), rows, cols), NEG_INF, F32)
    for bucket in np.unique(buckets):
        first = int(np.argmax(buckets == bucket))
        tiles = jnp.where((dist >= first)[None], per_head[:, int(bucket)][:, None, None], tiles)
    tiles = jnp.where(((dist >= 0) & (dist <= BLOCK))[None], tiles, NEG_INF)
    return tiles.reshape(len(heads) // 2, 2 * rows, cols)


def _head_stats(rel_bias, heads, sinks=None):
    bias_max = jnp.max(rel_bias.astype(F32), axis=0) * LOG2E
    rows = [jnp.zeros((len(heads),), F32) if sinks is None
            else jnp.stack([sinks[h] for h in heads]).astype(F32) * LOG2E,
            jnp.stack([bias_max[h] for h in heads])]
    rows += [jnp.zeros((len(heads),), F32)] * 6
    stats = jnp.stack(rows, axis=1)
    return jnp.broadcast_to(stats[:, :, None], (len(heads), 8, LANES)).reshape(len(heads) // 2, 2, 8, LANES)


def _rms(x, w):
    return x * lax.rsqrt(jnp.mean(x * x, axis=-1, keepdims=True) + EPS) * w


def _rep(stat, width):
    if width == LANES:
        return stat
    return jnp.concatenate([stat] * (width // LANES), axis=1)


def _silu(z):
    return z / (1.0 + jnp.exp(-z))


def _dot_nt(a, b):
    return lax.dot_general(a, b, (((1,), (1,)), ((), ())), preferred_element_type=F32)


def _low_half(rows):
    return lax.broadcasted_iota(jnp.int32, (rows, LANES), 1) < HEAD_DIM


K_SLOTS = (KA_SLOT,) + tuple(N_A_SLOTS + KB_SLOT + p for p in range(N_PAIRS))


def _inproj_even_kernel(x_ref, nw_ref, w_ref, a_ref, b_ref, g_ref, ksq_ref):
    hb = _rms(x_ref[0], nw_ref[...]).astype(BF16)
    q_scale = HEAD_DIM ** -0.5 * LOG2E
    low = _low_half(ROW_TILE)
    n_chunks = w_ref.shape[1] // (2 * LANES)
    for c in range(n_chunks):
        r = jnp.dot(hb, w_ref[:, c * 2 * LANES:(c + 1) * 2 * LANES], preferred_element_type=F32)
        for t in range(2):
            slot = 2 * c + t
            val = r[:, t * LANES:(t + 1) * LANES]
            if slot < N_A_SLOTS:
                a_ref[0, slot] = (val * q_scale if slot < N_PAIRS else val).astype(BF16)
            elif slot < N_A_SLOTS + N_B_SLOTS:
                b_ref[0, slot - N_A_SLOTS] = val * q_scale if slot - N_A_SLOTS < N_PAIRS else val
            else:
                g_ref[0, slot - N_A_SLOTS - N_B_SLOTS] = _silu(val).astype(BF16)
            if slot in K_SLOTS:
                kb = val.astype(BF16).astype(F32)
                sq = kb * kb
                for hh in range(2):
                    head_sq = jnp.sum(jnp.where(low, sq, 0.0) if hh == 0 else jnp.where(low, 0.0, sq),
                                      axis=-1, keepdims=True)
                    lanes = slice((2 * K_SLOTS.index(slot) + hh) * LANES, (2 * K_SLOTS.index(slot) + hh + 1) * LANES)
                    ksq_ref[0, 0, :, lanes] = jnp.broadcast_to(jnp.max(head_sq, axis=0, keepdims=True), (8, LANES))


def _inproj_even(x, norm_w, w):
    n_z = AB_WIDTH // LANES
    slots = lambda n: pl.BlockSpec((1, n, ROW_TILE, LANES), lambda b, i: (b, 0, i, 0))
    out = lambda n, dtype: jax.ShapeDtypeStruct((BATCH, n, SEQ, LANES), dtype)
    return pl.pallas_call(
        _inproj_even_kernel,
        grid=(BATCH, SEQ // ROW_TILE),
        in_specs=[
            pl.BlockSpec((1, ROW_TILE, D_MODEL), lambda b, i: (b, i, 0)),
            pl.BlockSpec((1, D_MODEL), lambda b, i: (0, 0)),
            pl.BlockSpec(w.shape, lambda b, i: (0, 0)),
        ],
        out_specs=[slots(N_A_SLOTS), slots(N_B_SLOTS), slots(n_z),
                   pl.BlockSpec((1, 1, 8, 2 * len(K_SLOTS) * LANES), lambda b, i: (b, i, 0, 0))],
        out_shape=[out(N_A_SLOTS, BF16), out(N_B_SLOTS, F32), out(n_z, BF16),
                   jax.ShapeDtypeStruct((BATCH, SEQ // ROW_TILE, 8, 2 * len(K_SLOTS) * LANES), F32)],
        compiler_params=_params(2),
        name="inproj_even",
    )(x, norm_w.reshape(1, D_MODEL), w)


def _outproj_kernel(*refs, n_g, final):
    g_refs, (w_ref, x_ref), rest = refs[:n_g], refs[n_g:n_g + 2], refs[n_g + 2:]
    g = jnp.concatenate([g_ref[0, p] for g_ref in g_refs for p in range(g_ref.shape[1])], axis=1)
    y = x_ref[0] + jnp.dot(g, w_ref[...], preferred_element_type=F32)
    if final:
        fw_ref, o_ref = rest
        y = _rms(y, fw_ref[...])
    else:
        (o_ref,) = rest
    o_ref[0] = y


def _outproj(gs, w, x, final_w=None):
    final = final_w is not None
    in_specs = [pl.BlockSpec((1, g.shape[1], ROW_TILE, LANES), lambda b, i: (b, 0, i, 0)) for g in gs] + [
        pl.BlockSpec(w.shape, lambda b, i: (0, 0)),
        pl.BlockSpec((1, ROW_TILE, D_MODEL), lambda b, i: (b, i, 0)),
    ]
    args = [*gs, w, x]
    if final:
        in_specs.append(pl.BlockSpec((1, D_MODEL), lambda b, i: (0, 0)))
        args.append(final_w.reshape(1, D_MODEL))
    return pl.pallas_call(
        functools.partial(_outproj_kernel, n_g=len(gs), final=final),
        grid=(BATCH, SEQ // ROW_TILE),
        in_specs=in_specs,
        out_specs=pl.BlockSpec((1, ROW_TILE, D_MODEL), lambda b, i: (b, i, 0)),
        out_shape=jax.ShapeDtypeStruct((BATCH, SEQ, D_MODEL), F32),
        compiler_params=_params(2),
        name="outproj_final" if final else "outproj",
    )(*args)


def _stack_heads(q):
    lane = lax.broadcasted_iota(jnp.int32, (1, LANES), 1)
    keep0 = jnp.where(lane < HEAD_DIM, 1.0, 0.0).astype(q.dtype)
    return jnp.concatenate([q * keep0, q * (1.0 - keep0).astype(q.dtype)], axis=0)


def _head_ones():
    row = lax.broadcasted_iota(jnp.int32, (LANES, 2 * LANES), 0)
    col = lax.broadcasted_iota(jnp.int32, (LANES, 2 * LANES), 1)
    return jnp.where((row < HEAD_DIM) == (col < LANES), 1.0, 0.0).astype(BF16)


def _max_key_norms(ksq_ref):
    k2 = ksq_ref[0, 0]
    for tile in range(1, ksq_ref.shape[1]):
        k2 = jnp.maximum(k2, ksq_ref[0, tile])
    return jnp.sqrt(k2[:1, :])


def _score_bounds(q, k_max, extra):
    sq = jnp.dot(q * q, _head_ones(), preferred_element_type=F32)
    bound = sq * lax.rsqrt(sq + NORM_FLOOR) * k_max + extra
    return jnp.concatenate([bound[:, :LANES], bound[:, LANES:]], axis=0)


def _fast_unit(q, k, v, bias, shift, low):
    r = q.shape[0]
    p = jnp.exp2(_dot_nt(_stack_heads(q), k) + bias - _rep(shift, k.shape[0]))
    sums = jnp.sum(p, axis=-1, keepdims=True)
    pv = jnp.dot(p.astype(BF16), v, preferred_element_type=F32)
    return jnp.where(low, pv[:r], pv[r:]), sums[:r], sums[r:]


def _sweep_blocks(n_blocks, block_fn, carry):
    carry = block_fn(0, 0, BLOCK, carry)

    def body(i, c):
        return block_fn(pl.multiple_of(i * BLOCK, BLOCK), pl.multiple_of((i - 1) * BLOCK, BLOCK), 2 * BLOCK, c)

    return lax.fori_loop(1, n_blocks, body, carry, unroll=BAND_UNROLL)


def _att_a_kernel(q_ref, k_ref, v_ref, g_ref, bias_ref, stat_ref, ksq_ref, o_ref):
    low = _low_half(BLOCK)
    sink = jnp.concatenate([jnp.broadcast_to(stat_ref[0, hh, 0:1, :], (BLOCK, LANES)) for hh in range(2)], axis=0)
    bias_max = jnp.concatenate([stat_ref[0, hh, 1:2, :] for hh in range(2)], axis=1)
    k_max = _max_key_norms(ksq_ref)

    def operands(qs, ks, n_keys):
        rows, keys = pl.ds(qs, BLOCK), pl.ds(ks, n_keys)
        return rows, q_ref[0, 0, rows, :], k_ref[0, 0, keys, :], v_ref[0, 0, keys, :], \
            bias_ref[0, :, 2 * BLOCK - n_keys:], g_ref[0, 0, rows, :].astype(F32)

    def fast_block(qs, ks, n_keys, l_min):
        rows, q, k, v, bias, gate = operands(qs, ks, n_keys)
        shift = jnp.maximum(_score_bounds(q, k_max, bias_max), sink)
        pv, sum0, sum1 = _fast_unit(q, k, v, bias, shift, low)
        sink_term = jnp.exp2(sink - shift)
        l0, l1 = sum0 + sink_term[:BLOCK], sum1 + sink_term[BLOCK:]
        o_ref[0, 0, rows, :] = (pv / jnp.where(low, l0, l1) * gate).astype(BF16)
        return jnp.minimum(l_min, jnp.minimum(l0, l1))

    def exact_block(qs, ks, n_keys, carry):
        rows, q, k, v, bias, gate = operands(qs, ks, n_keys)
        s = _dot_nt(_stack_heads(q), k) + bias
        m = jnp.maximum(jnp.max(s, axis=-1, keepdims=True), sink)
        p = jnp.exp2(s - _rep(m, n_keys))
        denom = jnp.sum(p, axis=-1, keepdims=True) + jnp.exp2(sink - m)
        o = jnp.dot(p.astype(BF16), v, preferred_element_type=F32) / denom
        o_ref[0, 0, rows, :] = (jnp.where(low, o[:BLOCK], o[BLOCK:]) * gate).astype(BF16)
        return carry

    l_min = _sweep_blocks(SEQ // BLOCK, fast_block, jnp.full((BLOCK, LANES), jnp.inf, F32))

    @pl.when(jnp.logical_not(jnp.min(l_min) >= UNDERFLOW_GUARD))
    def _():
        _sweep_blocks(SEQ // BLOCK, exact_block, 0)


def _ksq_spec(slot_of):
    return pl.BlockSpec((1, SEQ // ROW_TILE, 8, 2 * LANES), lambda b, p: (b, 0, 0, slot_of(p)))


def _att_a(qkv, z, bias, stats, ksq):
    seq_spec = lambda slot_of: pl.BlockSpec((1, 1, SEQ, LANES), slot_of)
    return pl.pallas_call(
        _att_a_kernel,
        grid=(BATCH, N_PAIRS),
        in_specs=[
            seq_spec(lambda b, p: (b, p, 0, 0)),
            seq_spec(lambda b, p: (b, KA_SLOT, 0, 0)),
            seq_spec(lambda b, p: (b, VA_SLOT, 0, 0)),
            seq_spec(lambda b, p: (b, p, 0, 0)),
            pl.BlockSpec((1, 2 * BLOCK, 2 * BLOCK), lambda b, p: (p, 0, 0)),
            pl.BlockSpec((1, 2, 8, LANES), lambda b, p: (p, 0, 0, 0)),
            _ksq_spec(lambda p: 0),
        ],
        out_specs=seq_spec(lambda b, p: (b, p, 0, 0)),
        out_shape=jax.ShapeDtypeStruct((BATCH, N_PAIRS, SEQ, LANES), BF16),
        compiler_params=_params(2),
        name="att_swa",
    )(qkv, qkv, qkv, z, bias, stats, ksq)


def _att_b_kernel(q_ref, k_ref, v_ref, g_ref, b1_ref, b4_ref, b16_ref, stat_ref, ksq_ref, o_ref,
                  m0_ref, m1_ref, l0_ref, l1_ref, acc_ref):
    low = _low_half(BLOCK)
    load = lambda ref, rows: ref[0, 0, rows, :].astype(BF16)
    bias_max = jnp.concatenate([stat_ref[0, hh, 1:2, :] for hh in range(2)], axis=1)
    k_max = _max_key_norms(ksq_ref)

    def sweep(unit):
        def block1(qs, ks, n_keys, carry):
            rows, keys = pl.ds(qs, BLOCK), pl.ds(ks, n_keys)
            unit(rows, load(q_ref, rows), load(k_ref, keys), load(v_ref, keys),
                 b1_ref[0, :, 2 * BLOCK - n_keys:], True)
            return carry

        _sweep_blocks(SEQ // BLOCK, block1, 0)

        def residue4(f, carry):
            def block4(us, ks, n_keys, c):
                rows, keys = pl.ds(us * 4 + f, BLOCK, stride=4), pl.ds(ks * 4 + f, n_keys, stride=4)
                unit(rows, load(q_ref, rows), load(k_ref, keys), load(v_ref, keys),
                     b4_ref[0, :, 2 * BLOCK - n_keys:], False)
                return c

            return _sweep_blocks(SEQ // 4 // BLOCK, block4, carry)

        lax.fori_loop(0, 4, residue4, 0)

        def residue16(r, carry):
            rows = pl.ds(r, SEQ // 16, stride=16)
            unit(rows, load(q_ref, rows), load(k_ref, rows), load(v_ref, rows), b16_ref[0], False)
            return carry

        lax.fori_loop(0, 16, residue16, 0, unroll=2)

    def fast_unit(rows, q, k, v, bias, first):
        r = q.shape[0]
        if first:
            shift = _score_bounds(q, k_max, bias_max)
            m0_ref[rows, :] = shift[:r]
            m1_ref[rows, :] = shift[r:]
        else:
            shift = jnp.concatenate([m0_ref[rows, :], m1_ref[rows, :]], axis=0)
        pv, sum0, sum1 = _fast_unit(q, k, v, bias, shift, low if r == BLOCK else _low_half(r))
        if first:
            acc_ref[rows, :] = pv
            l0_ref[rows, :] = jnp.broadcast_to(sum0, (r, LANES))
            l1_ref[rows, :] = jnp.broadcast_to(sum1, (r, LANES))
        else:
            acc_ref[rows, :] += pv
            l0_ref[rows, :] += sum0
            l1_ref[rows, :] += sum1

    def exact_unit(rows, q, k, v, bias, first):
        r, n_keys = q.shape[0], k.shape[0]
        low_r = low if r == BLOCK else _low_half(r)
        s = _dot_nt(_stack_heads(q), k) + bias
        m_new = jnp.broadcast_to(jnp.max(s, axis=-1, keepdims=True), (2 * r, LANES))
        if not first:
            m_old = jnp.concatenate([m0_ref[rows, :], m1_ref[rows, :]], axis=0)
            m_new = jnp.maximum(m_old, m_new)
            alpha = jnp.exp2(m_old - m_new)
        p = jnp.exp2(s - _rep(m_new, n_keys))
        sums = jnp.sum(p, axis=-1, keepdims=True)
        pv = jnp.dot(p.astype(BF16), v, preferred_element_type=F32)
        pv = jnp.where(low_r, pv[:r], pv[r:])
        m0_ref[rows, :] = m_new[:r]
        m1_ref[rows, :] = m_new[r:]
        if first:
            acc_ref[rows, :] = pv
            l0_ref[rows, :] = jnp.broadcast_to(sums[:r], (r, LANES))
            l1_ref[rows, :] = jnp.broadcast_to(sums[r:], (r, LANES))
        else:
            acc_ref[rows, :] = jnp.where(low_r, alpha[:r], alpha[r:]) * acc_ref[rows, :] + pv
            l0_ref[rows, :] = alpha[:r] * l0_ref[rows, :] + sums[:r]
            l1_ref[rows, :] = alpha[r:] * l1_ref[rows, :] + sums[r:]

    def finish(i, l_min):
        rows = pl.ds(pl.multiple_of(i * BLOCK, BLOCK), BLOCK)
        l0, l1 = l0_ref[rows, :], l1_ref[rows, :]
        gate = g_ref[0, 0, rows, :].astype(F32)
        o_ref[0, 0, rows, :] = (acc_ref[rows, :] / jnp.where(low, l0, l1) * gate).astype(BF16)
        return jnp.minimum(l_min, jnp.minimum(l0, l1))

    no_min = jnp.full((BLOCK, LANES), jnp.inf, F32)
    sweep(fast_unit)
    l_min = lax.fori_loop(0, SEQ // BLOCK, finish, no_min, unroll=BAND_UNROLL)

    @pl.when(jnp.logical_not(jnp.min(l_min) >= UNDERFLOW_GUARD))
    def _():
        sweep(exact_unit)
        lax.fori_loop(0, SEQ // BLOCK, finish, no_min)


def _att_b(qkv, z, b1, b4, b16, stats, ksq):
    seq_spec = lambda slot: pl.BlockSpec((1, 1, SEQ, LANES), lambda b, p: (b, slot + p, 0, 0))
    bias_spec = lambda a: pl.BlockSpec((1,) + a.shape[1:], lambda b, p: (p, 0, 0))
    return pl.pallas_call(
        _att_b_kernel,
        grid=(BATCH, N_PAIRS),
        in_specs=[seq_spec(QB_SLOT), seq_spec(KB_SLOT), seq_spec(VB_SLOT), seq_spec(N_PAIRS),
                  bias_spec(b1), bias_spec(b4), bias_spec(b16),
                  pl.BlockSpec((1, 2, 8, LANES), lambda b, p: (p, 0, 0, 0)),
                  _ksq_spec(lambda p: 1 + p)],
        out_specs=seq_spec(0),
        out_shape=jax.ShapeDtypeStruct((BATCH, N_PAIRS, SEQ, LANES), BF16),
        scratch_shapes=[pltpu.VMEM((SEQ, LANES), F32)] * 5,
        compiler_params=_params(2),
        name="att_dilated",
    )(qkv, qkv, qkv, z, b1, b4, b16, stats, ksq)


def _rope_table_kernel(pos_ref, freq_ref, cos_ref, sin_ref):
    ang = pos_ref[0].astype(F32) * freq_ref[...]
    cos_ref[0] = jnp.cos(ang)
    sin_ref[0] = jnp.sin(ang)


def _rope_tables(positions):
    inv_freq = ROPE_THETA ** (-jnp.arange(0, C_ROPE, 2, dtype=F32) / C_ROPE)
    lane = np.arange(LANES)
    freq = jnp.where(jnp.asarray(lane >= C_NOPE), inv_freq[lane % (C_ROPE // 2)], 0.0).reshape(1, LANES)
    table = jax.ShapeDtypeStruct((BATCH, SEQ, LANES), F32)
    spec = pl.BlockSpec((1, ROW_TILE, LANES), lambda b, i: (b, i, 0))
    return pl.pallas_call(
        _rope_table_kernel,
        grid=(BATCH, SEQ // ROW_TILE),
        in_specs=[pl.BlockSpec((1, ROW_TILE, 1), lambda b, i: (b, i, 0)),
                  pl.BlockSpec((1, LANES), lambda b, i: (0, 0))],
        out_specs=[spec, spec],
        out_shape=[table, table],
        compiler_params=_params(2),
        name="rope_tables",
    )(positions.reshape(BATCH, SEQ, 1), freq)


C_LOWRANK = C_Q_RANK + C_KV_RANK + 2 * LANES


def _inproj_odd_kernel(x_ref, nw_ref, w_ref, qnw_ref, wq_ref, kvnw_ref, wk_ref, wv_ref, cos_ref, sin_ref,
                       q_ref, k_ref, v_ref, g_ref):
    hb = _rms(x_ref[0], nw_ref[...]).astype(BF16)
    c = jnp.dot(hb, w_ref[:, :C_LOWRANK], preferred_element_type=F32)
    for j in range(C_WIDTH // (2 * LANES)):
        lo = C_LOWRANK + j * 2 * LANES
        r = jnp.dot(hb, w_ref[:, lo:lo + 2 * LANES], preferred_element_type=F32)
        for t in range(2):
            g_ref[0, 2 * j + t] = _silu(r[:, t * LANES:(t + 1) * LANES]).astype(BF16)

    qn = _rms(c[:, :C_Q_RANK], qnw_ref[...]).astype(BF16)
    kvn = _rms(c[:, C_Q_RANK:C_Q_RANK + C_KV_RANK], kvnw_ref[...]).astype(BF16)
    k_pe = c[:, C_Q_RANK + C_KV_RANK:C_Q_RANK + C_KV_RANK + LANES]
    k_pe_rot = c[:, C_Q_RANK + C_KV_RANK + LANES:]
    cos, sin = cos_ref[0], sin_ref[0]
    lane = lax.broadcasted_iota(jnp.int32, cos.shape, 1)
    q_table = jnp.where(lane < C_NOPE + C_ROPE, cos, sin) * ((C_NOPE + C_ROPE) ** -0.5 * LOG2E)
    k_rope = k_pe * cos + k_pe_rot * sin

    for j in range(C_HEADS // 2):
        cols = slice(j * 2 * LANES, (j + 1) * 2 * LANES)
        rq = jnp.dot(qn, wq_ref[:, cols], preferred_element_type=F32)
        rk = jnp.dot(kvn, wk_ref[:, cols], preferred_element_type=F32)
        for t in range(2):
            lanes = slice(t * LANES, (t + 1) * LANES)
            q_ref[0, 2 * j + t] = (rq[:, lanes] * q_table).astype(BF16)
            k_ref[0, 2 * j + t] = (rk[:, lanes] + k_rope).astype(BF16)
    for j in range(C_WIDTH // (2 * LANES)):
        rv = jnp.dot(kvn, wv_ref[:, j * 2 * LANES:(j + 1) * 2 * LANES], preferred_element_type=F32)
        for t in range(2):
            v_ref[0, 2 * j + t] = rv[:, t * LANES:(t + 1) * LANES].astype(BF16)


def _inproj_odd(x, norm_w, w, q_norm, wq, kv_norm, wk, wv, cos, sin):
    full = lambda a: pl.BlockSpec(a.shape, lambda b, i: (0,) * a.ndim)
    row = lambda width: pl.BlockSpec((1, ROW_TILE, width), lambda b, i: (b, i, 0))
    slots = lambda n: pl.BlockSpec((1, n, ROW_TILE, LANES), lambda b, i: (b, 0, i, 0))
    out = lambda n: jax.ShapeDtypeStruct((BATCH, n, SEQ, LANES), BF16)
    n_pairs = C_WIDTH // LANES
    vecs = [norm_w.reshape(1, D_MODEL), q_norm.reshape(1, C_Q_RANK), kv_norm.reshape(1, C_KV_RANK)]
    return pl.pallas_call(
        _inproj_odd_kernel,
        grid=(BATCH, SEQ // ROW_TILE),
        in_specs=[row(D_MODEL), full(vecs[0]), full(w), full(vecs[1]), full(wq), full(vecs[2]), full(wk), full(wv),
                  row(LANES), row(LANES)],
        out_specs=[slots(C_HEADS), slots(C_HEADS), slots(n_pairs), slots(n_pairs)],
        out_shape=[out(C_HEADS), out(C_HEADS), out(n_pairs), out(n_pairs)],
        compiler_params=_params(2),
        name="inproj_mla",
    )(x, vecs[0], w, vecs[1], wq, vecs[2], wk, wv, cos, sin)


V_ROWS = 80


def _flash_kernel(q_ref, k_ref, v_ref, g_ref, o_ref, vt_ref, kmax_ref, mrow_ref, ot_ref,
                  m0_ref, m1_ref, l0_ref, l1_ref, acc_ref):
    t, half = FLASH_TILE, FLASH_TILE // 2
    qi = pl.program_id(2)
    low = _low_half(t)
    m_refs, l_refs = (m0_ref, m1_ref), (l0_ref, l1_ref)
    ones = jnp.ones((8, LANES), BF16)

    @pl.when(qi == 0)
    def _():
        row = lax.broadcasted_iota(jnp.int32, (V_ROWS - HEAD_DIM, half), 0)
        ones_row = jnp.where(row == 0, 1.0, 0.0).astype(BF16)

        def prep_body(c, k2):
            rows = pl.ds(pl.multiple_of(c * half, half), half)
            v_t = v_ref[0, 0, rows, :].astype(F32).T
            out = []
            for hh in range(2):
                vt_ref[hh, :HEAD_DIM, rows] = v_t[hh * HEAD_DIM:(hh + 1) * HEAD_DIM].astype(BF16)
                vt_ref[hh, HEAD_DIM:, rows] = ones_row
                k = k_ref[0, hh, rows, :]
                sq = _dot_nt(ones, k * k)
                out.append(jnp.maximum(k2[hh], jnp.max(sq, axis=-1, keepdims=True)))
            return tuple(out)

        zero = jnp.zeros((8, 1), F32)
        k2 = lax.fori_loop(0, SEQ // half, prep_body, (zero, zero))
        for hh in range(2):
            kmax_ref[hh] = jnp.broadcast_to(jnp.sqrt(k2[hh]), kmax_ref.shape[1:])

    for hh in range(2):
        q = q_ref[0, hh]
        sq = _dot_nt(ones, q * q)
        mrow_ref[hh] = sq * lax.rsqrt(sq + NORM_FLOOR) * _rep(kmax_ref[hh], t)
    ot_ref[...] = jnp.zeros(ot_ref.shape, F32)

    def fast_step(ks, width, q_lo=0, q_n=FLASH_TILE, diagonal=False):
        for hh in range(2):
            s_t = _dot_nt(k_ref[0, hh, pl.ds(ks, width), :], q_ref[0, hh, q_lo:q_lo + q_n, :])
            if diagonal:
                causal = (lax.broadcasted_iota(jnp.int32, (width, q_n), 0)
                          <= lax.broadcasted_iota(jnp.int32, (width, q_n), 1))
                s_t = jnp.where(causal, s_t, NEG_INF)
            p_t = jnp.exp2(s_t - mrow_ref[hh, :1, q_lo:q_lo + q_n]).astype(BF16)
            ot_ref[hh, :, q_lo:q_lo + q_n] += jnp.dot(vt_ref[hh, :, pl.ds(ks, width)], p_t,
                                                      preferred_element_type=F32)

    def fast_body(j, carry):
        fast_step(pl.multiple_of(j * t, t), t)
        return carry

    lax.fori_loop(0, qi, fast_body, 0)
    q_start = pl.multiple_of(qi * t, t)
    fast_step(q_start, half, diagonal=True)
    fast_step(q_start + half, half, q_lo=half, q_n=half, diagonal=True)

    sums = [ot_ref[hh, HEAD_DIM:HEAD_DIM + 1, :] for hh in range(2)]
    healthy = jnp.minimum(jnp.min(sums[0]), jnp.min(sums[1])) >= UNDERFLOW_GUARD
    gate = g_ref[0, 0].astype(F32)

    @pl.when(healthy)
    def _():
        o_t = jnp.concatenate([ot_ref[hh, :HEAD_DIM, :] / sums[hh] for hh in range(2)], axis=0)
        o_ref[0, 0] = (o_t.T * gate).astype(BF16)

    def exact_step(j, carry):
        ks = pl.multiple_of(j * half, half)
        v = v_ref[0, 0, pl.ds(ks, half), :]
        visible = (lax.broadcasted_iota(jnp.int32, (t, half), 1) + (ks - q_start)
                   <= lax.broadcasted_iota(jnp.int32, (t, half), 0))
        pvs, alphas = [], []
        for hh in range(2):
            s = jnp.where(visible, _dot_nt(q_ref[0, hh], k_ref[0, hh, pl.ds(ks, half), :]), NEG_INF)
            m_old = m_refs[hh][...]
            m_new = jnp.maximum(m_old, jnp.max(s, axis=-1, keepdims=True))
            alpha = jnp.exp2(m_old - m_new)
            p = jnp.exp2(s - _rep(m_new, half))
            m_refs[hh][...] = m_new
            l_refs[hh][...] = alpha * l_refs[hh][...] + jnp.sum(p, axis=-1, keepdims=True)
            pvs.append(jnp.dot(p.astype(BF16), v, preferred_element_type=F32))
            alphas.append(alpha)
        acc_ref[...] = jnp.where(low, alphas[0], alphas[1]) * acc_ref[...] + jnp.where(low, pvs[0], pvs[1])
        return carry

    @pl.when(jnp.logical_not(healthy))
    def _():
        for hh in range(2):
            m_refs[hh][...] = jnp.full((t, LANES), NEG_INF, F32)
            l_refs[hh][...] = jnp.zeros((t, LANES), F32)
        acc_ref[...] = jnp.zeros((t, LANES), F32)
        lax.fori_loop(0, 2 * qi + 2, exact_step, 0)
        o = acc_ref[...] / jnp.where(low, l0_ref[...], l1_ref[...])
        o_ref[0, 0] = (o * gate).astype(BF16)


def _flash(q, k, v, z):
    t = FLASH_TILE
    n_pairs = C_WIDTH // LANES
    tile = pl.BlockSpec((1, 1, t, LANES), lambda b, p, i: (b, p, i, 0))
    return pl.pallas_call(
        _flash_kernel,
        grid=(BATCH, n_pairs, SEQ // t),
        in_specs=[
            pl.BlockSpec((1, 2, t, LANES), lambda b, p, i: (b, p, i, 0)),
            pl.BlockSpec((1, 2, SEQ, LANES), lambda b, p, i: (b, p, 0, 0)),
            pl.BlockSpec((1, 1, SEQ, LANES), lambda b, p, i: (b, p, 0, 0)),
            tile,
        ],
        out_specs=tile,
        out_shape=jax.ShapeDtypeStruct((BATCH, n_pairs, SEQ, LANES), BF16),
        scratch_shapes=[pltpu.VMEM((2, V_ROWS, SEQ), BF16), pltpu.VMEM((2, 8, LANES), F32),
                        pltpu.VMEM((2, 8, t), F32), pltpu.VMEM((2, V_ROWS, t), F32)]
        + [pltpu.VMEM((t, LANES), F32)] * 5,
        compiler_params=_params(3),
        name="mla_flash",
    )(q, k, v, z)


def _even_weights(w_in, w_out):
    perm = np.concatenate([np.arange(h * HEAD_DIM, (h + 1) * HEAD_DIM) for h in A_PAIR_HEADS])
    splits = np.cumsum([A_WIDTH, A_KV_WIDTH, A_KV_WIDTH, B_WIDTH, B_WIDTH, B_WIDTH])
    qa, ka, va, qb, kb, vb, z = jnp.split(w_in, splits, axis=1)
    w = jnp.concatenate([qa[:, perm], ka, va, qb, kb, vb, z[:, :A_WIDTH][:, perm], z[:, A_WIDTH:]],
                        axis=1).astype(BF16)
    w_o = jnp.concatenate([w_out[:A_WIDTH][perm], w_out[A_WIDTH:]], axis=0).astype(BF16)
    return w, w_o


def _rot_half_cols(w):
    half = w.shape[1] // 2
    return jnp.concatenate([-w[:, half:], w[:, :half]], axis=1)


def _odd_weights(w_in, w_qb, w_kvb):
    splits = np.cumsum([C_Q_RANK, C_KV_RANK, C_ROPE])
    w_cq, w_ckv, w_kpe, w_z = jnp.split(w_in, splits, axis=1)
    pad = jnp.zeros((D_MODEL, C_NOPE), w_in.dtype)
    w_kpe_rot = _rot_half_cols(w_kpe)
    w = jnp.concatenate([w_cq, w_ckv, pad, w_kpe, w_kpe, pad, w_kpe_rot, w_kpe_rot, w_z], axis=1).astype(BF16)

    wq = w_qb.reshape(C_Q_RANK, C_HEADS, C_NOPE + C_ROPE)
    wq_pe = wq[:, :, C_NOPE:]
    wq_rot = jnp.concatenate([-wq_pe[:, :, C_ROPE // 2:], wq_pe[:, :, :C_ROPE // 2]], axis=2)
    wq = jnp.concatenate([wq, wq_rot], axis=2).reshape(C_Q_RANK, C_HEADS * LANES).astype(BF16)

    wkv = w_kvb.reshape(C_KV_RANK, C_HEADS, C_NOPE + C_V)
    wk = jnp.concatenate([wkv[:, :, :C_NOPE], jnp.zeros((C_KV_RANK, C_HEADS, LANES - C_NOPE), w_kvb.dtype)], axis=2)
    wk = wk.reshape(C_KV_RANK, C_HEADS * LANES).astype(BF16)
    wv = wkv[:, :, C_NOPE:].reshape(C_KV_RANK, C_WIDTH).astype(BF16)
    return w, wq, wk, wv


def kernel(x, positions, norm_w, rel_bias, ab_w_in, ab_sinks, ab_w_out, c_w_in, c_q_norm, c_w_qb, c_kv_norm,
           c_w_kvb, c_w_out, final_norm):
    bias_a = _bias_tiles(rel_bias, A_PAIR_HEADS, 1, BLOCK, 2 * BLOCK, BLOCK)
    bias_b1 = _bias_tiles(rel_bias, B_PAIR_HEADS, 1, BLOCK, 2 * BLOCK, BLOCK)
    bias_b4 = _bias_tiles(rel_bias, B_PAIR_HEADS, 4, BLOCK, 2 * BLOCK, BLOCK)
    bias_b16 = _bias_tiles(rel_bias, B_PAIR_HEADS, 16, SEQ // 16, SEQ // 16, 0)
    stats_b = _head_stats(rel_bias, B_PAIR_HEADS)
    cos, sin = _rope_tables(positions)

    for layer in range(DEPTH):
        i = layer // 2
        last = final_norm if layer == DEPTH - 1 else None
        if layer % 2 == 0:
            w, w_o = _even_weights(ab_w_in[i], ab_w_out[i])
            qkv_a, qkv_b, z, ksq = _inproj_even(x, norm_w[layer], w)
            gs = [_att_a(qkv_a, z, bias_a, _head_stats(rel_bias, A_PAIR_HEADS, ab_sinks[i]), ksq),
                  _att_b(qkv_b, z, bias_b1, bias_b4, bias_b16, stats_b, ksq)]
        else:
            w, wq, wk, wv = _odd_weights(c_w_in[i], c_w_qb[i], c_w_kvb[i])
            q, k, v, z = _inproj_odd(x, norm_w[layer], w, c_q_norm[i], wq, c_kv_norm[i], wk, wv, cos, sin)
            gs = [_flash(q, k, v, z)]
            w_o = c_w_out[i].astype(BF16)
        x = _outproj(gs, w_o, x, last)
    return x
```

```python
import functools
import math

import numpy as np
import jax
import jax.numpy as jnp
from jax import lax
from jax.experimental import pallas as pl
from jax.experimental.pallas import tpu as pltpu

D_MODEL = 1024
BATCH = 4
SEQ = 4096
DEPTH = 4
HEAD_DIM = 64
BLOCK = 128
A_HEADS = 8
A_KV_HEADS = 2
A_WINDOW = 128
B_HEADS = 8
B_CONFIGS = ((128, 1), (512, 4), (2048, 16))
NUM_BUCKETS = 32
MAX_DISTANCE = 2048
C_HEADS = 16
C_Q_RANK = 256
C_KV_RANK = 128
C_NOPE = 64
C_ROPE = 32
C_V = 64
ROPE_THETA = 10000.0
EPS = 1e-6

A_WIDTH = A_HEADS * HEAD_DIM
A_KV_WIDTH = A_KV_HEADS * HEAD_DIM
B_WIDTH = B_HEADS * HEAD_DIM
AB_WIDTH = A_WIDTH + B_WIDTH
C_WIDTH = C_HEADS * C_V

LANES = 128
F32 = jnp.float32
BF16 = jnp.bfloat16
NEG_INF = float("-inf")
LOG2E = math.log2(math.e)
UNDERFLOW_GUARD = 2.0 ** -100
NORM_FLOOR = 1e-30

ROW_TILE = 512
FLASH_TILE = 1024
VMEM_LIMIT = 48 * 1024 * 1024

N_PAIRS = 4
KA_SLOT, VA_SLOT, N_A_SLOTS = 4, 5, 6
QB_SLOT, KB_SLOT, VB_SLOT, N_B_SLOTS = 0, 4, 8, 12
BAND_UNROLL = 8
A_PAIR_HEADS = [h for p in range(N_PAIRS) for h in (p, p + N_PAIRS)]
B_PAIR_HEADS = list(range(A_HEADS, A_HEADS + B_HEADS))


def _params(n_axes):
    return pltpu.CompilerParams(dimension_semantics=("arbitrary",) * n_axes,
                                vmem_limit_bytes=VMEM_LIMIT)


def _t5_bucket_np(dist):
    max_exact = NUM_BUCKETS // 2
    d = np.maximum(dist, 1).astype(np.float64)
    large = max_exact + (np.log(d / max_exact) / math.log(MAX_DISTANCE / max_exact)
                         * (NUM_BUCKETS - max_exact)).astype(np.int64)
    large = np.minimum(large, NUM_BUCKETS - 1)
    return np.where(dist < max_exact, dist, large).astype(np.int32)


def _bias_tiles(rel_bias, heads, dilation, rows, cols, offset):
    buckets = _t5_bucket_np(np.arange(BLOCK + 1) * dilation)
    per_head = jnp.stack([rel_bias[:, h] for h in heads]).astype(F32) * LOG2E
    dist = (lax.broadcasted_iota(jnp.int32, (rows, cols), 0) + offset
            - lax.broadcasted_iota(jnp.int32, (rows, cols), 1))
    tiles = jnp.full((len(heads), rows, cols), NEG_INF, F32)
    for bucket in np.unique(buckets):
        first = int(np.argmax(buckets == bucket))
        tiles = jnp.where((dist >= first)[None], per_head[:, int(bucket)][:, None, None], tiles)
    tiles = jnp.where(((dist >= 0) & (dist <= BLOCK))[None], tiles, NEG_INF)
    return tiles.reshape(len(heads) // 2, 2 * rows, cols)


def _head_stats(rel_bias, heads, sinks=None):
    bias_max = jnp.max(rel_bias.astype(F32), axis=0) * LOG2E
    rows = [jnp.zeros((len(heads),), F32) if sinks is None
            else jnp.stack([sinks[h] for h in heads]).astype(F32) * LOG2E,
            jnp.stack([bias_max[h] for h in heads])]
    rows += [jnp.zeros((len(heads),), F32)] * 6
    stats = jnp.stack(rows, axis=1)
    return jnp.broadcast_to(stats[:, :, None], (len(heads), 8, LANES)).reshape(len(heads) // 2, 2, 8, LANES)


def _rms(x, w):
    return x * lax.rsqrt(jnp.mean(x * x, axis=-1, keepdims=True) + EPS) * w


def _rep(stat, width):
    if width == LANES:
        return stat
    return jnp.concatenate([stat] * (width // LANES), axis=1)


def _silu(z):
    return z / (1.0 + jnp.exp(-z))


def _dot_nt(a, b):
    return lax.dot_general(a, b, (((1,), (1,)), ((), ())), preferred_element_type=F32)


def _low_half(rows):
    return lax.broadcasted_iota(jnp.int32, (rows, LANES), 1) < HEAD_DIM


EVEN_NORM_SLOTS = (tuple(range(N_PAIRS)) + (KA_SLOT,)
                   + tuple(N_A_SLOTS + QB_SLOT + p for p in range(N_PAIRS))
                   + tuple(N_A_SLOTS + KB_SLOT + p for p in range(N_PAIRS)))
QA_NORM, KA_NORM, QB_NORM, KB_NORM = 0, N_PAIRS, N_PAIRS + 1, 2 * N_PAIRS + 1


def _store_norm_stats(nsq_ref, block, val, heads_per_slot):
    vb = val.astype(BF16).astype(F32)
    sq = vb * vb
    if heads_per_slot == 1:
        parts = [sq]
    else:
        low = _low_half(val.shape[0])
        parts = [jnp.where(low, sq, 0.0), jnp.where(low, 0.0, sq)]
    for hh, part in enumerate(parts):
        worst = jnp.max(jnp.sum(part, axis=-1, keepdims=True), axis=0, keepdims=True)
        lanes = slice((block * heads_per_slot + hh) * LANES, (block * heads_per_slot + hh + 1) * LANES)
        nsq_ref[0, 0, :, lanes] = jnp.broadcast_to(worst, (8, LANES))


def _max_norms(nsq_ref):
    worst = nsq_ref[0, 0]
    for tile in range(1, nsq_ref.shape[1]):
        worst = jnp.maximum(worst, nsq_ref[0, tile])
    return jnp.sqrt(worst[:1, :])


def _inproj_even_kernel(x_ref, nw_ref, w_ref, a_ref, b_ref, g_ref, nsq_ref):
    hb = _rms(x_ref[0], nw_ref[...]).astype(BF16)
    q_scale = HEAD_DIM ** -0.5 * LOG2E
    n_chunks = w_ref.shape[1] // (2 * LANES)
    for c in range(n_chunks):
        r = jnp.dot(hb, w_ref[:, c * 2 * LANES:(c + 1) * 2 * LANES], preferred_element_type=F32)
        for t in range(2):
            slot = 2 * c + t
            val = r[:, t * LANES:(t + 1) * LANES]
            if slot < N_A_SLOTS:
                val = val * q_scale if slot < N_PAIRS else val
                a_ref[0, slot] = val.astype(BF16)
            elif slot < N_A_SLOTS + N_B_SLOTS:
                val = val * q_scale if slot - N_A_SLOTS < N_PAIRS else val
                b_ref[0, slot - N_A_SLOTS] = val
            else:
                g_ref[0, slot - N_A_SLOTS - N_B_SLOTS] = _silu(val).astype(BF16)
            if slot in EVEN_NORM_SLOTS:
                _store_norm_stats(nsq_ref, EVEN_NORM_SLOTS.index(slot), val, 2)


def _inproj_even(x, norm_w, w):
    n_z = AB_WIDTH // LANES
    slots = lambda n: pl.BlockSpec((1, n, ROW_TILE, LANES), lambda b, i: (b, 0, i, 0))
    out = lambda n, dtype: jax.ShapeDtypeStruct((BATCH, n, SEQ, LANES), dtype)
    return pl.pallas_call(
        _inproj_even_kernel,
        grid=(BATCH, SEQ // ROW_TILE),
        in_specs=[
            pl.BlockSpec((1, ROW_TILE, D_MODEL), lambda b, i: (b, i, 0)),
            pl.BlockSpec((1, D_MODEL), lambda b, i: (0, 0)),
            pl.BlockSpec(w.shape, lambda b, i: (0, 0)),
        ],
        out_specs=[slots(N_A_SLOTS), slots(N_B_SLOTS), slots(n_z),
                   pl.BlockSpec((1, 1, 8, 2 * len(EVEN_NORM_SLOTS) * LANES), lambda b, i: (b, i, 0, 0))],
        out_shape=[out(N_A_SLOTS, BF16), out(N_B_SLOTS, F32), out(n_z, BF16),
                   jax.ShapeDtypeStruct((BATCH, SEQ // ROW_TILE, 8, 2 * len(EVEN_NORM_SLOTS) * LANES), F32)],
        compiler_params=_params(2),
        name="inproj_even",
    )(x, norm_w.reshape(1, D_MODEL), w)


def _outproj_kernel(*refs, n_g, final):
    g_refs, (w_ref, x_ref), rest = refs[:n_g], refs[n_g:n_g + 2], refs[n_g + 2:]
    g = jnp.concatenate([g_ref[0, p] for g_ref in g_refs for p in range(g_ref.shape[1])], axis=1)
    y = x_ref[0] + jnp.dot(g, w_ref[...], preferred_element_type=F32)
    if final:
        fw_ref, o_ref = rest
        y = _rms(y, fw_ref[...])
    else:
        (o_ref,) = rest
    o_ref[0] = y


def _outproj(gs, w, x, final_w=None):
    final = final_w is not None
    in_specs = [pl.BlockSpec((1, g.shape[1], ROW_TILE, LANES), lambda b, i: (b, 0, i, 0)) for g in gs] + [
        pl.BlockSpec(w.shape, lambda b, i: (0, 0)),
        pl.BlockSpec((1, ROW_TILE, D_MODEL), lambda b, i: (b, i, 0)),
    ]
    args = [*gs, w, x]
    if final:
        in_specs.append(pl.BlockSpec((1, D_MODEL), lambda b, i: (0, 0)))
        args.append(final_w.reshape(1, D_MODEL))
    return pl.pallas_call(
        functools.partial(_outproj_kernel, n_g=len(gs), final=final),
        grid=(BATCH, SEQ // ROW_TILE),
        in_specs=in_specs,
        out_specs=pl.BlockSpec((1, ROW_TILE, D_MODEL), lambda b, i: (b, i, 0)),
        out_shape=jax.ShapeDtypeStruct((BATCH, SEQ, D_MODEL), F32),
        compiler_params=_params(2),
        name="outproj_final" if final else "outproj",
    )(*args)


def _stack_heads(q):
    lane = lax.broadcasted_iota(jnp.int32, (1, LANES), 1)
    keep0 = jnp.where(lane < HEAD_DIM, 1.0, 0.0).astype(q.dtype)
    return jnp.concatenate([q * keep0, q * (1.0 - keep0).astype(q.dtype)], axis=0)


def _pair_shift(qsq_ref, ksq_ref, stat_ref):
    bias_max = jnp.concatenate([stat_ref[0, hh, 1:2, :] for hh in range(2)], axis=1)
    return _max_norms(qsq_ref) * _max_norms(ksq_ref) + bias_max


def _stacked_rows(pair_row, rows):
    return jnp.concatenate([jnp.broadcast_to(pair_row[:, hh * LANES:(hh + 1) * LANES], (rows, LANES))
                            for hh in range(2)], axis=0)


def _fast_unit(q, k, v, shifted_bias, low):
    r = q.shape[0]
    p = jnp.exp2(_dot_nt(_stack_heads(q), k) + shifted_bias)
    sums = jnp.sum(p, axis=-1, keepdims=True)
    pv = jnp.dot(p.astype(BF16), v, preferred_element_type=F32)
    return jnp.where(low, pv[:r], pv[r:]), sums[:r], sums[r:]


def _sweep_blocks(n_blocks, block_fn, carry):
    carry = block_fn(0, 0, BLOCK, carry)

    def body(i, c):
        return block_fn(pl.multiple_of(i * BLOCK, BLOCK), pl.multiple_of((i - 1) * BLOCK, BLOCK), 2 * BLOCK, c)

    return lax.fori_loop(1, n_blocks, body, carry, unroll=BAND_UNROLL)


def _att_a_kernel(q_ref, k_ref, v_ref, g_ref, bias_ref, stat_ref, qsq_ref, ksq_ref, o_ref, shifted_bias_ref):
    low = _low_half(BLOCK)
    sink = jnp.concatenate([jnp.broadcast_to(stat_ref[0, hh, 0:1, :], (BLOCK, LANES)) for hh in range(2)], axis=0)
    shift = jnp.maximum(_stacked_rows(_pair_shift(qsq_ref, ksq_ref, stat_ref), BLOCK), sink)
    shifted_bias_ref[...] = bias_ref[0] - _rep(shift, 2 * BLOCK)
    sink_term = jnp.exp2(sink - shift)

    def operands(qs, ks, n_keys, bias_of):
        rows, keys = pl.ds(qs, BLOCK), pl.ds(ks, n_keys)
        return rows, q_ref[0, 0, rows, :], k_ref[0, 0, keys, :], v_ref[0, 0, keys, :], \
            bias_of[:, 2 * BLOCK - n_keys:], g_ref[0, 0, rows, :].astype(F32)

    def fast_block(qs, ks, n_keys, l_min):
        rows, q, k, v, shifted_bias, gate = operands(qs, ks, n_keys, shifted_bias_ref)
        pv, sum0, sum1 = _fast_unit(q, k, v, shifted_bias, low)
        l0, l1 = sum0 + sink_term[:BLOCK], sum1 + sink_term[BLOCK:]
        o_ref[0, 0, rows, :] = (pv / jnp.where(low, l0, l1) * gate).astype(BF16)
        return jnp.minimum(l_min, jnp.minimum(l0, l1))

    def exact_block(qs, ks, n_keys, carry):
        rows, q, k, v, bias, gate = operands(qs, ks, n_keys, bias_ref.at[0])
        s = _dot_nt(_stack_heads(q), k) + bias
        m = jnp.maximum(jnp.max(s, axis=-1, keepdims=True), sink)
        p = jnp.exp2(s - _rep(m, n_keys))
        denom = jnp.sum(p, axis=-1, keepdims=True) + jnp.exp2(sink - m)
        o = jnp.dot(p.astype(BF16), v, preferred_element_type=F32) / denom
        o_ref[0, 0, rows, :] = (jnp.where(low, o[:BLOCK], o[BLOCK:]) * gate).astype(BF16)
        return carry

    l_min = _sweep_blocks(SEQ // BLOCK, fast_block, jnp.full((BLOCK, LANES), jnp.inf, F32))

    @pl.when(jnp.logical_not(jnp.min(l_min) >= UNDERFLOW_GUARD))
    def _():
        _sweep_blocks(SEQ // BLOCK, exact_block, 0)


def _nsq_spec(block_of):
    return pl.BlockSpec((1, SEQ // ROW_TILE, 8, 2 * LANES), lambda b, p: (b, 0, 0, block_of(p)))


def _att_a(qkv, z, bias, stats, nsq):
    seq_spec = lambda slot_of: pl.BlockSpec((1, 1, SEQ, LANES), slot_of)
    return pl.pallas_call(
        _att_a_kernel,
        grid=(BATCH, N_PAIRS),
        in_specs=[
            seq_spec(lambda b, p: (b, p, 0, 0)),
            seq_spec(lambda b, p: (b, KA_SLOT, 0, 0)),
            seq_spec(lambda b, p: (b, VA_SLOT, 0, 0)),
            seq_spec(lambda b, p: (b, p, 0, 0)),
            pl.BlockSpec((1, 2 * BLOCK, 2 * BLOCK), lambda b, p: (p, 0, 0)),
            pl.BlockSpec((1, 2, 8, LANES), lambda b, p: (p, 0, 0, 0)),
            _nsq_spec(lambda p: QA_NORM + p),
            _nsq_spec(lambda p: KA_NORM),
        ],
        out_specs=seq_spec(lambda b, p: (b, p, 0, 0)),
        out_shape=jax.ShapeDtypeStruct((BATCH, N_PAIRS, SEQ, LANES), BF16),
        scratch_shapes=[pltpu.VMEM((2 * BLOCK, 2 * BLOCK), F32)],
        compiler_params=_params(2),
        name="att_swa",
    )(qkv, qkv, qkv, z, bias, stats, nsq, nsq)


def _att_b_kernel(q_ref, k_ref, v_ref, g_ref, b1_ref, b4_ref, b16_ref, stat_ref, qsq_ref, ksq_ref, o_ref,
                  m0_ref, m1_ref, l0_ref, l1_ref, acc_ref, sb1_ref, sb4_ref, sb16_ref):
    low = _low_half(BLOCK)
    load = lambda ref, rows: ref[0, 0, rows, :].astype(BF16)
    shift = _pair_shift(qsq_ref, ksq_ref, stat_ref)
    for bias_ref, shifted_ref in ((b1_ref, sb1_ref), (b4_ref, sb4_ref), (b16_ref, sb16_ref)):
        shifted_ref[...] = bias_ref[0] - _rep(_stacked_rows(shift, bias_ref.shape[1] // 2), bias_ref.shape[2])

    def sweep(unit, bias1, bias4, bias16):
        def block1(qs, ks, n_keys, carry):
            rows, keys = pl.ds(qs, BLOCK), pl.ds(ks, n_keys)
            unit(rows, load(q_ref, rows), load(k_ref, keys), load(v_ref, keys),
                 bias1[:, 2 * BLOCK - n_keys:], True)
            return carry

        _sweep_blocks(SEQ // BLOCK, block1, 0)

        def residue4(f, carry):
            def block4(us, ks, n_keys, c):
                rows, keys = pl.ds(us * 4 + f, BLOCK, stride=4), pl.ds(ks * 4 + f, n_keys, stride=4)
                unit(rows, load(q_ref, rows), load(k_ref, keys), load(v_ref, keys),
                     bias4[:, 2 * BLOCK - n_keys:], False)
                return c

            return _sweep_blocks(SEQ // 4 // BLOCK, block4, carry)

        lax.fori_loop(0, 4, residue4, 0)

        def residue16(r, carry):
            rows = pl.ds(r, SEQ // 16, stride=16)
            unit(rows, load(q_ref, rows), load(k_ref, rows), load(v_ref, rows), bias16[...], False)
            return carry

        lax.fori_loop(0, 16, residue16, 0, unroll=2)

    def fast_unit(rows, q, k, v, shifted_bias, first):
        r = q.shape[0]
        pv, sum0, sum1 = _fast_unit(q, k, v, shifted_bias, low if r == BLOCK else _low_half(r))
        if first:
            acc_ref[rows, :] = pv
            l0_ref[rows, :] = jnp.broadcast_to(sum0, (r, LANES))
            l1_ref[rows, :] = jnp.broadcast_to(sum1, (r, LANES))
        else:
            acc_ref[rows, :] += pv
            l0_ref[rows, :] += sum0
            l1_ref[rows, :] += sum1

    def exact_unit(rows, q, k, v, bias, first):
        r, n_keys = q.shape[0], k.shape[0]
        low_r = low if r == BLOCK else _low_half(r)
        s = _dot_nt(_stack_heads(q), k) + bias
        m_new = jnp.broadcast_to(jnp.max(s, axis=-1, keepdims=True), (2 * r, LANES))
        if not first:
            m_old = jnp.concatenate([m0_ref[rows, :], m1_ref[rows, :]], axis=0)
            m_new = jnp.maximum(m_old, m_new)
            alpha = jnp.exp2(m_old - m_new)
        p = jnp.exp2(s - _rep(m_new, n_keys))
        sums = jnp.sum(p, axis=-1, keepdims=True)
        pv = jnp.dot(p.astype(BF16), v, preferred_element_type=F32)
        pv = jnp.where(low_r, pv[:r], pv[r:])
        m0_ref[rows, :] = m_new[:r]
        m1_ref[rows, :] = m_new[r:]
        if first:
            acc_ref[rows, :] = pv
            l0_ref[rows, :] = jnp.broadcast_to(sums[:r], (r, LANES))
            l1_ref[rows, :] = jnp.broadcast_to(sums[r:], (r, LANES))
        else:
            acc_ref[rows, :] = jnp.where(low_r, alpha[:r], alpha[r:]) * acc_ref[rows, :] + pv
            l0_ref[rows, :] = alpha[:r] * l0_ref[rows, :] + sums[:r]
            l1_ref[rows, :] = alpha[r:] * l1_ref[rows, :] + sums[r:]

    def finish(i, l_min):
        rows = pl.ds(pl.multiple_of(i * BLOCK, BLOCK), BLOCK)
        l0, l1 = l0_ref[rows, :], l1_ref[rows, :]
        gate = g_ref[0, 0, rows, :].astype(F32)
        o_ref[0, 0, rows, :] = (acc_ref[rows, :] / jnp.where(low, l0, l1) * gate).astype(BF16)
        return jnp.minimum(l_min, jnp.minimum(l0, l1))

    no_min = jnp.full((BLOCK, LANES), jnp.inf, F32)
    sweep(fast_unit, sb1_ref, sb4_ref, sb16_ref)
    l_min = lax.fori_loop(0, SEQ // BLOCK, finish, no_min, unroll=BAND_UNROLL)

    @pl.when(jnp.logical_not(jnp.min(l_min) >= UNDERFLOW_GUARD))
    def _():
        sweep(exact_unit, b1_ref.at[0], b4_ref.at[0], b16_ref.at[0])
        lax.fori_loop(0, SEQ // BLOCK, finish, no_min)


def _att_b(qkv, z, b1, b4, b16, stats, nsq):
    seq_spec = lambda slot: pl.BlockSpec((1, 1, SEQ, LANES), lambda b, p: (b, slot + p, 0, 0))
    bias_spec = lambda a: pl.BlockSpec((1,) + a.shape[1:], lambda b, p: (p, 0, 0))
    return pl.pallas_call(
        _att_b_kernel,
        grid=(BATCH, N_PAIRS),
        in_specs=[seq_spec(QB_SLOT), seq_spec(KB_SLOT), seq_spec(VB_SLOT), seq_spec(N_PAIRS),
                  bias_spec(b1), bias_spec(b4), bias_spec(b16),
                  pl.BlockSpec((1, 2, 8, LANES), lambda b, p: (p, 0, 0, 0)),
                  _nsq_spec(lambda p: QB_NORM + p), _nsq_spec(lambda p: KB_NORM + p)],
        out_specs=seq_spec(0),
        out_shape=jax.ShapeDtypeStruct((BATCH, N_PAIRS, SEQ, LANES), BF16),
        scratch_shapes=[pltpu.VMEM((SEQ, LANES), F32)] * 5 + [pltpu.VMEM(b.shape[1:], F32) for b in (b1, b4, b16)],
        compiler_params=_params(2),
        name="att_dilated",
    )(qkv, qkv, qkv, z, b1, b4, b16, stats, nsq, nsq)


def _rope_table_kernel(pos_ref, freq_ref, cos_ref, sin_ref):
    ang = pos_ref[0].astype(F32) * freq_ref[...]
    cos_ref[0] = jnp.cos(ang)
    sin_ref[0] = jnp.sin(ang)


def _rope_tables(positions):
    inv_freq = ROPE_THETA ** (-jnp.arange(0, C_ROPE, 2, dtype=F32) / C_ROPE)
    lane = np.arange(LANES)
    freq = jnp.where(jnp.asarray(lane >= C_NOPE), inv_freq[lane % (C_ROPE // 2)], 0.0).reshape(1, LANES)
    table = jax.ShapeDtypeStruct((BATCH, SEQ, LANES), F32)
    spec = pl.BlockSpec((1, ROW_TILE, LANES), lambda b, i: (b, i, 0))
    return pl.pallas_call(
        _rope_table_kernel,
        grid=(BATCH, SEQ // ROW_TILE),
        in_specs=[pl.BlockSpec((1, ROW_TILE, 1), lambda b, i: (b, i, 0)),
                  pl.BlockSpec((1, LANES), lambda b, i: (0, 0))],
        out_specs=[spec, spec],
        out_shape=[table, table],
        compiler_params=_params(2),
        name="rope_tables",
    )(positions.reshape(BATCH, SEQ, 1), freq)


C_LOWRANK = C_Q_RANK + C_KV_RANK + 2 * LANES


def _inproj_odd_kernel(x_ref, nw_ref, w_ref, qnw_ref, wq_ref, kvnw_ref, wk_ref, wv_ref, cos_ref, sin_ref,
                       q_ref, k_ref, v_ref, g_ref, nsq_ref):
    hb = _rms(x_ref[0], nw_ref[...]).astype(BF16)
    c = jnp.dot(hb, w_ref[:, :C_LOWRANK], preferred_element_type=F32)
    for j in range(C_WIDTH // (2 * LANES)):
        lo = C_LOWRANK + j * 2 * LANES
        r = jnp.dot(hb, w_ref[:, lo:lo + 2 * LANES], preferred_element_type=F32)
        for t in range(2):
            g_ref[0, 2 * j + t] = _silu(r[:, t * LANES:(t + 1) * LANES]).astype(BF16)

    qn = _rms(c[:, :C_Q_RANK], qnw_ref[...]).astype(BF16)
    kvn = _rms(c[:, C_Q_RANK:C_Q_RANK + C_KV_RANK], kvnw_ref[...]).astype(BF16)
    k_pe = c[:, C_Q_RANK + C_KV_RANK:C_Q_RANK + C_KV_RANK + LANES]
    k_pe_rot = c[:, C_Q_RANK + C_KV_RANK + LANES:]
    cos, sin = cos_ref[0], sin_ref[0]
    lane = lax.broadcasted_iota(jnp.int32, cos.shape, 1)
    q_table = jnp.where(lane < C_NOPE + C_ROPE, cos, sin) * ((C_NOPE + C_ROPE) ** -0.5 * LOG2E)
    k_rope = k_pe * cos + k_pe_rot * sin

    for j in range(C_HEADS // 2):
        cols = slice(j * 2 * LANES, (j + 1) * 2 * LANES)
        rq = jnp.dot(qn, wq_ref[:, cols], preferred_element_type=F32)
        rk = jnp.dot(kvn, wk_ref[:, cols], preferred_element_type=F32)
        for t in range(2):
            lanes = slice(t * LANES, (t + 1) * LANES)
            q_head, k_head = rq[:, lanes] * q_table, rk[:, lanes] + k_rope
            q_ref[0, 2 * j + t] = q_head.astype(BF16)
            k_ref[0, 2 * j + t] = k_head.astype(BF16)
            _store_norm_stats(nsq_ref, 2 * j + t, q_head, 1)
            _store_norm_stats(nsq_ref, C_HEADS + 2 * j + t, k_head, 1)
    for j in range(C_WIDTH // (2 * LANES)):
        rv = jnp.dot(kvn, wv_ref[:, j * 2 * LANES:(j + 1) * 2 * LANES], preferred_element_type=F32)
        for t in range(2):
            v_ref[0, 2 * j + t] = rv[:, t * LANES:(t + 1) * LANES].astype(BF16)


def _inproj_odd(x, norm_w, w, q_norm, wq, kv_norm, wk, wv, cos, sin):
    full = lambda a: pl.BlockSpec(a.shape, lambda b, i: (0,) * a.ndim)
    row = lambda width: pl.BlockSpec((1, ROW_TILE, width), lambda b, i: (b, i, 0))
    slots = lambda n: pl.BlockSpec((1, n, ROW_TILE, LANES), lambda b, i: (b, 0, i, 0))
    out = lambda n: jax.ShapeDtypeStruct((BATCH, n, SEQ, LANES), BF16)
    n_pairs = C_WIDTH // LANES
    vecs = [norm_w.reshape(1, D_MODEL), q_norm.reshape(1, C_Q_RANK), kv_norm.reshape(1, C_KV_RANK)]
    return pl.pallas_call(
        _inproj_odd_kernel,
        grid=(BATCH, SEQ // ROW_TILE),
        in_specs=[row(D_MODEL), full(vecs[0]), full(w), full(vecs[1]), full(wq), full(vecs[2]), full(wk), full(wv),
                  row(LANES), row(LANES)],
        out_specs=[slots(C_HEADS), slots(C_HEADS), slots(n_pairs), slots(n_pairs),
                   pl.BlockSpec((1, 1, 8, 2 * C_HEADS * LANES), lambda b, i: (b, i, 0, 0))],
        out_shape=[out(C_HEADS), out(C_HEADS), out(n_pairs), out(n_pairs),
                   jax.ShapeDtypeStruct((BATCH, SEQ // ROW_TILE, 8, 2 * C_HEADS * LANES), F32)],
        compiler_params=_params(2),
        name="inproj_mla",
    )(x, vecs[0], w, vecs[1], wq, vecs[2], wk, wv, cos, sin)


V_ROWS = 80


def _flash_kernel(q_ref, k_ref, v_ref, g_ref, qsq_ref, ksq_ref, o_ref, vt_ref, ot_ref,
                  m0_ref, m1_ref, l0_ref, l1_ref, acc_ref):
    t, half = FLASH_TILE, FLASH_TILE // 2
    qi = pl.program_id(2)
    low = _low_half(t)
    m_refs, l_refs = (m0_ref, m1_ref), (l0_ref, l1_ref)

    @pl.when(qi == 0)
    def _():
        row = lax.broadcasted_iota(jnp.int32, (V_ROWS - HEAD_DIM, half), 0)
        ones_row = jnp.where(row == 0, 1.0, 0.0).astype(BF16)

        def prep_body(c, carry):
            rows = pl.ds(pl.multiple_of(c * half, half), half)
            v_t = v_ref[0, 0, rows, :].astype(F32).T
            for hh in range(2):
                vt_ref[hh, :HEAD_DIM, rows] = v_t[hh * HEAD_DIM:(hh + 1) * HEAD_DIM].astype(BF16)
                vt_ref[hh, HEAD_DIM:, rows] = ones_row
            return carry

        lax.fori_loop(0, SEQ // half, prep_body, 0)

    bound = _max_norms(qsq_ref) * _max_norms(ksq_ref)
    shifts = [bound[:, hh * LANES:hh * LANES + 1] for hh in range(2)]
    ot_ref[...] = jnp.zeros(ot_ref.shape, F32)

    def fast_step(ks, width, q_lo=0, q_n=FLASH_TILE, diagonal=False):
        for hh in range(2):
            s_t = _dot_nt(k_ref[0, hh, pl.ds(ks, width), :], q_ref[0, hh, q_lo:q_lo + q_n, :])
            if diagonal:
                causal = (lax.broadcasted_iota(jnp.int32, (width, q_n), 0)
                          <= lax.broadcasted_iota(jnp.int32, (width, q_n), 1))
                s_t = jnp.where(causal, s_t, NEG_INF)
            p_t = jnp.exp2(s_t - shifts[hh]).astype(BF16)
            ot_ref[hh, :, q_lo:q_lo + q_n] += jnp.dot(vt_ref[hh, :, pl.ds(ks, width)], p_t,
                                                      preferred_element_type=F32)

    def fast_body(j, carry):
        fast_step(pl.multiple_of(j * t, t), t)
        return carry

    lax.fori_loop(0, qi, fast_body, 0)
    q_start = pl.multiple_of(qi * t, t)
    fast_step(q_start, half, diagonal=True)
    fast_step(q_start + half, half, q_lo=half, q_n=half, diagonal=True)

    sums = [ot_ref[hh, HEAD_DIM:HEAD_DIM + 1, :] for hh in range(2)]
    healthy = jnp.minimum(jnp.min(sums[0]), jnp.min(sums[1])) >= UNDERFLOW_GUARD
    gate = g_ref[0, 0].astype(F32)

    @pl.when(healthy)
    def _():
        o_t = jnp.concatenate([ot_ref[hh, :HEAD_DIM, :] / sums[hh] for hh in range(2)], axis=0)
        o_ref[0, 0] = (o_t.T * gate).astype(BF16)

    def exact_step(j, carry):
        ks = pl.multiple_of(j * half, half)
        v = v_ref[0, 0, pl.ds(ks, half), :]
        visible = (lax.broadcasted_iota(jnp.int32, (t, half), 1) + (ks - q_start)
                   <= lax.broadcasted_iota(jnp.int32, (t, half), 0))
        pvs, alphas = [], []
        for hh in range(2):
            s = jnp.where(visible, _dot_nt(q_ref[0, hh], k_ref[0, hh, pl.ds(ks, half), :]), NEG_INF)
            m_old = m_refs[hh][...]
            m_new = jnp.maximum(m_old, jnp.max(s, axis=-1, keepdims=True))
            alpha = jnp.exp2(m_old - m_new)
            p = jnp.exp2(s - _rep(m_new, half))
            m_refs[hh][...] = m_new
            l_refs[hh][...] = alpha * l_refs[hh][...] + jnp.sum(p, axis=-1, keepdims=True)
            pvs.append(jnp.dot(p.astype(BF16), v, preferred_element_type=F32))
            alphas.append(alpha)
        acc_ref[...] = jnp.where(low, alphas[0], alphas[1]) * acc_ref[...] + jnp.where(low, pvs[0], pvs[1])
        return carry

    @pl.when(jnp.logical_not(healthy))
    def _():
        for hh in range(2):
            m_refs[hh][...] = jnp.full((t, LANES), NEG_INF, F32)
            l_refs[hh][...] = jnp.zeros((t, LANES), F32)
        acc_ref[...] = jnp.zeros((t, LANES), F32)
        lax.fori_loop(0, 2 * qi + 2, exact_step, 0)
        o = acc_ref[...] / jnp.where(low, l0_ref[...], l1_ref[...])
        o_ref[0, 0] = (o * gate).astype(BF16)


def _flash(q, k, v, z, nsq):
    t = FLASH_TILE
    n_pairs = C_WIDTH // LANES
    tile = pl.BlockSpec((1, 1, t, LANES), lambda b, p, i: (b, p, i, 0))
    return pl.pallas_call(
        _flash_kernel,
        grid=(BATCH, n_pairs, SEQ // t),
        in_specs=[
            pl.BlockSpec((1, 2, t, LANES), lambda b, p, i: (b, p, i, 0)),
            pl.BlockSpec((1, 2, SEQ, LANES), lambda b, p, i: (b, p, 0, 0)),
            pl.BlockSpec((1, 1, SEQ, LANES), lambda b, p, i: (b, p, 0, 0)),
            tile,
            pl.BlockSpec((1, t // ROW_TILE, 8, 2 * LANES), lambda b, p, i: (b, i, 0, p)),
            pl.BlockSpec((1, SEQ // ROW_TILE, 8, 2 * LANES), lambda b, p, i: (b, 0, 0, C_HEADS // 2 + p)),
        ],
        out_specs=tile,
        out_shape=jax.ShapeDtypeStruct((BATCH, n_pairs, SEQ, LANES), BF16),
        scratch_shapes=[pltpu.VMEM((2, V_ROWS, SEQ), BF16), pltpu.VMEM((2, V_ROWS, t), F32)]
        + [pltpu.VMEM((t, LANES), F32)] * 5,
        compiler_params=_params(3),
        name="mla_flash",
    )(q, k, v, z, nsq, nsq)


def _even_weights(w_in, w_out):
    perm = np.concatenate([np.arange(h * HEAD_DIM, (h + 1) * HEAD_DIM) for h in A_PAIR_HEADS])
    splits = np.cumsum([A_WIDTH, A_KV_WIDTH, A_KV_WIDTH, B_WIDTH, B_WIDTH, B_WIDTH])
    qa, ka, va, qb, kb, vb, z = jnp.split(w_in, splits, axis=1)
    w = jnp.concatenate([qa[:, perm], ka, va, qb, kb, vb, z[:, :A_WIDTH][:, perm], z[:, A_WIDTH:]],
                        axis=1).astype(BF16)
    w_o = jnp.concatenate([w_out[:A_WIDTH][perm], w_out[A_WIDTH:]], axis=0).astype(BF16)
    return w, w_o


def _rot_half_cols(w):
    half = w.shape[1] // 2
    return jnp.concatenate([-w[:, half:], w[:, :half]], axis=1)


def _odd_weights(w_in, w_qb, w_kvb):
    splits = np.cumsum([C_Q_RANK, C_KV_RANK, C_ROPE])
    w_cq, w_ckv, w_kpe, w_z = jnp.split(w_in, splits, axis=1)
    pad = jnp.zeros((D_MODEL, C_NOPE), w_in.dtype)
    w_kpe_rot = _rot_half_cols(w_kpe)
    w = jnp.concatenate([w_cq, w_ckv, pad, w_kpe, w_kpe, pad, w_kpe_rot, w_kpe_rot, w_z], axis=1).astype(BF16)

    wq = w_qb.reshape(C_Q_RANK, C_HEADS, C_NOPE + C_ROPE)
    wq_pe = wq[:, :, C_NOPE:]
    wq_rot = jnp.concatenate([-wq_pe[:, :, C_ROPE // 2:], wq_pe[:, :, :C_ROPE // 2]], axis=2)
    wq = jnp.concatenate([wq, wq_rot], axis=2).reshape(C_Q_RANK, C_HEADS * LANES).astype(BF16)

    wkv = w_kvb.reshape(C_KV_RANK, C_HEADS, C_NOPE + C_V)
    wk = jnp.concatenate([wkv[:, :, :C_NOPE], jnp.zeros((C_KV_RANK, C_HEADS, LANES - C_NOPE), w_kvb.dtype)], axis=2)
    wk = wk.reshape(C_KV_RANK, C_HEADS * LANES).astype(BF16)
    wv = wkv[:, :, C_NOPE:].reshape(C_KV_RANK, C_WIDTH).astype(BF16)
    return w, wq, wk, wv


def kernel(x, positions, norm_w, rel_bias, ab_w_in, ab_sinks, ab_w_out, c_w_in, c_q_norm, c_w_qb, c_kv_norm,
           c_w_kvb, c_w_out, final_norm):
    bias_a = _bias_tiles(rel_bias, A_PAIR_HEADS, 1, BLOCK, 2 * BLOCK, BLOCK)
    bias_b1 = _bias_tiles(rel_bias, B_PAIR_HEADS, 1, BLOCK, 2 * BLOCK, BLOCK)
    bias_b4 = _bias_tiles(rel_bias, B_PAIR_HEADS, 4, BLOCK, 2 * BLOCK, BLOCK)
    bias_b16 = _bias_tiles(rel_bias, B_PAIR_HEADS, 16, SEQ // 16, SEQ // 16, 0)
    stats_b = _head_stats(rel_bias, B_PAIR_HEADS)
    cos, sin = _rope_tables(positions)

    for layer in range(DEPTH):
        i = layer // 2
        last = final_norm if layer == DEPTH - 1 else None
        if layer % 2 == 0:
            w, w_o = _even_weights(ab_w_in[i], ab_w_out[i])
            qkv_a, qkv_b, z, ksq = _inproj_even(x, norm_w[layer], w)
            gs = [_att_a(qkv_a, z, bias_a, _head_stats(rel_bias, A_PAIR_HEADS, ab_sinks[i]), ksq),
                  _att_b(qkv_b, z, bias_b1, bias_b4, bias_b16, stats_b, ksq)]
        else:
            w, wq, wk, wv = _odd_weights(c_w_in[i], c_w_qb[i], c_w_kvb[i])
            q, k, v, z, nsq = _inproj_odd(x, norm_w[layer], w, c_q_norm[i], wq, c_kv_norm[i], wk, wv, cos, sin)
            gs = [_flash(q, k, v, z, nsq)]
            w_o = c_w_out[i].astype(BF16)
        x = _outproj(gs, w_o, x, last)
    return x
```

```python
import functools
import math

import numpy as np
import jax
import jax.numpy as jnp
from jax import lax
from jax.experimental import pallas as pl
from jax.experimental.pallas import tpu as pltpu

D_MODEL = 1024
BATCH = 4
SEQ = 4096
DEPTH = 4
HEAD_DIM = 64
BLOCK = 128
A_HEADS = 8
A_KV_HEADS = 2
A_WINDOW = 128
B_HEADS = 8
B_CONFIGS = ((128, 1), (512, 4), (2048, 16))
NUM_BUCKETS = 32
MAX_DISTANCE = 2048
C_HEADS = 16
C_Q_RANK = 256
C_KV_RANK = 128
C_NOPE = 64
C_ROPE = 32
C_V = 64
ROPE_THETA = 10000.0
EPS = 1e-6

A_WIDTH = A_HEADS * HEAD_DIM
A_KV_WIDTH = A_KV_HEADS * HEAD_DIM
B_WIDTH = B_HEADS * HEAD_DIM
AB_WIDTH = A_WIDTH + B_WIDTH
C_WIDTH = C_HEADS * C_V

LANES = 128
F32 = jnp.float32
BF16 = jnp.bfloat16
NEG_INF = float("-inf")
LOG2E = math.log2(math.e)
UNDERFLOW_GUARD = 2.0 ** -100
NORM_FLOOR = 1e-30

ROW_TILE = 512
FLASH_TILE = 1024
VMEM_LIMIT = 48 * 1024 * 1024

N_PAIRS = 4
KA_SLOT, VA_SLOT, N_A_SLOTS = 4, 5, 6
QB_SLOT, KB_SLOT, VB_SLOT, N_B_SLOTS = 0, 4, 8, 12
BAND_UNROLL = 8
A_PAIR_HEADS = [h for p in range(N_PAIRS) for h in (p, p + N_PAIRS)]
B_PAIR_HEADS = list(range(A_HEADS, A_HEADS + B_HEADS))


def _params(n_axes):
    return pltpu.CompilerParams(dimension_semantics=("arbitrary",) * n_axes,
                                vmem_limit_bytes=VMEM_LIMIT)


def _t5_bucket_np(dist):
    max_exact = NUM_BUCKETS // 2
    d = np.maximum(dist, 1).astype(np.float64)
    large = max_exact + (np.log(d / max_exact) / math.log(MAX_DISTANCE / max_exact)
                         * (NUM_BUCKETS - max_exact)).astype(np.int64)
    large = np.minimum(large, NUM_BUCKETS - 1)
    return np.where(dist < max_exact, dist, large).astype(np.int32)


def _bias_tiles(rel_bias, heads, dilation, rows, cols, offset):
    buckets = _t5_bucket_np(np.arange(BLOCK + 1) * dilation)
    per_head = jnp.take(rel_bias.astype(F32).T, np.asarray(heads), axis=0) * LOG2E
    dist = (lax.broadcasted_iota(jnp.int32, (rows, cols), 0) + offset
            - lax.broadcasted_iota(jnp.int32, (rows, cols), 1))
    tiles = jnp.full((len(heads), rows, cols), NEG_INF, F32)
    for bucket in np.unique(buckets):
        first = int(np.argmax(buckets == bucket))
        tiles = jnp.where((dist >= first)[None], per_head[:, int(bucket)][:, None, None], tiles)
    tiles = jnp.where(((dist >= 0) & (dist <= BLOCK))[None], tiles, NEG_INF)
    return tiles.reshape(len(heads) // 2, 2 * rows, cols)


def _head_stats(rel_bias, heads, sinks=None):
    order = np.asarray(heads)
    bias_max = jnp.take(jnp.max(rel_bias.astype(F32), axis=0), order) * LOG2E
    sink = jnp.zeros_like(bias_max) if sinks is None else jnp.take(sinks.astype(F32), order) * LOG2E
    stats = jnp.zeros((len(heads), 8), F32).at[:, 0].set(sink).at[:, 1].set(bias_max)
    return jnp.broadcast_to(stats[:, :, None], (len(heads), 8, LANES)).reshape(len(heads) // 2, 2, 8, LANES)


def _rms(x, w):
    return x * lax.rsqrt(jnp.mean(x * x, axis=-1, keepdims=True) + EPS) * w


def _rep(stat, width):
    if width == LANES:
        return stat
    return jnp.concatenate([stat] * (width // LANES), axis=1)


def _silu(z):
    return z / (1.0 + jnp.exp(-z))


def _dot_nt(a, b):
    return lax.dot_general(a, b, (((1,), (1,)), ((), ())), preferred_element_type=F32)


def _low_half(rows):
    return lax.broadcasted_iota(jnp.int32, (rows, LANES), 1) < HEAD_DIM


EVEN_NORM_SLOTS = (tuple(range(N_PAIRS)) + (KA_SLOT,)
                   + tuple(N_A_SLOTS + QB_SLOT + p for p in range(N_PAIRS))
                   + tuple(N_A_SLOTS + KB_SLOT + p for p in range(N_PAIRS)))
QA_NORM, KA_NORM, QB_NORM, KB_NORM = 0, N_PAIRS, N_PAIRS + 1, 2 * N_PAIRS + 1


def _store_norm_stats(nsq_ref, block, val, heads_per_slot):
    vb = val.astype(BF16).astype(F32)
    sq = vb * vb
    if heads_per_slot == 1:
        parts = [sq]
    else:
        low = _low_half(val.shape[0])
        parts = [jnp.where(low, sq, 0.0), jnp.where(low, 0.0, sq)]
    for hh, part in enumerate(parts):
        worst = jnp.max(jnp.sum(part, axis=-1, keepdims=True), axis=0, keepdims=True)
        lanes = slice((block * heads_per_slot + hh) * LANES, (block * heads_per_slot + hh + 1) * LANES)
        nsq_ref[0, 0, :, lanes] = jnp.broadcast_to(worst, (8, LANES))


def _max_norms(nsq_ref):
    worst = nsq_ref[0, 0]
    for tile in range(1, nsq_ref.shape[1]):
        worst = jnp.maximum(worst, nsq_ref[0, tile])
    return jnp.sqrt(worst[:1, :])


def _inproj_even_kernel(x_ref, nw_ref, w_ref, a_ref, b_ref, g_ref, nsq_ref):
    hb = _rms(x_ref[0], nw_ref[...]).astype(BF16)
    q_scale = HEAD_DIM ** -0.5 * LOG2E
    n_chunks = w_ref.shape[1] // (2 * LANES)
    for c in range(n_chunks):
        r = jnp.dot(hb, w_ref[:, c * 2 * LANES:(c + 1) * 2 * LANES], preferred_element_type=F32)
        for t in range(2):
            slot = 2 * c + t
            val = r[:, t * LANES:(t + 1) * LANES]
            if slot < N_A_SLOTS:
                val = val * q_scale if slot < N_PAIRS else val
                a_ref[0, slot] = val.astype(BF16)
            elif slot < N_A_SLOTS + N_B_SLOTS:
                val = val * q_scale if slot - N_A_SLOTS < N_PAIRS else val
                b_ref[0, slot - N_A_SLOTS] = val
            else:
                g_ref[0, slot - N_A_SLOTS - N_B_SLOTS] = _silu(val).astype(BF16)
            if slot in EVEN_NORM_SLOTS:
                _store_norm_stats(nsq_ref, EVEN_NORM_SLOTS.index(slot), val, 2)


def _inproj_even(x, norm_w, w):
    n_z = AB_WIDTH // LANES
    slots = lambda n: pl.BlockSpec((1, n, ROW_TILE, LANES), lambda b, i: (b, 0, i, 0))
    out = lambda n, dtype: jax.ShapeDtypeStruct((BATCH, n, SEQ, LANES), dtype)
    return pl.pallas_call(
        _inproj_even_kernel,
        grid=(BATCH, SEQ // ROW_TILE),
        in_specs=[
            pl.BlockSpec((1, ROW_TILE, D_MODEL), lambda b, i: (b, i, 0)),
            pl.BlockSpec((1, D_MODEL), lambda b, i: (0, 0)),
            pl.BlockSpec(w.shape, lambda b, i: (0, 0)),
        ],
        out_specs=[slots(N_A_SLOTS), slots(N_B_SLOTS), slots(n_z),
                   pl.BlockSpec((1, 1, 8, 2 * len(EVEN_NORM_SLOTS) * LANES), lambda b, i: (b, i, 0, 0))],
        out_shape=[out(N_A_SLOTS, BF16), out(N_B_SLOTS, F32), out(n_z, BF16),
                   jax.ShapeDtypeStruct((BATCH, SEQ // ROW_TILE, 8, 2 * len(EVEN_NORM_SLOTS) * LANES), F32)],
        compiler_params=_params(2),
        name="inproj_even",
    )(x, norm_w.reshape(1, D_MODEL), w)


def _outproj_kernel(*refs, n_g, final):
    g_refs, (w_ref, x_ref), rest = refs[:n_g], refs[n_g:n_g + 2], refs[n_g + 2:]
    g = jnp.concatenate([g_ref[0, p] for g_ref in g_refs for p in range(g_ref.shape[1])], axis=1)
    y = x_ref[0] + jnp.dot(g, w_ref[...], preferred_element_type=F32)
    if final:
        fw_ref, o_ref = rest
        y = _rms(y, fw_ref[...])
    else:
        (o_ref,) = rest
    o_ref[0] = y


def _outproj(gs, w, x, final_w=None):
    final = final_w is not None
    in_specs = [pl.BlockSpec((1, g.shape[1], ROW_TILE, LANES), lambda b, i: (b, 0, i, 0)) for g in gs] + [
        pl.BlockSpec(w.shape, lambda b, i: (0, 0)),
        pl.BlockSpec((1, ROW_TILE, D_MODEL), lambda b, i: (b, i, 0)),
    ]
    args = [*gs, w, x]
    if final:
        in_specs.append(pl.BlockSpec((1, D_MODEL), lambda b, i: (0, 0)))
        args.append(final_w.reshape(1, D_MODEL))
    return pl.pallas_call(
        functools.partial(_outproj_kernel, n_g=len(gs), final=final),
        grid=(BATCH, SEQ // ROW_TILE),
        in_specs=in_specs,
        out_specs=pl.BlockSpec((1, ROW_TILE, D_MODEL), lambda b, i: (b, i, 0)),
        out_shape=jax.ShapeDtypeStruct((BATCH, SEQ, D_MODEL), F32),
        compiler_params=_params(2),
        name="outproj_final" if final else "outproj",
    )(*args)


def _stack_heads(q):
    lane = lax.broadcasted_iota(jnp.int32, (1, LANES), 1)
    keep0 = jnp.where(lane < HEAD_DIM, 1.0, 0.0).astype(q.dtype)
    return jnp.concatenate([q * keep0, q * (1.0 - keep0).astype(q.dtype)], axis=0)


def _pair_shift(qsq_ref, ksq_ref, stat_ref):
    bias_max = jnp.concatenate([stat_ref[0, hh, 1:2, :] for hh in range(2)], axis=1)
    return _max_norms(qsq_ref) * _max_norms(ksq_ref) + bias_max


def _stacked_rows(pair_row, rows):
    return jnp.concatenate([jnp.broadcast_to(pair_row[:, hh * LANES:(hh + 1) * LANES], (rows, LANES))
                            for hh in range(2)], axis=0)


def _fast_unit(q, k, v, shifted_bias, low):
    r = q.shape[0]
    p = jnp.exp2(_dot_nt(_stack_heads(q), k) + shifted_bias)
    sums = jnp.sum(p, axis=-1, keepdims=True)
    pv = jnp.dot(p.astype(BF16), v, preferred_element_type=F32)
    return jnp.where(low, pv[:r], pv[r:]), sums[:r], sums[r:]


def _sweep_blocks(n_blocks, block_fn, carry):
    carry = block_fn(0, 0, BLOCK, carry)

    def body(i, c):
        return block_fn(pl.multiple_of(i * BLOCK, BLOCK), pl.multiple_of((i - 1) * BLOCK, BLOCK), 2 * BLOCK, c)

    return lax.fori_loop(1, n_blocks, body, carry, unroll=BAND_UNROLL)


def _att_a_kernel(q_ref, k_ref, v_ref, g_ref, bias_ref, stat_ref, qsq_ref, ksq_ref, o_ref, shifted_bias_ref):
    low = _low_half(BLOCK)
    sink = jnp.concatenate([jnp.broadcast_to(stat_ref[0, hh, 0:1, :], (BLOCK, LANES)) for hh in range(2)], axis=0)
    shift = jnp.maximum(_stacked_rows(_pair_shift(qsq_ref, ksq_ref, stat_ref), BLOCK), sink)
    shifted_bias_ref[...] = bias_ref[0] - _rep(shift, 2 * BLOCK)
    sink_term = jnp.exp2(sink - shift)

    def operands(qs, ks, n_keys, bias_of):
        rows, keys = pl.ds(qs, BLOCK), pl.ds(ks, n_keys)
        return rows, q_ref[0, 0, rows, :], k_ref[0, 0, keys, :], v_ref[0, 0, keys, :], \
            bias_of[:, 2 * BLOCK - n_keys:], g_ref[0, 0, rows, :].astype(F32)

    def fast_block(qs, ks, n_keys, l_min):
        rows, q, k, v, shifted_bias, gate = operands(qs, ks, n_keys, shifted_bias_ref)
        pv, sum0, sum1 = _fast_unit(q, k, v, shifted_bias, low)
        l0, l1 = sum0 + sink_term[:BLOCK], sum1 + sink_term[BLOCK:]
        o_ref[0, 0, rows, :] = (pv / jnp.where(low, l0, l1) * gate).astype(BF16)
        return jnp.minimum(l_min, jnp.minimum(l0, l1))

    def exact_block(qs, ks, n_keys, carry):
        rows, q, k, v, bias, gate = operands(qs, ks, n_keys, bias_ref.at[0])
        s = _dot_nt(_stack_heads(q), k) + bias
        m = jnp.maximum(jnp.max(s, axis=-1, keepdims=True), sink)
        p = jnp.exp2(s - _rep(m, n_keys))
        denom = jnp.sum(p, axis=-1, keepdims=True) + jnp.exp2(sink - m)
        o = jnp.dot(p.astype(BF16), v, preferred_element_type=F32) / denom
        o_ref[0, 0, rows, :] = (jnp.where(low, o[:BLOCK], o[BLOCK:]) * gate).astype(BF16)
        return carry

    l_min = _sweep_blocks(SEQ // BLOCK, fast_block, jnp.full((BLOCK, LANES), jnp.inf, F32))

    @pl.when(jnp.logical_not(jnp.min(l_min) >= UNDERFLOW_GUARD))
    def _():
        _sweep_blocks(SEQ // BLOCK, exact_block, 0)


def _nsq_spec(block_of):
    return pl.BlockSpec((1, SEQ // ROW_TILE, 8, 2 * LANES), lambda b, p: (b, 0, 0, block_of(p)))


def _att_a(qkv, z, bias, stats, nsq):
    seq_spec = lambda slot_of: pl.BlockSpec((1, 1, SEQ, LANES), slot_of)
    return pl.pallas_call(
        _att_a_kernel,
        grid=(BATCH, N_PAIRS),
        in_specs=[
            seq_spec(lambda b, p: (b, p, 0, 0)),
            seq_spec(lambda b, p: (b, KA_SLOT, 0, 0)),
            seq_spec(lambda b, p: (b, VA_SLOT, 0, 0)),
            seq_spec(lambda b, p: (b, p, 0, 0)),
            pl.BlockSpec((1, 2 * BLOCK, 2 * BLOCK), lambda b, p: (p, 0, 0)),
            pl.BlockSpec((1, 2, 8, LANES), lambda b, p: (p, 0, 0, 0)),
            _nsq_spec(lambda p: QA_NORM + p),
            _nsq_spec(lambda p: KA_NORM),
        ],
        out_specs=seq_spec(lambda b, p: (b, p, 0, 0)),
        out_shape=jax.ShapeDtypeStruct((BATCH, N_PAIRS, SEQ, LANES), BF16),
        scratch_shapes=[pltpu.VMEM((2 * BLOCK, 2 * BLOCK), F32)],
        compiler_params=_params(2),
        name="att_swa",
    )(qkv, qkv, qkv, z, bias, stats, nsq, nsq)


def _att_b_kernel(q_ref, k_ref, v_ref, g_ref, b1_ref, b4_ref, b16_ref, stat_ref, qsq_ref, ksq_ref, o_ref,
                  m0_ref, m1_ref, l_ref, acc_ref, sb1_ref, sb4_ref, sb16_ref):
    low = _low_half(BLOCK)
    load = lambda ref, rows: ref[0, 0, rows, :].astype(BF16)
    shift = _pair_shift(qsq_ref, ksq_ref, stat_ref)
    for bias_ref, shifted_ref in ((b1_ref, sb1_ref), (b4_ref, sb4_ref), (b16_ref, sb16_ref)):
        shifted_ref[...] = bias_ref[0] - _rep(_stacked_rows(shift, bias_ref.shape[1] // 2), bias_ref.shape[2])

    def sweep(unit, bias1, bias4, bias16):
        def residue16(r, carry):
            rows = pl.ds(r, SEQ // 16, stride=16)
            unit(rows, load(q_ref, rows), load(k_ref, rows), load(v_ref, rows), bias16[...], True)
            return carry

        lax.fori_loop(0, 16, residue16, 0, unroll=2)

        def residue4(f, carry):
            def block4(us, ks, n_keys, c):
                rows, keys = pl.ds(us * 4 + f, BLOCK, stride=4), pl.ds(ks * 4 + f, n_keys, stride=4)
                unit(rows, load(q_ref, rows), load(k_ref, keys), load(v_ref, keys),
                     bias4[:, 2 * BLOCK - n_keys:], False)
                return c

            return _sweep_blocks(SEQ // 4 // BLOCK, block4, carry)

        lax.fori_loop(0, 4, residue4, 0)

        def block1(qs, ks, n_keys, carry):
            rows, keys = pl.ds(qs, BLOCK), pl.ds(ks, n_keys)
            unit(rows, load(q_ref, rows), load(k_ref, keys), load(v_ref, keys),
                 bias1[:, 2 * BLOCK - n_keys:], False)
            return carry

        _sweep_blocks(SEQ // BLOCK, block1, 0)

    def fast_unit(rows, q, k, v, shifted_bias, first):
        low_r = low if q.shape[0] == BLOCK else _low_half(q.shape[0])
        pv, sum0, sum1 = _fast_unit(q, k, v, shifted_bias, low_r)
        sums = jnp.where(low_r, sum0, sum1)
        if first:
            acc_ref[rows, :] = pv
            l_ref[rows, :] = sums
        else:
            acc_ref[rows, :] += pv
            l_ref[rows, :] += sums

    def exact_unit(rows, q, k, v, bias, first):
        r, n_keys = q.shape[0], k.shape[0]
        low_r = low if r == BLOCK else _low_half(r)
        s = _dot_nt(_stack_heads(q), k) + bias
        m_new = jnp.broadcast_to(jnp.max(s, axis=-1, keepdims=True), (2 * r, LANES))
        if not first:
            m_old = jnp.concatenate([m0_ref[rows, :], m1_ref[rows, :]], axis=0)
            m_new = jnp.maximum(m_old, m_new)
            alpha = jnp.exp2(m_old - m_new)
            alpha = jnp.where(low_r, alpha[:r], alpha[r:])
        p = jnp.exp2(s - _rep(m_new, n_keys))
        sums = jnp.sum(p, axis=-1, keepdims=True)
        sums = jnp.where(low_r, sums[:r], sums[r:])
        pv = jnp.dot(p.astype(BF16), v, preferred_element_type=F32)
        pv = jnp.where(low_r, pv[:r], pv[r:])
        m0_ref[rows, :] = m_new[:r]
        m1_ref[rows, :] = m_new[r:]
        if first:
            acc_ref[rows, :] = pv
            l_ref[rows, :] = sums
        else:
            acc_ref[rows, :] = alpha * acc_ref[rows, :] + pv
            l_ref[rows, :] = alpha * l_ref[rows, :] + sums

    def finish(i, l_min):
        rows = pl.ds(pl.multiple_of(i * BLOCK, BLOCK), BLOCK)
        l = l_ref[rows, :]
        gate = g_ref[0, 0, rows, :].astype(F32)
        o_ref[0, 0, rows, :] = (acc_ref[rows, :] / l * gate).astype(BF16)
        return jnp.minimum(l_min, l)

    no_min = jnp.full((BLOCK, LANES), jnp.inf, F32)
    sweep(fast_unit, sb1_ref, sb4_ref, sb16_ref)
    l_min = lax.fori_loop(0, SEQ // BLOCK, finish, no_min, unroll=BAND_UNROLL)

    @pl.when(jnp.logical_not(jnp.min(l_min) >= UNDERFLOW_GUARD))
    def _():
        sweep(exact_unit, b1_ref.at[0], b4_ref.at[0], b16_ref.at[0])
        lax.fori_loop(0, SEQ // BLOCK, finish, no_min)


def _att_b(qkv, z, b1, b4, b16, stats, nsq):
    seq_spec = lambda slot: pl.BlockSpec((1, 1, SEQ, LANES), lambda b, p: (b, slot + p, 0, 0))
    bias_spec = lambda a: pl.BlockSpec((1,) + a.shape[1:], lambda b, p: (p, 0, 0))
    return pl.pallas_call(
        _att_b_kernel,
        grid=(BATCH, N_PAIRS),
        in_specs=[seq_spec(QB_SLOT), seq_spec(KB_SLOT), seq_spec(VB_SLOT), seq_spec(N_PAIRS),
                  bias_spec(b1), bias_spec(b4), bias_spec(b16),
                  pl.BlockSpec((1, 2, 8, LANES), lambda b, p: (p, 0, 0, 0)),
                  _nsq_spec(lambda p: QB_NORM + p), _nsq_spec(lambda p: KB_NORM + p)],
        out_specs=seq_spec(0),
        out_shape=jax.ShapeDtypeStruct((BATCH, N_PAIRS, SEQ, LANES), BF16),
        scratch_shapes=[pltpu.VMEM((SEQ, LANES), F32)] * 4 + [pltpu.VMEM(b.shape[1:], F32) for b in (b1, b4, b16)],
        compiler_params=_params(2),
        name="att_dilated",
    )(qkv, qkv, qkv, z, b1, b4, b16, stats, nsq, nsq)


def _rope_table_kernel(pos_ref, freq_ref, cos_ref, sin_ref):
    ang = pos_ref[0].astype(F32) * freq_ref[...]
    cos_ref[0] = jnp.cos(ang)
    sin_ref[0] = jnp.sin(ang)


def _rope_tables(positions):
    inv_freq = ROPE_THETA ** (-jnp.arange(0, C_ROPE, 2, dtype=F32) / C_ROPE)
    lane = np.arange(LANES)
    freq = jnp.where(jnp.asarray(lane >= C_NOPE), inv_freq[lane % (C_ROPE // 2)], 0.0).reshape(1, LANES)
    table = jax.ShapeDtypeStruct((BATCH, SEQ, LANES), F32)
    spec = pl.BlockSpec((1, ROW_TILE, LANES), lambda b, i: (b, i, 0))
    return pl.pallas_call(
        _rope_table_kernel,
        grid=(BATCH, SEQ // ROW_TILE),
        in_specs=[pl.BlockSpec((1, ROW_TILE, 1), lambda b, i: (b, i, 0)),
                  pl.BlockSpec((1, LANES), lambda b, i: (0, 0))],
        out_specs=[spec, spec],
        out_shape=[table, table],
        compiler_params=_params(2),
        name="rope_tables",
    )(positions.reshape(BATCH, SEQ, 1), freq)


C_LOWRANK = C_Q_RANK + C_KV_RANK + 2 * LANES


def _inproj_odd_kernel(x_ref, nw_ref, w_ref, qnw_ref, wq_ref, kvnw_ref, wk_ref, wv_ref, cos_ref, sin_ref,
                       q_ref, k_ref, v_ref, g_ref, nsq_ref):
    hb = _rms(x_ref[0], nw_ref[...]).astype(BF16)
    c = jnp.dot(hb, w_ref[:, :C_LOWRANK], preferred_element_type=F32)
    for j in range(C_WIDTH // (2 * LANES)):
        lo = C_LOWRANK + j * 2 * LANES
        r = jnp.dot(hb, w_ref[:, lo:lo + 2 * LANES], preferred_element_type=F32)
        for t in range(2):
            g_ref[0, 2 * j + t] = _silu(r[:, t * LANES:(t + 1) * LANES]).astype(BF16)

    qn = _rms(c[:, :C_Q_RANK], qnw_ref[...]).astype(BF16)
    kvn = _rms(c[:, C_Q_RANK:C_Q_RANK + C_KV_RANK], kvnw_ref[...]).astype(BF16)
    k_pe = c[:, C_Q_RANK + C_KV_RANK:C_Q_RANK + C_KV_RANK + LANES]
    k_pe_rot = c[:, C_Q_RANK + C_KV_RANK + LANES:]
    cos, sin = cos_ref[0], sin_ref[0]
    lane = lax.broadcasted_iota(jnp.int32, cos.shape, 1)
    q_table = jnp.where(lane < C_NOPE + C_ROPE, cos, sin) * ((C_NOPE + C_ROPE) ** -0.5 * LOG2E)
    k_rope = k_pe * cos + k_pe_rot * sin

    for j in range(C_HEADS // 2):
        cols = slice(j * 2 * LANES, (j + 1) * 2 * LANES)
        rq = jnp.dot(qn, wq_ref[:, cols], preferred_element_type=F32)
        rk = jnp.dot(kvn, wk_ref[:, cols], preferred_element_type=F32)
        for t in range(2):
            lanes = slice(t * LANES, (t + 1) * LANES)
            q_head, k_head = rq[:, lanes] * q_table, rk[:, lanes] + k_rope
            q_ref[0, 2 * j + t] = q_head.astype(BF16)
            k_ref[0, 2 * j + t] = k_head.astype(BF16)
            _store_norm_stats(nsq_ref, 2 * j + t, q_head, 1)
            _store_norm_stats(nsq_ref, C_HEADS + 2 * j + t, k_head, 1)
    for j in range(C_WIDTH // (2 * LANES)):
        rv = jnp.dot(kvn, wv_ref[:, j * 2 * LANES:(j + 1) * 2 * LANES], preferred_element_type=F32)
        for t in range(2):
            v_ref[0, 2 * j + t] = rv[:, t * LANES:(t + 1) * LANES].astype(BF16)


def _inproj_odd(x, norm_w, w, q_norm, wq, kv_norm, wk, wv, cos, sin):
    full = lambda a: pl.BlockSpec(a.shape, lambda b, i: (0,) * a.ndim)
    row = lambda width: pl.BlockSpec((1, ROW_TILE, width), lambda b, i: (b, i, 0))
    slots = lambda n: pl.BlockSpec((1, n, ROW_TILE, LANES), lambda b, i: (b, 0, i, 0))
    out = lambda n: jax.ShapeDtypeStruct((BATCH, n, SEQ, LANES), BF16)
    n_pairs = C_WIDTH // LANES
    vecs = [norm_w.reshape(1, D_MODEL), q_norm.reshape(1, C_Q_RANK), kv_norm.reshape(1, C_KV_RANK)]
    return pl.pallas_call(
        _inproj_odd_kernel,
        grid=(BATCH, SEQ // ROW_TILE),
        in_specs=[row(D_MODEL), full(vecs[0]), full(w), full(vecs[1]), full(wq), full(vecs[2]), full(wk), full(wv),
                  row(LANES), row(LANES)],
        out_specs=[slots(C_HEADS), slots(C_HEADS), slots(n_pairs), slots(n_pairs),
                   pl.BlockSpec((1, 1, 8, 2 * C_HEADS * LANES), lambda b, i: (b, i, 0, 0))],
        out_shape=[out(C_HEADS), out(C_HEADS), out(n_pairs), out(n_pairs),
                   jax.ShapeDtypeStruct((BATCH, SEQ // ROW_TILE, 8, 2 * C_HEADS * LANES), F32)],
        compiler_params=_params(2),
        name="inproj_mla",
    )(x, vecs[0], w, vecs[1], wq, vecs[2], wk, wv, cos, sin)


V_ROWS = 80


def _flash_kernel(q_ref, k_ref, v_ref, g_ref, qsq_ref, ksq_ref, o_ref, vt_ref, ot_ref,
                  m0_ref, m1_ref, l0_ref, l1_ref, acc_ref):
    t, half = FLASH_TILE, FLASH_TILE // 2
    qi = pl.program_id(2)
    low = _low_half(t)
    m_refs, l_refs = (m0_ref, m1_ref), (l0_ref, l1_ref)

    @pl.when(qi == 0)
    def _():
        row = lax.broadcasted_iota(jnp.int32, (V_ROWS - HEAD_DIM, half), 0)
        ones_row = jnp.where(row == 0, 1.0, 0.0).astype(BF16)

        def prep_body(c, carry):
            rows = pl.ds(pl.multiple_of(c * half, half), half)
            v_t = v_ref[0, 0, rows, :].astype(F32).T
            for hh in range(2):
                vt_ref[hh, :HEAD_DIM, rows] = v_t[hh * HEAD_DIM:(hh + 1) * HEAD_DIM].astype(BF16)
                vt_ref[hh, HEAD_DIM:, rows] = ones_row
            return carry

        lax.fori_loop(0, SEQ // half, prep_body, 0)

    bound = _max_norms(qsq_ref) * _max_norms(ksq_ref)
    shifts = [bound[:, hh * LANES:hh * LANES + 1] for hh in range(2)]
    ot_ref[...] = jnp.zeros(ot_ref.shape, F32)

    def fast_step(ks, width, q_lo=0, q_n=FLASH_TILE, diagonal=False):
        for hh in range(2):
            s_t = _dot_nt(k_ref[0, hh, pl.ds(ks, width), :], q_ref[0, hh, q_lo:q_lo + q_n, :])
            if diagonal:
                causal = (lax.broadcasted_iota(jnp.int32, (width, q_n), 0)
                          <= lax.broadcasted_iota(jnp.int32, (width, q_n), 1))
                s_t = jnp.where(causal, s_t, NEG_INF)
            p_t = jnp.exp2(s_t - shifts[hh]).astype(BF16)
            ot_ref[hh, :, q_lo:q_lo + q_n] += jnp.dot(vt_ref[hh, :, pl.ds(ks, width)], p_t,
                                                      preferred_element_type=F32)

    def fast_body(j, carry):
        fast_step(pl.multiple_of(j * t, t), t)
        return carry

    lax.fori_loop(0, qi, fast_body, 0)
    q_start = pl.multiple_of(qi * t, t)
    fast_step(q_start, half, diagonal=True)
    fast_step(q_start + half, half, q_lo=half, q_n=half, diagonal=True)

    sums = [ot_ref[hh, HEAD_DIM:HEAD_DIM + 1, :] for hh in range(2)]
    healthy = jnp.minimum(jnp.min(sums[0]), jnp.min(sums[1])) >= UNDERFLOW_GUARD
    gate = g_ref[0, 0].astype(F32)

    @pl.when(healthy)
    def _():
        o_t = jnp.concatenate([ot_ref[hh, :HEAD_DIM, :] / sums[hh] for hh in range(2)], axis=0)
        o_ref[0, 0] = (o_t.T * gate).astype(BF16)

    def exact_step(j, carry):
        ks = pl.multiple_of(j * half, half)
        v = v_ref[0, 0, pl.ds(ks, half), :]
        visible = (lax.broadcasted_iota(jnp.int32, (t, half), 1) + (ks - q_start)
                   <= lax.broadcasted_iota(jnp.int32, (t, half), 0))
        pvs, alphas = [], []
        for hh in range(2):
            s = jnp.where(visible, _dot_nt(q_ref[0, hh], k_ref[0, hh, pl.ds(ks, half), :]), NEG_INF)
            m_old = m_refs[hh][...]
            m_new = jnp.maximum(m_old, jnp.max(s, axis=-1, keepdims=True))
            alpha = jnp.exp2(m_old - m_new)
            p = jnp.exp2(s - _rep(m_new, half))
            m_refs[hh][...] = m_new
            l_refs[hh][...] = alpha * l_refs[hh][...] + jnp.sum(p, axis=-1, keepdims=True)
            pvs.append(jnp.dot(p.astype(BF16), v, preferred_element_type=F32))
            alphas.append(alpha)
        acc_ref[...] = jnp.where(low, alphas[0], alphas[1]) * acc_ref[...] + jnp.where(low, pvs[0], pvs[1])
        return carry

    @pl.when(jnp.logical_not(healthy))
    def _():
        for hh in range(2):
            m_refs[hh][...] = jnp.full((t, LANES), NEG_INF, F32)
            l_refs[hh][...] = jnp.zeros((t, LANES), F32)
        acc_ref[...] = jnp.zeros((t, LANES), F32)
        lax.fori_loop(0, 2 * qi + 2, exact_step, 0)
        o = acc_ref[...] / jnp.where(low, l0_ref[...], l1_ref[...])
        o_ref[0, 0] = (o * gate).astype(BF16)


def _flash(q, k, v, z, nsq):
    t = FLASH_TILE
    n_pairs = C_WIDTH // LANES
    tile = pl.BlockSpec((1, 1, t, LANES), lambda b, p, i: (b, p, i, 0))
    return pl.pallas_call(
        _flash_kernel,
        grid=(BATCH, n_pairs, SEQ // t),
        in_specs=[
            pl.BlockSpec((1, 2, t, LANES), lambda b, p, i: (b, p, i, 0)),
            pl.BlockSpec((1, 2, SEQ, LANES), lambda b, p, i: (b, p, 0, 0)),
            pl.BlockSpec((1, 1, SEQ, LANES), lambda b, p, i: (b, p, 0, 0)),
            tile,
            pl.BlockSpec((1, t // ROW_TILE, 8, 2 * LANES), lambda b, p, i: (b, i, 0, p)),
            pl.BlockSpec((1, SEQ // ROW_TILE, 8, 2 * LANES), lambda b, p, i: (b, 0, 0, C_HEADS // 2 + p)),
        ],
        out_specs=tile,
        out_shape=jax.ShapeDtypeStruct((BATCH, n_pairs, SEQ, LANES), BF16),
        scratch_shapes=[pltpu.VMEM((2, V_ROWS, SEQ), BF16), pltpu.VMEM((2, V_ROWS, t), F32)]
        + [pltpu.VMEM((t, LANES), F32)] * 5,
        compiler_params=_params(3),
        name="mla_flash",
    )(q, k, v, z, nsq, nsq)


def _pair_order_cols(w):
    rows = w.shape[0]
    return w.reshape(rows, A_KV_HEADS, N_PAIRS, HEAD_DIM).transpose(0, 2, 1, 3).reshape(rows, A_WIDTH)


def _even_weights(w_in, w_out):
    z_lo = A_WIDTH + 2 * A_KV_WIDTH + 3 * B_WIDTH
    w = jnp.concatenate([_pair_order_cols(w_in[:, :A_WIDTH]), w_in[:, A_WIDTH:z_lo],
                         _pair_order_cols(w_in[:, z_lo:z_lo + A_WIDTH]), w_in[:, z_lo + A_WIDTH:]],
                        axis=1).astype(BF16)
    wo_a = w_out[:A_WIDTH].reshape(A_KV_HEADS, N_PAIRS, HEAD_DIM, D_MODEL).transpose(1, 0, 2, 3)
    w_o = jnp.concatenate([wo_a.reshape(A_WIDTH, D_MODEL), w_out[A_WIDTH:]], axis=0).astype(BF16)
    return w, w_o


def _rot_half_cols(w):
    half = w.shape[1] // 2
    return jnp.concatenate([-w[:, half:], w[:, :half]], axis=1)


def _odd_weights(w_in, w_qb, w_kvb):
    splits = np.cumsum([C_Q_RANK, C_KV_RANK, C_ROPE])
    w_cq, w_ckv, w_kpe, w_z = jnp.split(w_in, splits, axis=1)
    pad = jnp.zeros((D_MODEL, C_NOPE), w_in.dtype)
    w_kpe_rot = _rot_half_cols(w_kpe)
    w = jnp.concatenate([w_cq, w_ckv, pad, w_kpe, w_kpe, pad, w_kpe_rot, w_kpe_rot, w_z], axis=1).astype(BF16)

    wq = w_qb.reshape(C_Q_RANK, C_HEADS, C_NOPE + C_ROPE)
    wq_pe = wq[:, :, C_NOPE:]
    wq_rot = jnp.concatenate([-wq_pe[:, :, C_ROPE // 2:], wq_pe[:, :, :C_ROPE // 2]], axis=2)
    wq = jnp.concatenate([wq, wq_rot], axis=2).reshape(C_Q_RANK, C_HEADS * LANES).astype(BF16)

    wkv = w_kvb.reshape(C_KV_RANK, C_HEADS, C_NOPE + C_V)
    wk = jnp.concatenate([wkv[:, :, :C_NOPE], jnp.zeros((C_KV_RANK, C_HEADS, LANES - C_NOPE), w_kvb.dtype)], axis=2)
    wk = wk.reshape(C_KV_RANK, C_HEADS * LANES).astype(BF16)
    wv = wkv[:, :, C_NOPE:].reshape(C_KV_RANK, C_WIDTH).astype(BF16)
    return w, wq, wk, wv


def kernel(x, positions, norm_w, rel_bias, ab_w_in, ab_sinks, ab_w_out, c_w_in, c_q_norm, c_w_qb, c_kv_norm,
           c_w_kvb, c_w_out, final_norm):
    bias_a = _bias_tiles(rel_bias, A_PAIR_HEADS, 1, BLOCK, 2 * BLOCK, BLOCK)
    bias_b1 = _bias_tiles(rel_bias, B_PAIR_HEADS, 1, BLOCK, 2 * BLOCK, BLOCK)
    bias_b4 = _bias_tiles(rel_bias, B_PAIR_HEADS, 4, BLOCK, 2 * BLOCK, BLOCK)
    bias_b16 = _bias_tiles(rel_bias, B_PAIR_HEADS, 16, SEQ // 16, SEQ // 16, 0)
    stats_b = _head_stats(rel_bias, B_PAIR_HEADS)
    cos, sin = _rope_tables(positions)

    for layer in range(DEPTH):
        i = layer // 2
        last = final_norm if layer == DEPTH - 1 else None
        if layer % 2 == 0:
            w, w_o = _even_weights(ab_w_in[i], ab_w_out[i])
            qkv_a, qkv_b, z, ksq = _inproj_even(x, norm_w[layer], w)
            gs = [_att_a(qkv_a, z, bias_a, _head_stats(rel_bias, A_PAIR_HEADS, ab_sinks[i]), ksq),
                  _att_b(qkv_b, z, bias_b1, bias_b4, bias_b16, stats_b, ksq)]
        else:
            w, wq, wk, wv = _odd_weights(c_w_in[i], c_w_qb[i], c_w_kvb[i])
            q, k, v, z, nsq = _inproj_odd(x, norm_w[layer], w, c_q_norm[i], wq, c_kv_norm[i], wk, wv, cos, sin)
            gs = [_flash(q, k, v, z, nsq)]
            w_o = c_w_out[i].astype(BF16)
        x = _outproj(gs, w_o, x, last)
    return x
```

```python
import functools
import math

import numpy as np
import jax
import jax.numpy as jnp
from jax import lax
from jax.experimental import pallas as pl
from jax.experimental.pallas import tpu as pltpu

D_MODEL = 1024
BATCH = 4
SEQ = 4096
DEPTH = 4
HEAD_DIM = 64
BLOCK = 128
A_HEADS = 8
A_KV_HEADS = 2
A_WINDOW = 128
B_HEADS = 8
B_CONFIGS = ((128, 1), (512, 4), (2048, 16))
NUM_BUCKETS = 32
MAX_DISTANCE = 2048
C_HEADS = 16
C_Q_RANK = 256
C_KV_RANK = 128
C_NOPE = 64
C_ROPE = 32
C_V = 64
ROPE_THETA = 10000.0
EPS = 1e-6

A_WIDTH = A_HEADS * HEAD_DIM
A_KV_WIDTH = A_KV_HEADS * HEAD_DIM
B_WIDTH = B_HEADS * HEAD_DIM
AB_WIDTH = A_WIDTH + B_WIDTH
C_WIDTH = C_HEADS * C_V

LANES = 128
F32 = jnp.float32
BF16 = jnp.bfloat16
NEG_INF = float("-inf")
LOG2E = math.log2(math.e)
UNDERFLOW_GUARD = 2.0 ** -100
NORM_FLOOR = 1e-30

ROW_TILE = 512
FLASH_TILE = 1024
VMEM_LIMIT = 48 * 1024 * 1024

N_PAIRS = 4
KA_SLOT, VA_SLOT, N_A_SLOTS = 4, 5, 6
QB_SLOT, KB_SLOT, VB_SLOT, N_B_SLOTS = 0, 4, 8, 12
BAND_UNROLL = 8
A_PAIR_HEADS = [h for p in range(N_PAIRS) for h in (p, p + N_PAIRS)]
B_PAIR_HEADS = list(range(A_HEADS, A_HEADS + B_HEADS))


def _params(n_axes):
    return pltpu.CompilerParams(dimension_semantics=("arbitrary",) * n_axes,
                                vmem_limit_bytes=VMEM_LIMIT)


def _t5_bucket_np(dist):
    max_exact = NUM_BUCKETS // 2
    d = np.maximum(dist, 1).astype(np.float64)
    large = max_exact + (np.log(d / max_exact) / math.log(MAX_DISTANCE / max_exact)
                         * (NUM_BUCKETS - max_exact)).astype(np.int64)
    large = np.minimum(large, NUM_BUCKETS - 1)
    return np.where(dist < max_exact, dist, large).astype(np.int32)


def _bias_tiles(rel_bias, heads, dilation, rows, cols, offset):
    buckets = _t5_bucket_np(np.arange(BLOCK + 1) * dilation)
    per_head = jnp.take(rel_bias.astype(F32).T, np.asarray(heads), axis=0) * LOG2E
    dist = (lax.broadcasted_iota(jnp.int32, (rows, cols), 0) + offset
            - lax.broadcasted_iota(jnp.int32, (rows, cols), 1))
    tiles = jnp.full((len(heads), rows, cols), NEG_INF, F32)
    for bucket in np.unique(buckets):
        first = int(np.argmax(buckets == bucket))
        tiles = jnp.where((dist >= first)[None], per_head[:, int(bucket)][:, None, None], tiles)
    tiles = jnp.where(((dist >= 0) & (dist <= BLOCK))[None], tiles, NEG_INF)
    return tiles.reshape(len(heads) // 2, 2 * rows, cols)


def _head_stats(rel_bias, heads, sinks=None):
    order = np.asarray(heads)
    bias_max = jnp.take(jnp.max(rel_bias.astype(F32), axis=0), order) * LOG2E
    sink = jnp.zeros_like(bias_max) if sinks is None else jnp.take(sinks.astype(F32), order) * LOG2E
    stats = jnp.zeros((len(heads), 8), F32).at[:, 0].set(sink).at[:, 1].set(bias_max)
    return jnp.broadcast_to(stats[:, :, None], (len(heads), 8, LANES)).reshape(len(heads) // 2, 2, 8, LANES)


def _rms(x, w):
    return x * lax.rsqrt(jnp.mean(x * x, axis=-1, keepdims=True) + EPS) * w


def _rep(stat, width):
    if width == LANES:
        return stat
    return jnp.concatenate([stat] * (width // LANES), axis=1)


def _silu(z):
    return z / (1.0 + jnp.exp(-z))


def _dot_nt(a, b):
    return lax.dot_general(a, b, (((1,), (1,)), ((), ())), preferred_element_type=F32)


def _low_half(rows):
    return lax.broadcasted_iota(jnp.int32, (rows, LANES), 1) < HEAD_DIM


EVEN_NORM_SLOTS = (tuple(range(N_PAIRS)) + (KA_SLOT,)
                   + tuple(N_A_SLOTS + QB_SLOT + p for p in range(N_PAIRS))
                   + tuple(N_A_SLOTS + KB_SLOT + p for p in range(N_PAIRS)))
QA_NORM, KA_NORM, QB_NORM, KB_NORM = 0, N_PAIRS, N_PAIRS + 1, 2 * N_PAIRS + 1


def _store_norm_stats(nsq_ref, block, val, heads_per_slot):
    vb = val.astype(BF16).astype(F32)
    sq = vb * vb
    if heads_per_slot == 1:
        parts = [sq]
    else:
        low = _low_half(val.shape[0])
        parts = [jnp.where(low, sq, 0.0), jnp.where(low, 0.0, sq)]
    for hh, part in enumerate(parts):
        worst = jnp.max(jnp.sum(part, axis=-1, keepdims=True), axis=0, keepdims=True)
        lanes = slice((block * heads_per_slot + hh) * LANES, (block * heads_per_slot + hh + 1) * LANES)
        nsq_ref[0, 0, :, lanes] = jnp.broadcast_to(worst, (8, LANES))


def _max_norms(nsq_ref):
    worst = nsq_ref[0, 0]
    for tile in range(1, nsq_ref.shape[1]):
        worst = jnp.maximum(worst, nsq_ref[0, tile])
    return jnp.sqrt(worst[:1, :])


def _inproj_even_kernel(x_ref, nw_ref, w_ref, a_ref, b_ref, g_ref, nsq_ref):
    hb = _rms(x_ref[0], nw_ref[...]).astype(BF16)
    q_scale = HEAD_DIM ** -0.5 * LOG2E
    n_chunks = w_ref.shape[1] // (2 * LANES)
    for c in range(n_chunks):
        r = jnp.dot(hb, w_ref[:, c * 2 * LANES:(c + 1) * 2 * LANES], preferred_element_type=F32)
        for t in range(2):
            slot = 2 * c + t
            val = r[:, t * LANES:(t + 1) * LANES]
            if slot < N_A_SLOTS:
                val = val * q_scale if slot < N_PAIRS else val
                a_ref[0, slot] = val.astype(BF16)
            elif slot < N_A_SLOTS + N_B_SLOTS:
                val = val * q_scale if slot - N_A_SLOTS < N_PAIRS else val
                b_ref[0, slot - N_A_SLOTS] = val
            else:
                g_ref[0, slot - N_A_SLOTS - N_B_SLOTS] = _silu(val).astype(BF16)
            if slot in EVEN_NORM_SLOTS:
                _store_norm_stats(nsq_ref, EVEN_NORM_SLOTS.index(slot), val, 2)


def _inproj_even(x, norm_w, w):
    n_z = AB_WIDTH // LANES
    slots = lambda n: pl.BlockSpec((1, n, ROW_TILE, LANES), lambda b, i: (b, 0, i, 0))
    out = lambda n, dtype: jax.ShapeDtypeStruct((BATCH, n, SEQ, LANES), dtype)
    return pl.pallas_call(
        _inproj_even_kernel,
        grid=(BATCH, SEQ // ROW_TILE),
        in_specs=[
            pl.BlockSpec((1, ROW_TILE, D_MODEL), lambda b, i: (b, i, 0)),
            pl.BlockSpec((1, D_MODEL), lambda b, i: (0, 0)),
            pl.BlockSpec(w.shape, lambda b, i: (0, 0)),
        ],
        out_specs=[slots(N_A_SLOTS), slots(N_B_SLOTS), slots(n_z),
                   pl.BlockSpec((1, 1, 8, 2 * len(EVEN_NORM_SLOTS) * LANES), lambda b, i: (b, i, 0, 0))],
        out_shape=[out(N_A_SLOTS, BF16), out(N_B_SLOTS, F32), out(n_z, BF16),
                   jax.ShapeDtypeStruct((BATCH, SEQ // ROW_TILE, 8, 2 * len(EVEN_NORM_SLOTS) * LANES), F32)],
        compiler_params=_params(2),
        name="inproj_even",
    )(x, norm_w.reshape(1, D_MODEL), w)


def _outproj_kernel(*refs, n_g, final):
    g_refs, (w_ref, x_ref), rest = refs[:n_g], refs[n_g:n_g + 2], refs[n_g + 2:]
    g = jnp.concatenate([g_ref[0, p] for g_ref in g_refs for p in range(g_ref.shape[1])], axis=1)
    y = x_ref[0] + jnp.dot(g, w_ref[...], preferred_element_type=F32)
    if final:
        fw_ref, o_ref = rest
        y = _rms(y, fw_ref[...])
    else:
        (o_ref,) = rest
    o_ref[0] = y


def _outproj(gs, w, x, final_w=None):
    final = final_w is not None
    in_specs = [pl.BlockSpec((1, g.shape[1], ROW_TILE, LANES), lambda b, i: (b, 0, i, 0)) for g in gs] + [
        pl.BlockSpec(w.shape, lambda b, i: (0, 0)),
        pl.BlockSpec((1, ROW_TILE, D_MODEL), lambda b, i: (b, i, 0)),
    ]
    args = [*gs, w, x]
    if final:
        in_specs.append(pl.BlockSpec((1, D_MODEL), lambda b, i: (0, 0)))
        args.append(final_w.reshape(1, D_MODEL))
    return pl.pallas_call(
        functools.partial(_outproj_kernel, n_g=len(gs), final=final),
        grid=(BATCH, SEQ // ROW_TILE),
        in_specs=in_specs,
        out_specs=pl.BlockSpec((1, ROW_TILE, D_MODEL), lambda b, i: (b, i, 0)),
        out_shape=jax.ShapeDtypeStruct((BATCH, SEQ, D_MODEL), F32),
        compiler_params=_params(2),
        name="outproj_final" if final else "outproj",
    )(*args)


def _stack_heads(q):
    lane = lax.broadcasted_iota(jnp.int32, (1, LANES), 1)
    keep0 = jnp.where(lane < HEAD_DIM, 1.0, 0.0).astype(q.dtype)
    return jnp.concatenate([q * keep0, q * (1.0 - keep0).astype(q.dtype)], axis=0)


def _pair_shift(qsq_ref, ksq_ref, stat_ref):
    bias_max = jnp.concatenate([stat_ref[0, hh, 1:2, :] for hh in range(2)], axis=1)
    return _max_norms(qsq_ref) * _max_norms(ksq_ref) + bias_max


def _stacked_rows(pair_row, rows):
    return jnp.concatenate([jnp.broadcast_to(pair_row[:, hh * LANES:(hh + 1) * LANES], (rows, LANES))
                            for hh in range(2)], axis=0)


def _fast_unit(q, k, v, shifted_bias, low):
    r = q.shape[0]
    p = jnp.exp2(_dot_nt(_stack_heads(q), k) + shifted_bias)
    sums = jnp.sum(p, axis=-1, keepdims=True)
    pv = jnp.dot(p.astype(BF16), v, preferred_element_type=F32)
    return jnp.where(low, pv[:r], pv[r:]), sums[:r], sums[r:]


def _sweep_blocks(n_blocks, block_fn, carry):
    carry = block_fn(0, 0, BLOCK, carry)

    def body(i, c):
        return block_fn(pl.multiple_of(i * BLOCK, BLOCK), pl.multiple_of((i - 1) * BLOCK, BLOCK), 2 * BLOCK, c)

    return lax.fori_loop(1, n_blocks, body, carry, unroll=BAND_UNROLL)


def _att_a_kernel(q_ref, k_ref, v_ref, g_ref, bias_ref, stat_ref, qsq_ref, ksq_ref, o_ref, shifted_bias_ref):
    low = _low_half(BLOCK)
    sink = jnp.concatenate([jnp.broadcast_to(stat_ref[0, hh, 0:1, :], (BLOCK, LANES)) for hh in range(2)], axis=0)
    shift = jnp.maximum(_stacked_rows(_pair_shift(qsq_ref, ksq_ref, stat_ref), BLOCK), sink)
    shifted_bias_ref[...] = bias_ref[0] - _rep(shift, 2 * BLOCK)
    sink_term = jnp.exp2(sink - shift)

    def operands(qs, ks, n_keys, bias_of):
        rows, keys = pl.ds(qs, BLOCK), pl.ds(ks, n_keys)
        return rows, q_ref[0, 0, rows, :], k_ref[0, 0, keys, :], v_ref[0, 0, keys, :], \
            bias_of[:, 2 * BLOCK - n_keys:], g_ref[0, 0, rows, :].astype(F32)

    def fast_block(qs, ks, n_keys, l_min):
        rows, q, k, v, shifted_bias, gate = operands(qs, ks, n_keys, shifted_bias_ref)
        pv, sum0, sum1 = _fast_unit(q, k, v, shifted_bias, low)
        l0, l1 = sum0 + sink_term[:BLOCK], sum1 + sink_term[BLOCK:]
        o_ref[0, 0, rows, :] = (pv / jnp.where(low, l0, l1) * gate).astype(BF16)
        return jnp.minimum(l_min, jnp.minimum(l0, l1))

    def exact_block(qs, ks, n_keys, carry):
        rows, q, k, v, bias, gate = operands(qs, ks, n_keys, bias_ref.at[0])
        s = _dot_nt(_stack_heads(q), k) + bias
        m = jnp.maximum(jnp.max(s, axis=-1, keepdims=True), sink)
        p = jnp.exp2(s - _rep(m, n_keys))
        denom = jnp.sum(p, axis=-1, keepdims=True) + jnp.exp2(sink - m)
        o = jnp.dot(p.astype(BF16), v, preferred_element_type=F32) / denom
        o_ref[0, 0, rows, :] = (jnp.where(low, o[:BLOCK], o[BLOCK:]) * gate).astype(BF16)
        return carry

    l_min = _sweep_blocks(SEQ // BLOCK, fast_block, jnp.full((BLOCK, LANES), jnp.inf, F32))

    @pl.when(jnp.logical_not(jnp.min(l_min) >= UNDERFLOW_GUARD))
    def _():
        _sweep_blocks(SEQ // BLOCK, exact_block, 0)


def _nsq_spec(block_of):
    return pl.BlockSpec((1, SEQ // ROW_TILE, 8, 2 * LANES), lambda b, p: (b, 0, 0, block_of(p)))


def _att_a(qkv, z, bias, stats, nsq):
    seq_spec = lambda slot_of: pl.BlockSpec((1, 1, SEQ, LANES), slot_of)
    return pl.pallas_call(
        _att_a_kernel,
        grid=(BATCH, N_PAIRS),
        in_specs=[
            seq_spec(lambda b, p: (b, p, 0, 0)),
            seq_spec(lambda b, p: (b, KA_SLOT, 0, 0)),
            seq_spec(lambda b, p: (b, VA_SLOT, 0, 0)),
            seq_spec(lambda b, p: (b, p, 0, 0)),
            pl.BlockSpec((1, 2 * BLOCK, 2 * BLOCK), lambda b, p: (p, 0, 0)),
            pl.BlockSpec((1, 2, 8, LANES), lambda b, p: (p, 0, 0, 0)),
            _nsq_spec(lambda p: QA_NORM + p),
            _nsq_spec(lambda p: KA_NORM),
        ],
        out_specs=seq_spec(lambda b, p: (b, p, 0, 0)),
        out_shape=jax.ShapeDtypeStruct((BATCH, N_PAIRS, SEQ, LANES), BF16),
        scratch_shapes=[pltpu.VMEM((2 * BLOCK, 2 * BLOCK), F32)],
        compiler_params=_params(2),
        name="att_swa",
    )(qkv, qkv, qkv, z, bias, stats, nsq, nsq)


def _att_b_kernel(q_ref, k_ref, v_ref, g_ref, b1_ref, b4_ref, b16_ref, stat_ref, qsq_ref, ksq_ref, o_ref,
                  m0_ref, m1_ref, l_ref, acc_ref, sb1_ref, sb4_ref, sb16_ref):
    low = _low_half(BLOCK)
    load = lambda ref, rows: ref[0, 0, rows, :].astype(BF16)
    shift = _pair_shift(qsq_ref, ksq_ref, stat_ref)
    for bias_ref, shifted_ref in ((b1_ref, sb1_ref), (b4_ref, sb4_ref), (b16_ref, sb16_ref)):
        shifted_ref[...] = bias_ref[0] - _rep(_stacked_rows(shift, bias_ref.shape[1] // 2), bias_ref.shape[2])

    def sweep(unit, bias1, bias4, bias16):
        def residue16(r, carry):
            rows = pl.ds(r, SEQ // 16, stride=16)
            unit(rows, load(q_ref, rows), load(k_ref, rows), load(v_ref, rows), bias16[...], True)
            return carry

        lax.fori_loop(0, 16, residue16, 0, unroll=2)

        def residue4(f, carry):
            def block4(us, ks, n_keys, c):
                rows, keys = pl.ds(us * 4 + f, BLOCK, stride=4), pl.ds(ks * 4 + f, n_keys, stride=4)
                unit(rows, load(q_ref, rows), load(k_ref, keys), load(v_ref, keys),
                     bias4[:, 2 * BLOCK - n_keys:], False)
                return c

            return _sweep_blocks(SEQ // 4 // BLOCK, block4, carry)

        lax.fori_loop(0, 4, residue4, 0)

        def block1(qs, ks, n_keys, carry):
            rows, keys = pl.ds(qs, BLOCK), pl.ds(ks, n_keys)
            unit(rows, load(q_ref, rows), load(k_ref, keys), load(v_ref, keys),
                 bias1[:, 2 * BLOCK - n_keys:], False)
            return carry

        _sweep_blocks(SEQ // BLOCK, block1, 0)

    def fast_unit(rows, q, k, v, shifted_bias, first):
        low_r = low if q.shape[0] == BLOCK else _low_half(q.shape[0])
        pv, sum0, sum1 = _fast_unit(q, k, v, shifted_bias, low_r)
        sums = jnp.where(low_r, sum0, sum1)
        if first:
            acc_ref[rows, :] = pv
            l_ref[rows, :] = sums
        else:
            acc_ref[rows, :] += pv
            l_ref[rows, :] += sums

    def exact_unit(rows, q, k, v, bias, first):
        r, n_keys = q.shape[0], k.shape[0]
        low_r = low if r == BLOCK else _low_half(r)
        s = _dot_nt(_stack_heads(q), k) + bias
        m_new = jnp.broadcast_to(jnp.max(s, axis=-1, keepdims=True), (2 * r, LANES))
        if not first:
            m_old = jnp.concatenate([m0_ref[rows, :], m1_ref[rows, :]], axis=0)
            m_new = jnp.maximum(m_old, m_new)
            alpha = jnp.exp2(m_old - m_new)
            alpha = jnp.where(low_r, alpha[:r], alpha[r:])
        p = jnp.exp2(s - _rep(m_new, n_keys))
        sums = jnp.sum(p, axis=-1, keepdims=True)
        sums = jnp.where(low_r, sums[:r], sums[r:])
        pv = jnp.dot(p.astype(BF16), v, preferred_element_type=F32)
        pv = jnp.where(low_r, pv[:r], pv[r:])
        m0_ref[rows, :] = m_new[:r]
        m1_ref[rows, :] = m_new[r:]
        if first:
            acc_ref[rows, :] = pv
            l_ref[rows, :] = sums
        else:
            acc_ref[rows, :] = alpha * acc_ref[rows, :] + pv
            l_ref[rows, :] = alpha * l_ref[rows, :] + sums

    def finish(i, l_min):
        rows = pl.ds(pl.multiple_of(i * BLOCK, BLOCK), BLOCK)
        l = l_ref[rows, :]
        gate = g_ref[0, 0, rows, :].astype(F32)
        o_ref[0, 0, rows, :] = (acc_ref[rows, :] / l * gate).astype(BF16)
        return jnp.minimum(l_min, l)

    no_min = jnp.full((BLOCK, LANES), jnp.inf, F32)
    sweep(fast_unit, sb1_ref, sb4_ref, sb16_ref)
    l_min = lax.fori_loop(0, SEQ // BLOCK, finish, no_min, unroll=BAND_UNROLL)

    @pl.when(jnp.logical_not(jnp.min(l_min) >= UNDERFLOW_GUARD))
    def _():
        sweep(exact_unit, b1_ref.at[0], b4_ref.at[0], b16_ref.at[0])
        lax.fori_loop(0, SEQ // BLOCK, finish, no_min)


def _att_b(qkv, z, b1, b4, b16, stats, nsq):
    seq_spec = lambda slot: pl.BlockSpec((1, 1, SEQ, LANES), lambda b, p: (b, slot + p, 0, 0))
    bias_spec = lambda a: pl.BlockSpec((1,) + a.shape[1:], lambda b, p: (p, 0, 0))
    return pl.pallas_call(
        _att_b_kernel,
        grid=(BATCH, N_PAIRS),
        in_specs=[seq_spec(QB_SLOT), seq_spec(KB_SLOT), seq_spec(VB_SLOT), seq_spec(N_PAIRS),
                  bias_spec(b1), bias_spec(b4), bias_spec(b16),
                  pl.BlockSpec((1, 2, 8, LANES), lambda b, p: (p, 0, 0, 0)),
                  _nsq_spec(lambda p: QB_NORM + p), _nsq_spec(lambda p: KB_NORM + p)],
        out_specs=seq_spec(0),
        out_shape=jax.ShapeDtypeStruct((BATCH, N_PAIRS, SEQ, LANES), BF16),
        scratch_shapes=[pltpu.VMEM((SEQ, LANES), F32)] * 4 + [pltpu.VMEM(b.shape[1:], F32) for b in (b1, b4, b16)],
        compiler_params=_params(2),
        name="att_dilated",
    )(qkv, qkv, qkv, z, b1, b4, b16, stats, nsq, nsq)


def _rope_table_kernel(pos_ref, freq_ref, cos_ref, sin_ref):
    ang = pos_ref[0].astype(F32) * freq_ref[...]
    cos_ref[0] = jnp.cos(ang)
    sin_ref[0] = jnp.sin(ang)


def _rope_tables(positions):
    inv_freq = ROPE_THETA ** (-jnp.arange(0, C_ROPE, 2, dtype=F32) / C_ROPE)
    lane = np.arange(LANES)
    freq = jnp.where(jnp.asarray(lane >= C_NOPE), inv_freq[lane % (C_ROPE // 2)], 0.0).reshape(1, LANES)
    table = jax.ShapeDtypeStruct((BATCH, SEQ, LANES), F32)
    spec = pl.BlockSpec((1, ROW_TILE, LANES), lambda b, i: (b, i, 0))
    return pl.pallas_call(
        _rope_table_kernel,
        grid=(BATCH, SEQ // ROW_TILE),
        in_specs=[pl.BlockSpec((1, ROW_TILE, 1), lambda b, i: (b, i, 0)),
                  pl.BlockSpec((1, LANES), lambda b, i: (0, 0))],
        out_specs=[spec, spec],
        out_shape=[table, table],
        compiler_params=_params(2),
        name="rope_tables",
    )(positions.reshape(BATCH, SEQ, 1), freq)


C_LOWRANK = C_Q_RANK + C_KV_RANK + 2 * LANES


def _inproj_odd_kernel(x_ref, nw_ref, w_ref, qnw_ref, wq_ref, kvnw_ref, wk_ref, wv_ref, cos_ref, sin_ref,
                       q_ref, k_ref, v_ref, g_ref, nsq_ref):
    hb = _rms(x_ref[0], nw_ref[...]).astype(BF16)
    c = jnp.dot(hb, w_ref[:, :C_LOWRANK], preferred_element_type=F32)
    for j in range(C_WIDTH // (2 * LANES)):
        lo = C_LOWRANK + j * 2 * LANES
        r = jnp.dot(hb, w_ref[:, lo:lo + 2 * LANES], preferred_element_type=F32)
        for t in range(2):
            g_ref[0, 2 * j + t] = _silu(r[:, t * LANES:(t + 1) * LANES]).astype(BF16)

    qn = _rms(c[:, :C_Q_RANK], qnw_ref[...]).astype(BF16)
    kvn = _rms(c[:, C_Q_RANK:C_Q_RANK + C_KV_RANK], kvnw_ref[...]).astype(BF16)
    k_pe = c[:, C_Q_RANK + C_KV_RANK:C_Q_RANK + C_KV_RANK + LANES]
    k_pe_rot = c[:, C_Q_RANK + C_KV_RANK + LANES:]
    cos, sin = cos_ref[0], sin_ref[0]
    lane = lax.broadcasted_iota(jnp.int32, cos.shape, 1)
    q_table = jnp.where(lane < C_NOPE + C_ROPE, cos, sin) * ((C_NOPE + C_ROPE) ** -0.5 * LOG2E)
    k_rope = k_pe * cos + k_pe_rot * sin

    for j in range(C_HEADS // 2):
        cols = slice(j * 2 * LANES, (j + 1) * 2 * LANES)
        rq = jnp.dot(qn, wq_ref[:, cols], preferred_element_type=F32)
        rk = jnp.dot(kvn, wk_ref[:, cols], preferred_element_type=F32)
        for t in range(2):
            lanes = slice(t * LANES, (t + 1) * LANES)
            q_head, k_head = rq[:, lanes] * q_table, rk[:, lanes] + k_rope
            q_ref[0, 2 * j + t] = q_head.astype(BF16)
            k_ref[0, 2 * j + t] = k_head.astype(BF16)
            _store_norm_stats(nsq_ref, 2 * j + t, q_head, 1)
            _store_norm_stats(nsq_ref, C_HEADS + 2 * j + t, k_head, 1)
    for j in range(C_WIDTH // (2 * LANES)):
        rv = jnp.dot(kvn, wv_ref[:, j * 2 * LANES:(j + 1) * 2 * LANES], preferred_element_type=F32)
        for t in range(2):
            v_ref[0, 2 * j + t] = rv[:, t * LANES:(t + 1) * LANES].astype(BF16)


def _inproj_odd(x, norm_w, w, q_norm, wq, kv_norm, wk, wv, cos, sin):
    full = lambda a: pl.BlockSpec(a.shape, lambda b, i: (0,) * a.ndim)
    row = lambda width: pl.BlockSpec((1, ROW_TILE, width), lambda b, i: (b, i, 0))
    slots = lambda n: pl.BlockSpec((1, n, ROW_TILE, LANES), lambda b, i: (b, 0, i, 0))
    out = lambda n: jax.ShapeDtypeStruct((BATCH, n, SEQ, LANES), BF16)
    n_pairs = C_WIDTH // LANES
    vecs = [norm_w.reshape(1, D_MODEL), q_norm.reshape(1, C_Q_RANK), kv_norm.reshape(1, C_KV_RANK)]
    return pl.pallas_call(
        _inproj_odd_kernel,
        grid=(BATCH, SEQ // ROW_TILE),
        in_specs=[row(D_MODEL), full(vecs[0]), full(w), full(vecs[1]), full(wq), full(vecs[2]), full(wk), full(wv),
                  row(LANES), row(LANES)],
        out_specs=[slots(C_HEADS), slots(C_HEADS), slots(n_pairs), slots(n_pairs),
                   pl.BlockSpec((1, 1, 8, 2 * C_HEADS * LANES), lambda b, i: (b, i, 0, 0))],
        out_shape=[out(C_HEADS), out(C_HEADS), out(n_pairs), out(n_pairs),
                   jax.ShapeDtypeStruct((BATCH, SEQ // ROW_TILE, 8, 2 * C_HEADS * LANES), F32)],
        compiler_params=_params(2),
        name="inproj_mla",
    )(x, vecs[0], w, vecs[1], wq, vecs[2], wk, wv, cos, sin)


V_ROWS = 80


def _flash_kernel(q_ref, k_ref, v_ref, g_ref, qsq_ref, ksq_ref, o_ref, vt_ref, redo_ref,
                  m0_ref, m1_ref, l0_ref, l1_ref, acc_ref):
    t, half = FLASH_TILE, FLASH_TILE // 2
    n_tiles = SEQ // t
    step = pl.program_id(2)
    low = _low_half(t)
    m_refs, l_refs = (m0_ref, m1_ref), (l0_ref, l1_ref)

    @pl.when(step == 0)
    def _():
        row = lax.broadcasted_iota(jnp.int32, (V_ROWS - HEAD_DIM, half), 0)
        ones_row = jnp.where(row == 0, 1.0, 0.0).astype(BF16)

        def prep_body(c, carry):
            rows = pl.ds(pl.multiple_of(c * half, half), half)
            v_t = v_ref[0, 0, rows, :].astype(F32).T
            for hh in range(2):
                vt_ref[hh, :HEAD_DIM, rows] = v_t[hh * HEAD_DIM:(hh + 1) * HEAD_DIM].astype(BF16)
                vt_ref[hh, HEAD_DIM:, rows] = ones_row
            return carry

        lax.fori_loop(0, SEQ // half, prep_body, 0)

    bound = _max_norms(qsq_ref) * _max_norms(ksq_ref)
    shifts = [bound[:, hh * LANES:hh * LANES + 1] for hh in range(2)]

    def fast_tile(tile):
        qs = tile * t

        def part(hh, ks, width, q_lo, q_n, diagonal):
            s_t = _dot_nt(k_ref[0, hh, ks:ks + width, :], q_ref[0, hh, qs + q_lo:qs + q_lo + q_n, :])
            if diagonal:
                causal = (lax.broadcasted_iota(jnp.int32, (width, q_n), 0)
                          <= lax.broadcasted_iota(jnp.int32, (width, q_n), 1))
                s_t = jnp.where(causal, s_t, NEG_INF)
            p_t = jnp.exp2(s_t - shifts[hh]).astype(BF16)
            return jnp.dot(vt_ref[hh, :, ks:ks + width], p_t, preferred_element_type=F32)

        o_t, sums = [], []
        for hh in range(2):
            acc = part(hh, qs, half, 0, t, True)
            for j in range(tile):
                acc = acc + part(hh, j * t, t, 0, t, False)
            late = acc[:, half:] + part(hh, qs + half, half, half, half, True)
            acc = jnp.concatenate([acc[:, :half], late], axis=1)
            sums.append(acc[HEAD_DIM:HEAD_DIM + 1, :])
            o_t.append(acc[:HEAD_DIM] / sums[hh])
        gate = g_ref[0, 0, qs:qs + t, :].astype(F32)
        o_ref[0, 0, qs:qs + t, :] = (jnp.concatenate(o_t, axis=0).T * gate).astype(BF16)
        return jnp.minimum(jnp.min(sums[0]), jnp.min(sums[1])) >= UNDERFLOW_GUARD

    for first in range(n_tiles // 2):
        @pl.when(step == first)
        def _(first=first):
            healthy = jnp.logical_and(fast_tile(first), fast_tile(n_tiles - 1 - first))
            redo_ref[0] = jnp.where(healthy, 0, 1).astype(jnp.int32)

    def exact_tile(which, carry):
        tile = jnp.where(which == 0, step, n_tiles - 1 - step)
        q_start = pl.multiple_of(tile * t, t)
        q_rows = pl.ds(q_start, t)

        def exact_step(j, c):
            ks = pl.multiple_of(j * half, half)
            v = v_ref[0, 0, pl.ds(ks, half), :]
            visible = (lax.broadcasted_iota(jnp.int32, (t, half), 1) + (ks - q_start)
                       <= lax.broadcasted_iota(jnp.int32, (t, half), 0))
            pvs, alphas = [], []
            for hh in range(2):
                s = jnp.where(visible, _dot_nt(q_ref[0, hh, q_rows, :], k_ref[0, hh, pl.ds(ks, half), :]), NEG_INF)
                m_old = m_refs[hh][...]
                m_new = jnp.maximum(m_old, jnp.max(s, axis=-1, keepdims=True))
                alpha = jnp.exp2(m_old - m_new)
                p = jnp.exp2(s - _rep(m_new, half))
                m_refs[hh][...] = m_new
                l_refs[hh][...] = alpha * l_refs[hh][...] + jnp.sum(p, axis=-1, keepdims=True)
                pvs.append(jnp.dot(p.astype(BF16), v, preferred_element_type=F32))
                alphas.append(alpha)
            acc_ref[...] = jnp.where(low, alphas[0], alphas[1]) * acc_ref[...] + jnp.where(low, pvs[0], pvs[1])
            return c

        for hh in range(2):
            m_refs[hh][...] = jnp.full((t, LANES), NEG_INF, F32)
            l_refs[hh][...] = jnp.zeros((t, LANES), F32)
        acc_ref[...] = jnp.zeros((t, LANES), F32)
        lax.fori_loop(0, 2 * tile + 2, exact_step, 0)
        o = acc_ref[...] / jnp.where(low, l0_ref[...], l1_ref[...])
        o_ref[0, 0, q_rows, :] = (o * g_ref[0, 0, q_rows, :].astype(F32)).astype(BF16)
        return carry

    @pl.when(redo_ref[0] != 0)
    def _():
        lax.fori_loop(0, 2, exact_tile, 0)


def _flash(q, k, v, z, nsq):
    t = FLASH_TILE
    n_pairs = C_WIDTH // LANES
    heads = pl.BlockSpec((1, 2, SEQ, LANES), lambda b, p, i: (b, p, 0, 0))
    pair = pl.BlockSpec((1, 1, SEQ, LANES), lambda b, p, i: (b, p, 0, 0))
    return pl.pallas_call(
        _flash_kernel,
        grid=(BATCH, n_pairs, SEQ // t // 2),
        in_specs=[
            heads, heads, pair, pair,
            pl.BlockSpec((1, SEQ // ROW_TILE, 8, 2 * LANES), lambda b, p, i: (b, 0, 0, p)),
            pl.BlockSpec((1, SEQ // ROW_TILE, 8, 2 * LANES), lambda b, p, i: (b, 0, 0, C_HEADS // 2 + p)),
        ],
        out_specs=pair,
        out_shape=jax.ShapeDtypeStruct((BATCH, n_pairs, SEQ, LANES), BF16),
        scratch_shapes=[pltpu.VMEM((2, V_ROWS, SEQ), BF16), pltpu.SMEM((1,), jnp.int32)]
        + [pltpu.VMEM((t, LANES), F32)] * 5,
        compiler_params=_params(3),
        name="mla_flash",
    )(q, k, v, z, nsq, nsq)


def _pair_order_cols(w):
    rows = w.shape[0]
    return w.reshape(rows, A_KV_HEADS, N_PAIRS, HEAD_DIM).transpose(0, 2, 1, 3).reshape(rows, A_WIDTH)


def _even_weights(w_in, w_out):
    z_lo = A_WIDTH + 2 * A_KV_WIDTH + 3 * B_WIDTH
    w = jnp.concatenate([_pair_order_cols(w_in[:, :A_WIDTH]), w_in[:, A_WIDTH:z_lo],
                         _pair_order_cols(w_in[:, z_lo:z_lo + A_WIDTH]), w_in[:, z_lo + A_WIDTH:]],
                        axis=1).astype(BF16)
    wo_a = w_out[:A_WIDTH].reshape(A_KV_HEADS, N_PAIRS, HEAD_DIM, D_MODEL).transpose(1, 0, 2, 3)
    w_o = jnp.concatenate([wo_a.reshape(A_WIDTH, D_MODEL), w_out[A_WIDTH:]], axis=0).astype(BF16)
    return w, w_o


def _rot_half_cols(w):
    half = w.shape[1] // 2
    return jnp.concatenate([-w[:, half:], w[:, :half]], axis=1)


def _odd_weights(w_in, w_qb, w_kvb):
    splits = np.cumsum([C_Q_RANK, C_KV_RANK, C_ROPE])
    w_cq, w_ckv, w_kpe, w_z = jnp.split(w_in, splits, axis=1)
    pad = jnp.zeros((D_MODEL, C_NOPE), w_in.dtype)
    w_kpe_rot = _rot_half_cols(w_kpe)
    w = jnp.concatenate([w_cq, w_ckv, pad, w_kpe, w_kpe, pad, w_kpe_rot, w_kpe_rot, w_z], axis=1).astype(BF16)

    wq = w_qb.reshape(C_Q_RANK, C_HEADS, C_NOPE + C_ROPE)
    wq_pe = wq[:, :, C_NOPE:]
    wq_rot = jnp.concatenate([-wq_pe[:, :, C_ROPE // 2:], wq_pe[:, :, :C_ROPE // 2]], axis=2)
    wq = jnp.concatenate([wq, wq_rot], axis=2).reshape(C_Q_RANK, C_HEADS * LANES).astype(BF16)

    wkv = w_kvb.reshape(C_KV_RANK, C_HEADS, C_NOPE + C_V)
    wk = jnp.concatenate([wkv[:, :, :C_NOPE], jnp.zeros((C_KV_RANK, C_HEADS, LANES - C_NOPE), w_kvb.dtype)], axis=2)
    wk = wk.reshape(C_KV_RANK, C_HEADS * LANES).astype(BF16)
    wv = wkv[:, :, C_NOPE:].reshape(C_KV_RANK, C_WIDTH).astype(BF16)
    return w, wq, wk, wv


def kernel(x, positions, norm_w, rel_bias, ab_w_in, ab_sinks, ab_w_out, c_w_in, c_q_norm, c_w_qb, c_kv_norm,
           c_w_kvb, c_w_out, final_norm):
    bias_a = _bias_tiles(rel_bias, A_PAIR_HEADS, 1, BLOCK, 2 * BLOCK, BLOCK)
    bias_b1 = _bias_tiles(rel_bias, B_PAIR_HEADS, 1, BLOCK, 2 * BLOCK, BLOCK)
    bias_b4 = _bias_tiles(rel_bias, B_PAIR_HEADS, 4, BLOCK, 2 * BLOCK, BLOCK)
    bias_b16 = _bias_tiles(rel_bias, B_PAIR_HEADS, 16, SEQ // 16, SEQ // 16, 0)
    stats_b = _head_stats(rel_bias, B_PAIR_HEADS)
    cos, sin = _rope_tables(positions)

    for layer in range(DEPTH):
        i = layer // 2
        last = final_norm if layer == DEPTH - 1 else None
        if layer % 2 == 0:
            w, w_o = _even_weights(ab_w_in[i], ab_w_out[i])
            qkv_a, qkv_b, z, ksq = _inproj_even(x, norm_w[layer], w)
            gs = [_att_a(qkv_a, z, bias_a, _head_stats(rel_bias, A_PAIR_HEADS, ab_sinks[i]), ksq),
                  _att_b(qkv_b, z, bias_b1, bias_b4, bias_b16, stats_b, ksq)]
        else:
            w, wq, wk, wv = _odd_weights(c_w_in[i], c_w_qb[i], c_w_kvb[i])
            q, k, v, z, nsq = _inproj_odd(x, norm_w[layer], w, c_q_norm[i], wq, c_kv_norm[i], wk, wv, cos, sin)
            gs = [_flash(q, k, v, z, nsq)]
            w_o = c_w_out[i].astype(BF16)
        x = _outproj(gs, w_o, x, last)
    return x
```

```python
import functools
import math

import numpy as np
import jax
import jax.numpy as jnp
from jax import lax
from jax.experimental import pallas as pl
from jax.experimental.pallas import tpu as pltpu

D_MODEL = 1024
BATCH = 4
SEQ = 4096
DEPTH = 4
HEAD_DIM = 64
BLOCK = 128
A_HEADS = 8
A_KV_HEADS = 2
A_WINDOW = 128
B_HEADS = 8
B_CONFIGS = ((128, 1), (512, 4), (2048, 16))
NUM_BUCKETS = 32
MAX_DISTANCE = 2048
C_HEADS = 16
C_Q_RANK = 256
C_KV_RANK = 128
C_NOPE = 64
C_ROPE = 32
C_V = 64
ROPE_THETA = 10000.0
EPS = 1e-6

A_WIDTH = A_HEADS * HEAD_DIM
A_KV_WIDTH = A_KV_HEADS * HEAD_DIM
B_WIDTH = B_HEADS * HEAD_DIM
AB_WIDTH = A_WIDTH + B_WIDTH
C_WIDTH = C_HEADS * C_V

LANES = 128
F32 = jnp.float32
BF16 = jnp.bfloat16
NEG_INF = float("-inf")
LOG2E = math.log2(math.e)
UNDERFLOW_GUARD = 2.0 ** -100
NORM_FLOOR = 1e-30

ROW_TILE = 512
FLASH_TILE = 1024
VMEM_LIMIT = 48 * 1024 * 1024

N_PAIRS = 4
KA_SLOT, VA_SLOT, N_A_SLOTS = 4, 5, 6
QB_SLOT, KB_SLOT, VB_SLOT, N_B_SLOTS = 0, 4, 8, 12
BAND_UNROLL = 8
A_PAIR_HEADS = [h for p in range(N_PAIRS) for h in (p, p + N_PAIRS)]
B_PAIR_HEADS = list(range(A_HEADS, A_HEADS + B_HEADS))


def _params(n_axes):
    return pltpu.CompilerParams(dimension_semantics=("arbitrary",) * n_axes,
                                vmem_limit_bytes=VMEM_LIMIT)


def _t5_bucket_np(dist):
    max_exact = NUM_BUCKETS // 2
    d = np.maximum(dist, 1).astype(np.float64)
    large = max_exact + (np.log(d / max_exact) / math.log(MAX_DISTANCE / max_exact)
                         * (NUM_BUCKETS - max_exact)).astype(np.int64)
    large = np.minimum(large, NUM_BUCKETS - 1)
    return np.where(dist < max_exact, dist, large).astype(np.int32)


def _bias_tiles(rel_bias, heads, dilation, rows, cols, offset):
    buckets = _t5_bucket_np(np.arange(BLOCK + 1) * dilation)
    per_head = jnp.take(rel_bias.astype(F32).T, np.asarray(heads), axis=0) * LOG2E
    dist = (lax.broadcasted_iota(jnp.int32, (rows, cols), 0) + offset
            - lax.broadcasted_iota(jnp.int32, (rows, cols), 1))
    tiles = jnp.full((len(heads), rows, cols), NEG_INF, F32)
    for bucket in np.unique(buckets):
        first = int(np.argmax(buckets == bucket))
        tiles = jnp.where((dist >= first)[None], per_head[:, int(bucket)][:, None, None], tiles)
    tiles = jnp.where(((dist >= 0) & (dist <= BLOCK))[None], tiles, NEG_INF)
    return tiles.reshape(len(heads) // 2, 2 * rows, cols)


def _head_stats(rel_bias, heads, sinks=None):
    order = np.asarray(heads)
    bias_max = jnp.take(jnp.max(rel_bias.astype(F32), axis=0), order) * LOG2E
    sink = jnp.zeros_like(bias_max) if sinks is None else jnp.take(sinks.astype(F32), order) * LOG2E
    stats = jnp.zeros((len(heads), 8), F32).at[:, 0].set(sink).at[:, 1].set(bias_max)
    return jnp.broadcast_to(stats[:, :, None], (len(heads), 8, LANES)).reshape(len(heads) // 2, 2, 8, LANES)


def _rms(x, w):
    return x * lax.rsqrt(jnp.mean(x * x, axis=-1, keepdims=True) + EPS) * w


def _rep(stat, width):
    if width == LANES:
        return stat
    return jnp.concatenate([stat] * (width // LANES), axis=1)


def _silu(z):
    return z / (1.0 + jnp.exp(-z))


def _dot_nt(a, b):
    return lax.dot_general(a, b, (((1,), (1,)), ((), ())), preferred_element_type=F32)


def _low_half(rows):
    return lax.broadcasted_iota(jnp.int32, (rows, LANES), 1) < HEAD_DIM


EVEN_NORM_SLOTS = (tuple(range(N_PAIRS)) + (KA_SLOT,)
                   + tuple(N_A_SLOTS + QB_SLOT + p for p in range(N_PAIRS))
                   + tuple(N_A_SLOTS + KB_SLOT + p for p in range(N_PAIRS)))
QA_NORM, KA_NORM, QB_NORM, KB_NORM = 0, N_PAIRS, N_PAIRS + 1, 2 * N_PAIRS + 1


def _store_norm_stats(nsq_ref, block, val, heads_per_slot):
    vb = val.astype(BF16).astype(F32)
    sq = vb * vb
    if heads_per_slot == 1:
        parts = [sq]
    else:
        low = _low_half(val.shape[0])
        parts = [jnp.where(low, sq, 0.0), jnp.where(low, 0.0, sq)]
    for hh, part in enumerate(parts):
        worst = jnp.max(jnp.sum(part, axis=-1, keepdims=True), axis=0, keepdims=True)
        lanes = slice((block * heads_per_slot + hh) * LANES, (block * heads_per_slot + hh + 1) * LANES)
        nsq_ref[0, 0, :, lanes] = jnp.broadcast_to(worst, (8, LANES))


def _max_norms(nsq_ref):
    worst = nsq_ref[0, 0]
    for tile in range(1, nsq_ref.shape[1]):
        worst = jnp.maximum(worst, nsq_ref[0, tile])
    return jnp.sqrt(worst[:1, :])


def _gated_rows(g_refs):
    return jnp.concatenate([g_ref[0, p] for g_ref in g_refs for p in range(g_ref.shape[1])], axis=1)


def _projection_kernel(*refs, n_g, n_in, body):
    if n_g:
        g_refs, w_out_ref, x_ref, rest = refs[:n_g], refs[n_g], refs[n_g + 1], refs[n_g + 2:]
        ins, x_out_ref, outs = rest[:n_in], rest[n_in], rest[n_in + 1:]
        x = x_ref[0] + jnp.dot(_gated_rows(g_refs), w_out_ref[...], preferred_element_type=F32)
        x_out_ref[0] = x
    else:
        x, ins, outs = refs[0][0], refs[1:1 + n_in], refs[1 + n_in:]
    body(x, *ins, *outs)


def _projection_call(body, name, x, prev, ins, in_specs, out_specs, out_shape):
    x_spec = pl.BlockSpec((1, ROW_TILE, D_MODEL), lambda b, i: (b, i, 0))
    lead, lead_specs = [x], [x_spec]
    if prev is not None:
        gs, w_out = prev
        lead = [*gs, w_out, x]
        lead_specs = [pl.BlockSpec((1, g.shape[1], ROW_TILE, LANES), lambda b, i: (b, 0, i, 0)) for g in gs] + [
            pl.BlockSpec(w_out.shape, lambda b, i: (0, 0)), x_spec]
        out_specs = [x_spec] + out_specs
        out_shape = [jax.ShapeDtypeStruct(x.shape, F32)] + out_shape
    return pl.pallas_call(
        functools.partial(_projection_kernel, n_g=0 if prev is None else len(prev[0]), n_in=len(ins), body=body),
        grid=(BATCH, SEQ // ROW_TILE),
        in_specs=lead_specs + in_specs,
        out_specs=out_specs,
        out_shape=out_shape,
        compiler_params=_params(2),
        name=name if prev is None else "residual_" + name,
    )(*lead, *ins)


def _inproj_even_body(x, nw_ref, w_ref, a_ref, b_ref, g_ref, nsq_ref):
    hb = _rms(x, nw_ref[...]).astype(BF16)
    q_scale = HEAD_DIM ** -0.5 * LOG2E
    n_chunks = w_ref.shape[1] // (2 * LANES)
    for c in range(n_chunks):
        r = jnp.dot(hb, w_ref[:, c * 2 * LANES:(c + 1) * 2 * LANES], preferred_element_type=F32)
        for t in range(2):
            slot = 2 * c + t
            val = r[:, t * LANES:(t + 1) * LANES]
            if slot < N_A_SLOTS:
                val = val * q_scale if slot < N_PAIRS else val
                a_ref[0, slot] = val.astype(BF16)
            elif slot < N_A_SLOTS + N_B_SLOTS:
                val = val * q_scale if slot - N_A_SLOTS < N_PAIRS else val
                b_ref[0, slot - N_A_SLOTS] = val
            else:
                g_ref[0, slot - N_A_SLOTS - N_B_SLOTS] = _silu(val).astype(BF16)
            if slot in EVEN_NORM_SLOTS:
                _store_norm_stats(nsq_ref, EVEN_NORM_SLOTS.index(slot), val, 2)


def _inproj_even(x, norm_w, w, prev=None):
    n_z = AB_WIDTH // LANES
    slots = lambda n: pl.BlockSpec((1, n, ROW_TILE, LANES), lambda b, i: (b, 0, i, 0))
    out = lambda n, dtype: jax.ShapeDtypeStruct((BATCH, n, SEQ, LANES), dtype)
    return _projection_call(
        _inproj_even_body, "inproj_even", x, prev,
        ins=[norm_w.reshape(1, D_MODEL), w],
        in_specs=[pl.BlockSpec((1, D_MODEL), lambda b, i: (0, 0)), pl.BlockSpec(w.shape, lambda b, i: (0, 0))],
        out_specs=[slots(N_A_SLOTS), slots(N_B_SLOTS), slots(n_z),
                   pl.BlockSpec((1, 1, 8, 2 * len(EVEN_NORM_SLOTS) * LANES), lambda b, i: (b, i, 0, 0))],
        out_shape=[out(N_A_SLOTS, BF16), out(N_B_SLOTS, F32), out(n_z, BF16),
                   jax.ShapeDtypeStruct((BATCH, SEQ // ROW_TILE, 8, 2 * len(EVEN_NORM_SLOTS) * LANES), F32)])


def _outproj_kernel(*refs, n_g, final):
    g_refs, (w_ref, x_ref), rest = refs[:n_g], refs[n_g:n_g + 2], refs[n_g + 2:]
    y = x_ref[0] + jnp.dot(_gated_rows(g_refs), w_ref[...], preferred_element_type=F32)
    if final:
        fw_ref, o_ref = rest
        y = _rms(y, fw_ref[...])
    else:
        (o_ref,) = rest
    o_ref[0] = y


def _outproj(gs, w, x, final_w=None):
    final = final_w is not None
    in_specs = [pl.BlockSpec((1, g.shape[1], ROW_TILE, LANES), lambda b, i: (b, 0, i, 0)) for g in gs] + [
        pl.BlockSpec(w.shape, lambda b, i: (0, 0)),
        pl.BlockSpec((1, ROW_TILE, D_MODEL), lambda b, i: (b, i, 0)),
    ]
    args = [*gs, w, x]
    if final:
        in_specs.append(pl.BlockSpec((1, D_MODEL), lambda b, i: (0, 0)))
        args.append(final_w.reshape(1, D_MODEL))
    return pl.pallas_call(
        functools.partial(_outproj_kernel, n_g=len(gs), final=final),
        grid=(BATCH, SEQ // ROW_TILE),
        in_specs=in_specs,
        out_specs=pl.BlockSpec((1, ROW_TILE, D_MODEL), lambda b, i: (b, i, 0)),
        out_shape=jax.ShapeDtypeStruct((BATCH, SEQ, D_MODEL), F32),
        compiler_params=_params(2),
        name="outproj_final" if final else "outproj",
    )(*args)


def _stack_heads(q):
    lane = lax.broadcasted_iota(jnp.int32, (1, LANES), 1)
    keep0 = jnp.where(lane < HEAD_DIM, 1.0, 0.0).astype(q.dtype)
    return jnp.concatenate([q * keep0, q * (1.0 - keep0).astype(q.dtype)], axis=0)


def _pair_shift(qsq_ref, ksq_ref, stat_ref):
    bias_max = jnp.concatenate([stat_ref[0, hh, 1:2, :] for hh in range(2)], axis=1)
    return _max_norms(qsq_ref) * _max_norms(ksq_ref) + bias_max


def _stacked_rows(pair_row, rows):
    return jnp.concatenate([jnp.broadcast_to(pair_row[:, hh * LANES:(hh + 1) * LANES], (rows, LANES))
                            for hh in range(2)], axis=0)


def _fast_unit(q, k, v, shifted_bias, low):
    r = q.shape[0]
    p = jnp.exp2(_dot_nt(_stack_heads(q), k) + shifted_bias)
    sums = jnp.sum(p, axis=-1, keepdims=True)
    pv = jnp.dot(p.astype(BF16), v, preferred_element_type=F32)
    return jnp.where(low, pv[:r], pv[r:]), sums[:r], sums[r:]


def _sweep_blocks(n_blocks, block_fn, carry):
    carry = block_fn(0, 0, BLOCK, carry)

    def body(i, c):
        return block_fn(pl.multiple_of(i * BLOCK, BLOCK), pl.multiple_of((i - 1) * BLOCK, BLOCK), 2 * BLOCK, c)

    return lax.fori_loop(1, n_blocks, body, carry, unroll=BAND_UNROLL)


def _att_a_kernel(q_ref, k_ref, v_ref, g_ref, bias_ref, stat_ref, qsq_ref, ksq_ref, o_ref, shifted_bias_ref):
    low = _low_half(BLOCK)
    sink = jnp.concatenate([jnp.broadcast_to(stat_ref[0, hh, 0:1, :], (BLOCK, LANES)) for hh in range(2)], axis=0)
    shift = jnp.maximum(_stacked_rows(_pair_shift(qsq_ref, ksq_ref, stat_ref), BLOCK), sink)
    shifted_bias_ref[...] = bias_ref[0] - _rep(shift, 2 * BLOCK)
    sink_term = jnp.exp2(sink - shift)

    def operands(qs, ks, n_keys, bias_of):
        rows, keys = pl.ds(qs, BLOCK), pl.ds(ks, n_keys)
        return rows, q_ref[0, 0, rows, :], k_ref[0, 0, keys, :], v_ref[0, 0, keys, :], \
            bias_of[:, 2 * BLOCK - n_keys:], g_ref[0, 0, rows, :].astype(F32)

    def fast_block(qs, ks, n_keys, l_min):
        rows, q, k, v, shifted_bias, gate = operands(qs, ks, n_keys, shifted_bias_ref)
        pv, sum0, sum1 = _fast_unit(q, k, v, shifted_bias, low)
        l0, l1 = sum0 + sink_term[:BLOCK], sum1 + sink_term[BLOCK:]
        o_ref[0, 0, rows, :] = (pv / jnp.where(low, l0, l1) * gate).astype(BF16)
        return jnp.minimum(l_min, jnp.minimum(l0, l1))

    def exact_block(qs, ks, n_keys, carry):
        rows, q, k, v, bias, gate = operands(qs, ks, n_keys, bias_ref.at[0])
        s = _dot_nt(_stack_heads(q), k) + bias
        m = jnp.maximum(jnp.max(s, axis=-1, keepdims=True), sink)
        p = jnp.exp2(s - _rep(m, n_keys))
        denom = jnp.sum(p, axis=-1, keepdims=True) + jnp.exp2(sink - m)
        o = jnp.dot(p.astype(BF16), v, preferred_element_type=F32) / denom
        o_ref[0, 0, rows, :] = (jnp.where(low, o[:BLOCK], o[BLOCK:]) * gate).astype(BF16)
        return carry

    l_min = _sweep_blocks(SEQ // BLOCK, fast_block, jnp.full((BLOCK, LANES), jnp.inf, F32))

    @pl.when(jnp.logical_not(jnp.min(l_min) >= UNDERFLOW_GUARD))
    def _():
        _sweep_blocks(SEQ // BLOCK, exact_block, 0)


def _nsq_spec(block_of):
    return pl.BlockSpec((1, SEQ // ROW_TILE, 8, 2 * LANES), lambda b, p: (b, 0, 0, block_of(p)))


def _att_a(qkv, z, bias, stats, nsq):
    seq_spec = lambda slot_of: pl.BlockSpec((1, 1, SEQ, LANES), slot_of)
    return pl.pallas_call(
        _att_a_kernel,
        grid=(BATCH, N_PAIRS),
        in_specs=[
            seq_spec(lambda b, p: (b, p, 0, 0)),
            seq_spec(lambda b, p: (b, KA_SLOT, 0, 0)),
            seq_spec(lambda b, p: (b, VA_SLOT, 0, 0)),
            seq_spec(lambda b, p: (b, p, 0, 0)),
            pl.BlockSpec((1, 2 * BLOCK, 2 * BLOCK), lambda b, p: (p, 0, 0)),
            pl.BlockSpec((1, 2, 8, LANES), lambda b, p: (p, 0, 0, 0)),
            _nsq_spec(lambda p: QA_NORM + p),
            _nsq_spec(lambda p: KA_NORM),
        ],
        out_specs=seq_spec(lambda b, p: (b, p, 0, 0)),
        out_shape=jax.ShapeDtypeStruct((BATCH, N_PAIRS, SEQ, LANES), BF16),
        scratch_shapes=[pltpu.VMEM((2 * BLOCK, 2 * BLOCK), F32)],
        compiler_params=_params(2),
        name="att_swa",
    )(qkv, qkv, qkv, z, bias, stats, nsq, nsq)


def _att_b_kernel(q_ref, k_ref, v_ref, g_ref, b1_ref, b4_ref, b16_ref, stat_ref, qsq_ref, ksq_ref, o_ref,
                  m0_ref, m1_ref, l_ref, acc_ref, sb1_ref, sb4_ref, sb16_ref):
    low = _low_half(BLOCK)
    load = lambda ref, rows: ref[0, 0, rows, :].astype(BF16)
    shift = _pair_shift(qsq_ref, ksq_ref, stat_ref)
    for bias_ref, shifted_ref in ((b1_ref, sb1_ref), (b4_ref, sb4_ref), (b16_ref, sb16_ref)):
        shifted_ref[...] = bias_ref[0] - _rep(_stacked_rows(shift, bias_ref.shape[1] // 2), bias_ref.shape[2])

    def sweep(unit, bias1, bias4, bias16):
        def residue16(r, carry):
            rows = pl.ds(r, SEQ // 16, stride=16)
            unit(rows, load(q_ref, rows), load(k_ref, rows), load(v_ref, rows), bias16[...], True)
            return carry

        lax.fori_loop(0, 16, residue16, 0, unroll=2)

        def residue4(f, carry):
            def block4(us, ks, n_keys, c):
                rows, keys = pl.ds(us * 4 + f, BLOCK, stride=4), pl.ds(ks * 4 + f, n_keys, stride=4)
                unit(rows, load(q_ref, rows), load(k_ref, keys), load(v_ref, keys),
                     bias4[:, 2 * BLOCK - n_keys:], False)
                return c

            return _sweep_blocks(SEQ // 4 // BLOCK, block4, carry)

        lax.fori_loop(0, 4, residue4, 0)

        def block1(qs, ks, n_keys, carry):
            rows, keys = pl.ds(qs, BLOCK), pl.ds(ks, n_keys)
            unit(rows, load(q_ref, rows), load(k_ref, keys), load(v_ref, keys),
                 bias1[:, 2 * BLOCK - n_keys:], False)
            return carry

        _sweep_blocks(SEQ // BLOCK, block1, 0)

    def fast_unit(rows, q, k, v, shifted_bias, first):
        low_r = low if q.shape[0] == BLOCK else _low_half(q.shape[0])
        pv, sum0, sum1 = _fast_unit(q, k, v, shifted_bias, low_r)
        sums = jnp.where(low_r, sum0, sum1)
        if first:
            acc_ref[rows, :] = pv
            l_ref[rows, :] = sums
        else:
            acc_ref[rows, :] += pv
            l_ref[rows, :] += sums

    def exact_unit(rows, q, k, v, bias, first):
        r, n_keys = q.shape[0], k.shape[0]
        low_r = low if r == BLOCK else _low_half(r)
        s = _dot_nt(_stack_heads(q), k) + bias
        m_new = jnp.broadcast_to(jnp.max(s, axis=-1, keepdims=True), (2 * r, LANES))
        if not first:
            m_old = jnp.concatenate([m0_ref[rows, :], m1_ref[rows, :]], axis=0)
            m_new = jnp.maximum(m_old, m_new)
            alpha = jnp.exp2(m_old - m_new)
            alpha = jnp.where(low_r, alpha[:r], alpha[r:])
        p = jnp.exp2(s - _rep(m_new, n_keys))
        sums = jnp.sum(p, axis=-1, keepdims=True)
        sums = jnp.where(low_r, sums[:r], sums[r:])
        pv = jnp.dot(p.astype(BF16), v, preferred_element_type=F32)
        pv = jnp.where(low_r, pv[:r], pv[r:])
        m0_ref[rows, :] = m_new[:r]
        m1_ref[rows, :] = m_new[r:]
        if first:
            acc_ref[rows, :] = pv
            l_ref[rows, :] = sums
        else:
            acc_ref[rows, :] = alpha * acc_ref[rows, :] + pv
            l_ref[rows, :] = alpha * l_ref[rows, :] + sums

    def finish(i, l_min):
        rows = pl.ds(pl.multiple_of(i * BLOCK, BLOCK), BLOCK)
        l = l_ref[rows, :]
        gate = g_ref[0, 0, rows, :].astype(F32)
        o_ref[0, 0, rows, :] = (acc_ref[rows, :] / l * gate).astype(BF16)
        return jnp.minimum(l_min, l)

    no_min = jnp.full((BLOCK, LANES), jnp.inf, F32)
    sweep(fast_unit, sb1_ref, sb4_ref, sb16_ref)
    l_min = lax.fori_loop(0, SEQ // BLOCK, finish, no_min, unroll=BAND_UNROLL)

    @pl.when(jnp.logical_not(jnp.min(l_min) >= UNDERFLOW_GUARD))
    def _():
        sweep(exact_unit, b1_ref.at[0], b4_ref.at[0], b16_ref.at[0])
        lax.fori_loop(0, SEQ // BLOCK, finish, no_min)


def _att_b(qkv, z, b1, b4, b16, stats, nsq):
    seq_spec = lambda slot: pl.BlockSpec((1, 1, SEQ, LANES), lambda b, p: (b, slot + p, 0, 0))
    bias_spec = lambda a: pl.BlockSpec((1,) + a.shape[1:], lambda b, p: (p, 0, 0))
    return pl.pallas_call(
        _att_b_kernel,
        grid=(BATCH, N_PAIRS),
        in_specs=[seq_spec(QB_SLOT), seq_spec(KB_SLOT), seq_spec(VB_SLOT), seq_spec(N_PAIRS),
                  bias_spec(b1), bias_spec(b4), bias_spec(b16),
                  pl.BlockSpec((1, 2, 8, LANES), lambda b, p: (p, 0, 0, 0)),
                  _nsq_spec(lambda p: QB_NORM + p), _nsq_spec(lambda p: KB_NORM + p)],
        out_specs=seq_spec(0),
        out_shape=jax.ShapeDtypeStruct((BATCH, N_PAIRS, SEQ, LANES), BF16),
        scratch_shapes=[pltpu.VMEM((SEQ, LANES), F32)] * 4 + [pltpu.VMEM(b.shape[1:], F32) for b in (b1, b4, b16)],
        compiler_params=_params(2),
        name="att_dilated",
    )(qkv, qkv, qkv, z, b1, b4, b16, stats, nsq, nsq)


def _rope_table_kernel(pos_ref, freq_ref, cos_ref, sin_ref):
    ang = pos_ref[0].astype(F32) * freq_ref[...]
    cos_ref[0] = jnp.cos(ang)
    sin_ref[0] = jnp.sin(ang)


def _rope_tables(positions):
    inv_freq = ROPE_THETA ** (-jnp.arange(0, C_ROPE, 2, dtype=F32) / C_ROPE)
    lane = np.arange(LANES)
    freq = jnp.where(jnp.asarray(lane >= C_NOPE), inv_freq[lane % (C_ROPE // 2)], 0.0).reshape(1, LANES)
    table = jax.ShapeDtypeStruct((BATCH, SEQ, LANES), F32)
    spec = pl.BlockSpec((1, ROW_TILE, LANES), lambda b, i: (b, i, 0))
    return pl.pallas_call(
        _rope_table_kernel,
        grid=(BATCH, SEQ // ROW_TILE),
        in_specs=[pl.BlockSpec((1, ROW_TILE, 1), lambda b, i: (b, i, 0)),
                  pl.BlockSpec((1, LANES), lambda b, i: (0, 0))],
        out_specs=[spec, spec],
        out_shape=[table, table],
        compiler_params=_params(2),
        name="rope_tables",
    )(positions.reshape(BATCH, SEQ, 1), freq)


C_LOWRANK = C_Q_RANK + C_KV_RANK + 2 * LANES


def _inproj_odd_body(x, nw_ref, w_ref, qnw_ref, wq_ref, kvnw_ref, wk_ref, wv_ref, cos_ref, sin_ref,
                     q_ref, k_ref, v_ref, g_ref, nsq_ref):
    hb = _rms(x, nw_ref[...]).astype(BF16)
    c = jnp.dot(hb, w_ref[:, :C_LOWRANK], preferred_element_type=F32)
    for j in range(C_WIDTH // (2 * LANES)):
        lo = C_LOWRANK + j * 2 * LANES
        r = jnp.dot(hb, w_ref[:, lo:lo + 2 * LANES], preferred_element_type=F32)
        for t in range(2):
            g_ref[0, 2 * j + t] = _silu(r[:, t * LANES:(t + 1) * LANES]).astype(BF16)

    qn = _rms(c[:, :C_Q_RANK], qnw_ref[...]).astype(BF16)
    kvn = _rms(c[:, C_Q_RANK:C_Q_RANK + C_KV_RANK], kvnw_ref[...]).astype(BF16)
    k_pe = c[:, C_Q_RANK + C_KV_RANK:C_Q_RANK + C_KV_RANK + LANES]
    k_pe_rot = c[:, C_Q_RANK + C_KV_RANK + LANES:]
    cos, sin = cos_ref[0], sin_ref[0]
    lane = lax.broadcasted_iota(jnp.int32, cos.shape, 1)
    q_table = jnp.where(lane < C_NOPE + C_ROPE, cos, sin) * ((C_NOPE + C_ROPE) ** -0.5 * LOG2E)
    k_rope = k_pe * cos + k_pe_rot * sin

    for j in range(C_HEADS // 2):
        cols = slice(j * 2 * LANES, (j + 1) * 2 * LANES)
        rq = jnp.dot(qn, wq_ref[:, cols], preferred_element_type=F32)
        rk = jnp.dot(kvn, wk_ref[:, cols], preferred_element_type=F32)
        for t in range(2):
            lanes = slice(t * LANES, (t + 1) * LANES)
            q_head, k_head = rq[:, lanes] * q_table, rk[:, lanes] + k_rope
            q_ref[0, 2 * j + t] = q_head.astype(BF16)
            k_ref[0, 2 * j + t] = k_head.astype(BF16)
            _store_norm_stats(nsq_ref, 2 * j + t, q_head, 1)
            _store_norm_stats(nsq_ref, C_HEADS + 2 * j + t, k_head, 1)
    for j in range(C_WIDTH // (2 * LANES)):
        rv = jnp.dot(kvn, wv_ref[:, j * 2 * LANES:(j + 1) * 2 * LANES], preferred_element_type=F32)
        for t in range(2):
            v_ref[0, 2 * j + t] = rv[:, t * LANES:(t + 1) * LANES].astype(BF16)


def _inproj_odd(x, norm_w, w, q_norm, wq, kv_norm, wk, wv, cos, sin, prev=None):
    full = lambda a: pl.BlockSpec(a.shape, lambda b, i: (0,) * a.ndim)
    row = lambda width: pl.BlockSpec((1, ROW_TILE, width), lambda b, i: (b, i, 0))
    slots = lambda n: pl.BlockSpec((1, n, ROW_TILE, LANES), lambda b, i: (b, 0, i, 0))
    out = lambda n: jax.ShapeDtypeStruct((BATCH, n, SEQ, LANES), BF16)
    n_pairs = C_WIDTH // LANES
    vecs = [norm_w.reshape(1, D_MODEL), q_norm.reshape(1, C_Q_RANK), kv_norm.reshape(1, C_KV_RANK)]
    return _projection_call(
        _inproj_odd_body, "inproj_mla", x, prev,
        ins=[vecs[0], w, vecs[1], wq, vecs[2], wk, wv, cos, sin],
        in_specs=[full(vecs[0]), full(w), full(vecs[1]), full(wq), full(vecs[2]), full(wk), full(wv),
                  row(LANES), row(LANES)],
        out_specs=[slots(C_HEADS), slots(C_HEADS), slots(n_pairs), slots(n_pairs),
                   pl.BlockSpec((1, 1, 8, 2 * C_HEADS * LANES), lambda b, i: (b, i, 0, 0))],
        out_shape=[out(C_HEADS), out(C_HEADS), out(n_pairs), out(n_pairs),
                   jax.ShapeDtypeStruct((BATCH, SEQ // ROW_TILE, 8, 2 * C_HEADS * LANES), F32)])


V_ROWS = 80


def _flash_kernel(q_ref, k_ref, v_ref, g_ref, qsq_ref, ksq_ref, o_ref, vt_ref, redo_ref,
                  m0_ref, m1_ref, l0_ref, l1_ref, acc_ref):
    t, half = FLASH_TILE, FLASH_TILE // 2
    n_tiles = SEQ // t
    step = pl.program_id(2)
    low = _low_half(t)
    m_refs, l_refs = (m0_ref, m1_ref), (l0_ref, l1_ref)

    @pl.when(step == 0)
    def _():
        row = lax.broadcasted_iota(jnp.int32, (V_ROWS - HEAD_DIM, half), 0)
        ones_row = jnp.where(row == 0, 1.0, 0.0).astype(BF16)

        def prep_body(c, carry):
            rows = pl.ds(pl.multiple_of(c * half, half), half)
            v_t = v_ref[0, 0, rows, :].astype(F32).T
            for hh in range(2):
                vt_ref[hh, :HEAD_DIM, rows] = v_t[hh * HEAD_DIM:(hh + 1) * HEAD_DIM].astype(BF16)
                vt_ref[hh, HEAD_DIM:, rows] = ones_row
            return carry

        lax.fori_loop(0, SEQ // half, prep_body, 0)

    bound = _max_norms(qsq_ref) * _max_norms(ksq_ref)
    shifts = [bound[:, hh * LANES:hh * LANES + 1] for hh in range(2)]

    def fast_tile(tile):
        qs = tile * t

        def part(hh, ks, width, q_lo, q_n, diagonal):
            s_t = _dot_nt(k_ref[0, hh, ks:ks + width, :], q_ref[0, hh, qs + q_lo:qs + q_lo + q_n, :])
            if diagonal:
                causal = (lax.broadcasted_iota(jnp.int32, (width, q_n), 0)
                          <= lax.broadcasted_iota(jnp.int32, (width, q_n), 1))
                s_t = jnp.where(causal, s_t, NEG_INF)
            p_t = jnp.exp2(s_t - shifts[hh]).astype(BF16)
            return jnp.dot(vt_ref[hh, :, ks:ks + width], p_t, preferred_element_type=F32)

        o_t, sums = [], []
        for hh in range(2):
            acc = part(hh, qs, half, 0, t, True)
            for j in range(tile):
                acc = acc + part(hh, j * t, t, 0, t, False)
            late = acc[:, half:] + part(hh, qs + half, half, half, half, True)
            acc = jnp.concatenate([acc[:, :half], late], axis=1)
            sums.append(acc[HEAD_DIM:HEAD_DIM + 1, :])
            o_t.append(acc[:HEAD_DIM] / sums[hh])
        gate = g_ref[0, 0, qs:qs + t, :].astype(F32)
        o_ref[0, 0, qs:qs + t, :] = (jnp.concatenate(o_t, axis=0).T * gate).astype(BF16)
        return jnp.minimum(jnp.min(sums[0]), jnp.min(sums[1])) >= UNDERFLOW_GUARD

    for first in range(n_tiles // 2):
        @pl.when(step == first)
        def _(first=first):
            healthy = jnp.logical_and(fast_tile(first), fast_tile(n_tiles - 1 - first))
            redo_ref[0] = jnp.where(healthy, 0, 1).astype(jnp.int32)

    def exact_tile(which, carry):
        tile = jnp.where(which == 0, step, n_tiles - 1 - step)
        q_start = pl.multiple_of(tile * t, t)
        q_rows = pl.ds(q_start, t)

        def exact_step(j, c):
            ks = pl.multiple_of(j * half, half)
            v = v_ref[0, 0, pl.ds(ks, half), :]
            visible = (lax.broadcasted_iota(jnp.int32, (t, half), 1) + (ks - q_start)
                       <= lax.broadcasted_iota(jnp.int32, (t, half), 0))
            pvs, alphas = [], []
            for hh in range(2):
                s = jnp.where(visible, _dot_nt(q_ref[0, hh, q_rows, :], k_ref[0, hh, pl.ds(ks, half), :]), NEG_INF)
                m_old = m_refs[hh][...]
                m_new = jnp.maximum(m_old, jnp.max(s, axis=-1, keepdims=True))
                alpha = jnp.exp2(m_old - m_new)
                p = jnp.exp2(s - _rep(m_new, half))
                m_refs[hh][...] = m_new
                l_refs[hh][...] = alpha * l_refs[hh][...] + jnp.sum(p, axis=-1, keepdims=True)
                pvs.append(jnp.dot(p.astype(BF16), v, preferred_element_type=F32))
                alphas.append(alpha)
            acc_ref[...] = jnp.where(low, alphas[0], alphas[1]) * acc_ref[...] + jnp.where(low, pvs[0], pvs[1])
            return c

        for hh in range(2):
            m_refs[hh][...] = jnp.full((t, LANES), NEG_INF, F32)
            l_refs[hh][...] = jnp.zeros((t, LANES), F32)
        acc_ref[...] = jnp.zeros((t, LANES), F32)
        lax.fori_loop(0, 2 * tile + 2, exact_step, 0)
        o = acc_ref[...] / jnp.where(low, l0_ref[...], l1_ref[...])
        o_ref[0, 0, q_rows, :] = (o * g_ref[0, 0, q_rows, :].astype(F32)).astype(BF16)
        return carry

    @pl.when(redo_ref[0] != 0)
    def _():
        lax.fori_loop(0, 2, exact_tile, 0)


def _flash(q, k, v, z, nsq):
    t = FLASH_TILE
    n_pairs = C_WIDTH // LANES
    heads = pl.BlockSpec((1, 2, SEQ, LANES), lambda b, p, i: (b, p, 0, 0))
    pair = pl.BlockSpec((1, 1, SEQ, LANES), lambda b, p, i: (b, p, 0, 0))
    return pl.pallas_call(
        _flash_kernel,
        grid=(BATCH, n_pairs, SEQ // t // 2),
        in_specs=[
            heads, heads, pair, pair,
            pl.BlockSpec((1, SEQ // ROW_TILE, 8, 2 * LANES), lambda b, p, i: (b, 0, 0, p)),
            pl.BlockSpec((1, SEQ // ROW_TILE, 8, 2 * LANES), lambda b, p, i: (b, 0, 0, C_HEADS // 2 + p)),
        ],
        out_specs=pair,
        out_shape=jax.ShapeDtypeStruct((BATCH, n_pairs, SEQ, LANES), BF16),
        scratch_shapes=[pltpu.VMEM((2, V_ROWS, SEQ), BF16), pltpu.SMEM((1,), jnp.int32)]
        + [pltpu.VMEM((t, LANES), F32)] * 5,
        compiler_params=_params(3),
        name="mla_flash",
    )(q, k, v, z, nsq, nsq)


def _pair_order_cols(w):
    rows = w.shape[0]
    return w.reshape(rows, A_KV_HEADS, N_PAIRS, HEAD_DIM).transpose(0, 2, 1, 3).reshape(rows, A_WIDTH)


def _even_weights(w_in, w_out):
    z_lo = A_WIDTH + 2 * A_KV_WIDTH + 3 * B_WIDTH
    w = jnp.concatenate([_pair_order_cols(w_in[:, :A_WIDTH]), w_in[:, A_WIDTH:z_lo],
                         _pair_order_cols(w_in[:, z_lo:z_lo + A_WIDTH]), w_in[:, z_lo + A_WIDTH:]],
                        axis=1).astype(BF16)
    wo_a = w_out[:A_WIDTH].reshape(A_KV_HEADS, N_PAIRS, HEAD_DIM, D_MODEL).transpose(1, 0, 2, 3)
    w_o = jnp.concatenate([wo_a.reshape(A_WIDTH, D_MODEL), w_out[A_WIDTH:]], axis=0).astype(BF16)
    return w, w_o


def _rot_half_cols(w):
    half = w.shape[1] // 2
    return jnp.concatenate([-w[:, half:], w[:, :half]], axis=1)


def _odd_weights(w_in, w_qb, w_kvb):
    splits = np.cumsum([C_Q_RANK, C_KV_RANK, C_ROPE])
    w_cq, w_ckv, w_kpe, w_z = jnp.split(w_in, splits, axis=1)
    pad = jnp.zeros((D_MODEL, C_NOPE), w_in.dtype)
    w_kpe_rot = _rot_half_cols(w_kpe)
    w = jnp.concatenate([w_cq, w_ckv, pad, w_kpe, w_kpe, pad, w_kpe_rot, w_kpe_rot, w_z], axis=1).astype(BF16)

    wq = w_qb.reshape(C_Q_RANK, C_HEADS, C_NOPE + C_ROPE)
    wq_pe = wq[:, :, C_NOPE:]
    wq_rot = jnp.concatenate([-wq_pe[:, :, C_ROPE // 2:], wq_pe[:, :, :C_ROPE // 2]], axis=2)
    wq = jnp.concatenate([wq, wq_rot], axis=2).reshape(C_Q_RANK, C_HEADS * LANES).astype(BF16)

    wkv = w_kvb.reshape(C_KV_RANK, C_HEADS, C_NOPE + C_V)
    wk = jnp.concatenate([wkv[:, :, :C_NOPE], jnp.zeros((C_KV_RANK, C_HEADS, LANES - C_NOPE), w_kvb.dtype)], axis=2)
    wk = wk.reshape(C_KV_RANK, C_HEADS * LANES).astype(BF16)
    wv = wkv[:, :, C_NOPE:].reshape(C_KV_RANK, C_WIDTH).astype(BF16)
    return w, wq, wk, wv


def kernel(x, positions, norm_w, rel_bias, ab_w_in, ab_sinks, ab_w_out, c_w_in, c_q_norm, c_w_qb, c_kv_norm,
           c_w_kvb, c_w_out, final_norm):
    bias_a = _bias_tiles(rel_bias, A_PAIR_HEADS, 1, BLOCK, 2 * BLOCK, BLOCK)
    bias_b1 = _bias_tiles(rel_bias, B_PAIR_HEADS, 1, BLOCK, 2 * BLOCK, BLOCK)
    bias_b4 = _bias_tiles(rel_bias, B_PAIR_HEADS, 4, BLOCK, 2 * BLOCK, BLOCK)
    bias_b16 = _bias_tiles(rel_bias, B_PAIR_HEADS, 16, SEQ // 16, SEQ // 16, 0)
    stats_b = _head_stats(rel_bias, B_PAIR_HEADS)
    cos, sin = _rope_tables(positions)

    prev = None
    for layer in range(DEPTH):
        i = layer // 2
        if layer % 2 == 0:
            w, w_o = _even_weights(ab_w_in[i], ab_w_out[i])
            outs = _inproj_even(x, norm_w[layer], w, prev)
            if prev is not None:
                x, outs = outs[0], outs[1:]
            qkv_a, qkv_b, gate, nsq = outs
            gs = [_att_a(qkv_a, gate, bias_a, _head_stats(rel_bias, A_PAIR_HEADS, ab_sinks[i]), nsq),
                  _att_b(qkv_b, gate, bias_b1, bias_b4, bias_b16, stats_b, nsq)]
        else:
            w, wq, wk, wv = _odd_weights(c_w_in[i], c_w_qb[i], c_w_kvb[i])
            outs = _inproj_odd(x, norm_w[layer], w, c_q_norm[i], wq, c_kv_norm[i], wk, wv, cos, sin, prev)
            if prev is not None:
                x, outs = outs[0], outs[1:]
            q, k, v, gate, nsq = outs
            gs = [_flash(q, k, v, gate, nsq)]
            w_o = c_w_out[i].astype(BF16)
        prev = (gs, w_o)
    return _outproj(prev[0], prev[1], x, final_norm)
```

```python
import functools
import math

import numpy as np
import jax
import jax.numpy as jnp
from jax import lax
from jax.experimental import pallas as pl
from jax.experimental.pallas import tpu as pltpu

D_MODEL = 1024
BATCH = 4
SEQ = 4096
DEPTH = 4
HEAD_DIM = 64
BLOCK = 128
A_HEADS = 8
A_KV_HEADS = 2
A_WINDOW = 128
B_HEADS = 8
B_CONFIGS = ((128, 1), (512, 4), (2048, 16))
NUM_BUCKETS = 32
MAX_DISTANCE = 2048
C_HEADS = 16
C_Q_RANK = 256
C_KV_RANK = 128
C_NOPE = 64
C_ROPE = 32
C_V = 64
ROPE_THETA = 10000.0
EPS = 1e-6

A_WIDTH = A_HEADS * HEAD_DIM
A_KV_WIDTH = A_KV_HEADS * HEAD_DIM
B_WIDTH = B_HEADS * HEAD_DIM
AB_WIDTH = A_WIDTH + B_WIDTH
C_WIDTH = C_HEADS * C_V

LANES = 128
F32 = jnp.float32
BF16 = jnp.bfloat16
NEG_INF = float("-inf")
LOG2E = math.log2(math.e)
UNDERFLOW_GUARD = 2.0 ** -100
NORM_FLOOR = 1e-30

ROW_TILE = 512
FLASH_TILE = 1024
VMEM_LIMIT = 48 * 1024 * 1024

N_PAIRS = 4
KA_SLOT, VA_SLOT, N_A_SLOTS = 4, 5, 6
QB_SLOT, KB_SLOT, VB_SLOT, N_B_SLOTS = 0, 4, 8, 12
BAND_UNROLL = 8
SWA_UNROLL = 2
A_PAIR_HEADS = [h for p in range(N_PAIRS) for h in (p, p + N_PAIRS)]
B_PAIR_HEADS = list(range(A_HEADS, A_HEADS + B_HEADS))


def _params(n_axes):
    return pltpu.CompilerParams(dimension_semantics=("arbitrary",) * n_axes,
                                vmem_limit_bytes=VMEM_LIMIT)


def _t5_bucket_np(dist):
    max_exact = NUM_BUCKETS // 2
    d = np.maximum(dist, 1).astype(np.float64)
    large = max_exact + (np.log(d / max_exact) / math.log(MAX_DISTANCE / max_exact)
                         * (NUM_BUCKETS - max_exact)).astype(np.int64)
    large = np.minimum(large, NUM_BUCKETS - 1)
    return np.where(dist < max_exact, dist, large).astype(np.int32)


def _bias_tiles(rel_bias, heads, dilation, rows, cols, offset):
    buckets = _t5_bucket_np(np.arange(BLOCK + 1) * dilation)
    per_head = jnp.take(rel_bias.astype(F32).T, np.asarray(heads), axis=0) * LOG2E
    dist = (lax.broadcasted_iota(jnp.int32, (rows, cols), 0) + offset
            - lax.broadcasted_iota(jnp.int32, (rows, cols), 1))
    tiles = jnp.full((len(heads), rows, cols), NEG_INF, F32)
    for bucket in np.unique(buckets):
        first = int(np.argmax(buckets == bucket))
        tiles = jnp.where((dist >= first)[None], per_head[:, int(bucket)][:, None, None], tiles)
    tiles = jnp.where(((dist >= 0) & (dist <= BLOCK))[None], tiles, NEG_INF)
    return tiles.reshape(len(heads) // 2, 2 * rows, cols)


def _head_stats(rel_bias, heads, sinks=None):
    order = np.asarray(heads)
    bias_max = jnp.take(jnp.max(rel_bias.astype(F32), axis=0), order) * LOG2E
    sink = jnp.zeros_like(bias_max) if sinks is None else jnp.take(sinks.astype(F32), order) * LOG2E
    stats = jnp.zeros((len(heads), 8), F32).at[:, 0].set(sink).at[:, 1].set(bias_max)
    return jnp.broadcast_to(stats[:, :, None], (len(heads), 8, LANES)).reshape(len(heads) // 2, 2, 8, LANES)


def _rms(x, w):
    return x * lax.rsqrt(jnp.mean(x * x, axis=-1, keepdims=True) + EPS) * w


def _rep(stat, width):
    if width == LANES:
        return stat
    return jnp.concatenate([stat] * (width // LANES), axis=1)


def _silu(z):
    return z / (1.0 + jnp.exp(-z))


def _dot_nt(a, b):
    return lax.dot_general(a, b, (((1,), (1,)), ((), ())), preferred_element_type=F32)


def _low_half(rows):
    return lax.broadcasted_iota(jnp.int32, (rows, LANES), 1) < HEAD_DIM


EVEN_NORM_SLOTS = (tuple(range(N_PAIRS)) + (KA_SLOT,)
                   + tuple(N_A_SLOTS + QB_SLOT + p for p in range(N_PAIRS))
                   + tuple(N_A_SLOTS + KB_SLOT + p for p in range(N_PAIRS)))
QA_NORM, KA_NORM, QB_NORM, KB_NORM = 0, N_PAIRS, N_PAIRS + 1, 2 * N_PAIRS + 1


def _store_norm_stats(nsq_ref, block, val, heads_per_slot):
    vb = val.astype(BF16).astype(F32)
    sq = vb * vb
    if heads_per_slot == 1:
        parts = [sq]
    else:
        low = _low_half(val.shape[0])
        parts = [jnp.where(low, sq, 0.0), jnp.where(low, 0.0, sq)]
    for hh, part in enumerate(parts):
        worst = jnp.max(jnp.sum(part, axis=-1, keepdims=True), axis=0, keepdims=True)
        lanes = slice((block * heads_per_slot + hh) * LANES, (block * heads_per_slot + hh + 1) * LANES)
        nsq_ref[0, 0, :, lanes] = jnp.broadcast_to(worst, (8, LANES))


def _max_norms(nsq_ref):
    worst = nsq_ref[0, 0]
    for tile in range(1, nsq_ref.shape[1]):
        worst = jnp.maximum(worst, nsq_ref[0, tile])
    return jnp.sqrt(worst[:1, :])


def _gated_rows(g_refs):
    return jnp.concatenate([g_ref[0, p] for g_ref in g_refs for p in range(g_ref.shape[1])], axis=1)


def _projection_kernel(*refs, n_g, n_in, body):
    if n_g:
        g_refs, w_out_ref, x_ref, rest = refs[:n_g], refs[n_g], refs[n_g + 1], refs[n_g + 2:]
        ins, x_out_ref, outs = rest[:n_in], rest[n_in], rest[n_in + 1:]
        x = x_ref[0] + jnp.dot(_gated_rows(g_refs), w_out_ref[...], preferred_element_type=F32)
        x_out_ref[0] = x
    else:
        x, ins, outs = refs[0][0], refs[1:1 + n_in], refs[1 + n_in:]
    body(x, *ins, *outs)


def _projection_call(body, name, x, prev, ins, in_specs, out_specs, out_shape):
    x_spec = pl.BlockSpec((1, ROW_TILE, D_MODEL), lambda b, i: (b, i, 0))
    lead, lead_specs = [x], [x_spec]
    if prev is not None:
        gs, w_out = prev
        lead = [*gs, w_out, x]
        lead_specs = [pl.BlockSpec((1, g.shape[1], ROW_TILE, LANES), lambda b, i: (b, 0, i, 0)) for g in gs] + [
            pl.BlockSpec(w_out.shape, lambda b, i: (0, 0)), x_spec]
        out_specs = [x_spec] + out_specs
        out_shape = [jax.ShapeDtypeStruct(x.shape, F32)] + out_shape
    return pl.pallas_call(
        functools.partial(_projection_kernel, n_g=0 if prev is None else len(prev[0]), n_in=len(ins), body=body),
        grid=(BATCH, SEQ // ROW_TILE),
        in_specs=lead_specs + in_specs,
        out_specs=out_specs,
        out_shape=out_shape,
        compiler_params=_params(2),
        name=name if prev is None else "residual_" + name,
    )(*lead, *ins)


def _inproj_even_body(x, nw_ref, w_ref, a_ref, b_ref, g_ref, nsq_ref):
    hb = _rms(x, nw_ref[...]).astype(BF16)
    q_scale = HEAD_DIM ** -0.5 * LOG2E
    n_chunks = w_ref.shape[1] // (2 * LANES)
    for c in range(n_chunks):
        r = jnp.dot(hb, w_ref[:, c * 2 * LANES:(c + 1) * 2 * LANES], preferred_element_type=F32)
        for t in range(2):
            slot = 2 * c + t
            val = r[:, t * LANES:(t + 1) * LANES]
            if slot < N_A_SLOTS:
                val = val * q_scale if slot < N_PAIRS else val
                a_ref[0, slot] = val.astype(BF16)
            elif slot < N_A_SLOTS + N_B_SLOTS:
                val = val * q_scale if slot - N_A_SLOTS < N_PAIRS else val
                b_ref[0, slot - N_A_SLOTS] = val
            else:
                g_ref[0, slot - N_A_SLOTS - N_B_SLOTS] = _silu(val).astype(BF16)
            if slot in EVEN_NORM_SLOTS:
                _store_norm_stats(nsq_ref, EVEN_NORM_SLOTS.index(slot), val, 2)


def _inproj_even(x, norm_w, w, prev=None):
    n_z = AB_WIDTH // LANES
    slots = lambda n: pl.BlockSpec((1, n, ROW_TILE, LANES), lambda b, i: (b, 0, i, 0))
    out = lambda n, dtype: jax.ShapeDtypeStruct((BATCH, n, SEQ, LANES), dtype)
    return _projection_call(
        _inproj_even_body, "inproj_even", x, prev,
        ins=[norm_w.reshape(1, D_MODEL), w],
        in_specs=[pl.BlockSpec((1, D_MODEL), lambda b, i: (0, 0)), pl.BlockSpec(w.shape, lambda b, i: (0, 0))],
        out_specs=[slots(N_A_SLOTS), slots(N_B_SLOTS), slots(n_z),
                   pl.BlockSpec((1, 1, 8, 2 * len(EVEN_NORM_SLOTS) * LANES), lambda b, i: (b, i, 0, 0))],
        out_shape=[out(N_A_SLOTS, BF16), out(N_B_SLOTS, F32), out(n_z, BF16),
                   jax.ShapeDtypeStruct((BATCH, SEQ // ROW_TILE, 8, 2 * len(EVEN_NORM_SLOTS) * LANES), F32)])


def _outproj_kernel(*refs, n_g, final):
    g_refs, (w_ref, x_ref), rest = refs[:n_g], refs[n_g:n_g + 2], refs[n_g + 2:]
    y = x_ref[0] + jnp.dot(_gated_rows(g_refs), w_ref[...], preferred_element_type=F32)
    if final:
        fw_ref, o_ref = rest
        y = _rms(y, fw_ref[...])
    else:
        (o_ref,) = rest
    o_ref[0] = y


def _outproj(gs, w, x, final_w=None):
    final = final_w is not None
    in_specs = [pl.BlockSpec((1, g.shape[1], ROW_TILE, LANES), lambda b, i: (b, 0, i, 0)) for g in gs] + [
        pl.BlockSpec(w.shape, lambda b, i: (0, 0)),
        pl.BlockSpec((1, ROW_TILE, D_MODEL), lambda b, i: (b, i, 0)),
    ]
    args = [*gs, w, x]
    if final:
        in_specs.append(pl.BlockSpec((1, D_MODEL), lambda b, i: (0, 0)))
        args.append(final_w.reshape(1, D_MODEL))
    return pl.pallas_call(
        functools.partial(_outproj_kernel, n_g=len(gs), final=final),
        grid=(BATCH, SEQ // ROW_TILE),
        in_specs=in_specs,
        out_specs=pl.BlockSpec((1, ROW_TILE, D_MODEL), lambda b, i: (b, i, 0)),
        out_shape=jax.ShapeDtypeStruct((BATCH, SEQ, D_MODEL), F32),
        compiler_params=_params(2),
        name="outproj_final" if final else "outproj",
    )(*args)


def _stack_heads(q):
    lane = lax.broadcasted_iota(jnp.int32, (1, LANES), 1)
    keep0 = jnp.where(lane < HEAD_DIM, 1.0, 0.0).astype(q.dtype)
    return jnp.concatenate([q * keep0, q * (1.0 - keep0).astype(q.dtype)], axis=0)


def _pair_shift(qsq_ref, ksq_ref, stat_ref):
    bias_max = jnp.concatenate([stat_ref[0, hh, 1:2, :] for hh in range(2)], axis=1)
    return _max_norms(qsq_ref) * _max_norms(ksq_ref) + bias_max


def _stacked_rows(pair_row, rows):
    return jnp.concatenate([jnp.broadcast_to(pair_row[:, hh * LANES:(hh + 1) * LANES], (rows, LANES))
                            for hh in range(2)], axis=0)


def _fast_unit(q, k, v, shifted_bias, low):
    r = q.shape[0]
    p = jnp.exp2(_dot_nt(_stack_heads(q), k) + shifted_bias)
    sums = jnp.sum(p, axis=-1, keepdims=True)
    pv = jnp.dot(p.astype(BF16), v, preferred_element_type=F32)
    return jnp.where(low, pv[:r], pv[r:]), sums[:r], sums[r:]


def _sweep_blocks(n_blocks, block_fn, carry, unroll=BAND_UNROLL):
    carry = block_fn(0, 0, BLOCK, carry)

    def body(i, c):
        return block_fn(pl.multiple_of(i * BLOCK, BLOCK), pl.multiple_of((i - 1) * BLOCK, BLOCK), 2 * BLOCK, c)

    return lax.fori_loop(1, n_blocks, body, carry, unroll=unroll)


def _att_a_kernel(q_ref, k_ref, v_ref, g_ref, bias_ref, stat_ref, qsq_ref, ksq_ref, o_ref, shifted_bias_ref):
    low = _low_half(BLOCK)
    groups = 2 * N_PAIRS

    def stat_row(r):
        return jnp.concatenate([stat_ref[p, hh, r:r + 1, :] for p in range(N_PAIRS) for hh in range(2)], axis=1)

    def group_rows(row):
        return jnp.concatenate([jnp.broadcast_to(row[:, g * LANES:(g + 1) * LANES], (BLOCK, LANES))
                                for g in range(groups)], axis=0)

    sink = group_rows(stat_row(0))
    bound = _max_norms(qsq_ref) * jnp.concatenate([_max_norms(ksq_ref)] * N_PAIRS, axis=1) + stat_row(1)
    shift = jnp.maximum(group_rows(bound), sink)
    shifted_bias_ref[...] = bias_ref[...] - _rep(shift, 2 * BLOCK)
    sink_term = jnp.exp2(sink - shift)

    def operands(qs, ks, n_keys, bias_of):
        rows, keys = pl.ds(qs, BLOCK), pl.ds(ks, n_keys)
        q = jnp.concatenate([_stack_heads(q_ref[0, p, rows, :]) for p in range(N_PAIRS)], axis=0)
        return rows, q, k_ref[0, 0, keys, :], v_ref[0, 0, keys, :], bias_of[:, 2 * BLOCK - n_keys:]

    def emit(rows, pv, denom):
        for p in range(N_PAIRS):
            h0, h1 = slice(2 * p * BLOCK, (2 * p + 1) * BLOCK), slice((2 * p + 1) * BLOCK, (2 * p + 2) * BLOCK)
            o = jnp.where(low, pv[h0], pv[h1]) / jnp.where(low, denom[h0], denom[h1])
            o_ref[0, p, rows, :] = (o * g_ref[0, p, rows, :].astype(F32)).astype(BF16)

    def fast_block(qs, ks, n_keys, l_min):
        rows, q, k, v, shifted_bias = operands(qs, ks, n_keys, shifted_bias_ref)
        p = jnp.exp2(_dot_nt(q, k) + shifted_bias)
        denom = jnp.sum(p, axis=-1, keepdims=True) + sink_term
        emit(rows, jnp.dot(p.astype(BF16), v, preferred_element_type=F32), denom)
        for g in range(groups):
            l_min = jnp.minimum(l_min, denom[g * BLOCK:(g + 1) * BLOCK])
        return l_min

    def exact_block(qs, ks, n_keys, carry):
        rows, q, k, v, bias = operands(qs, ks, n_keys, bias_ref)
        s = _dot_nt(q, k) + bias
        m = jnp.maximum(jnp.max(s, axis=-1, keepdims=True), sink)
        p = jnp.exp2(s - _rep(m, n_keys))
        denom = jnp.sum(p, axis=-1, keepdims=True) + jnp.exp2(sink - m)
        emit(rows, jnp.dot(p.astype(BF16), v, preferred_element_type=F32), denom)
        return carry

    l_min = _sweep_blocks(SEQ // BLOCK, fast_block, jnp.full((BLOCK, LANES), jnp.inf, F32), SWA_UNROLL)

    @pl.when(jnp.logical_not(jnp.min(l_min) >= UNDERFLOW_GUARD))
    def _():
        _sweep_blocks(SEQ // BLOCK, exact_block, 0, SWA_UNROLL)


def _nsq_spec(block_of):
    return pl.BlockSpec((1, SEQ // ROW_TILE, 8, 2 * LANES), lambda b, p: (b, 0, 0, block_of(p)))


def _att_a(qkv, z, bias, stats, nsq):
    seq_spec = lambda n, slot: pl.BlockSpec((1, n, SEQ, LANES), lambda b: (b, slot, 0, 0))
    tiles = SEQ // ROW_TILE
    return pl.pallas_call(
        _att_a_kernel,
        grid=(BATCH,),
        in_specs=[
            seq_spec(N_PAIRS, 0), seq_spec(1, KA_SLOT), seq_spec(1, VA_SLOT), seq_spec(N_PAIRS, 0),
            pl.BlockSpec((N_PAIRS * 2 * BLOCK, 2 * BLOCK), lambda b: (0, 0)),
            pl.BlockSpec(stats.shape, lambda b: (0, 0, 0, 0)),
            pl.BlockSpec((1, tiles, 8, N_PAIRS * 2 * LANES), lambda b: (b, 0, 0, QA_NORM)),
            pl.BlockSpec((1, tiles, 8, 2 * LANES), lambda b: (b, 0, 0, KA_NORM)),
        ],
        out_specs=seq_spec(N_PAIRS, 0),
        out_shape=jax.ShapeDtypeStruct((BATCH, N_PAIRS, SEQ, LANES), BF16),
        scratch_shapes=[pltpu.VMEM((N_PAIRS * 2 * BLOCK, 2 * BLOCK), F32)],
        compiler_params=_params(1),
        name="att_swa",
    )(qkv, qkv, qkv, z, bias.reshape(N_PAIRS * 2 * BLOCK, 2 * BLOCK), stats, nsq, nsq)


def _att_b_kernel(q_ref, k_ref, v_ref, g_ref, b1_ref, b4_ref, b16_ref, stat_ref, qsq_ref, ksq_ref, o_ref,
                  m0_ref, m1_ref, l_ref, acc_ref, sb1_ref, sb4_ref, sb16_ref):
    low = _low_half(BLOCK)
    load = lambda ref, rows: ref[0, 0, rows, :].astype(BF16)
    shift = _pair_shift(qsq_ref, ksq_ref, stat_ref)
    for bias_ref, shifted_ref in ((b1_ref, sb1_ref), (b4_ref, sb4_ref), (b16_ref, sb16_ref)):
        shifted_ref[...] = bias_ref[0] - _rep(_stacked_rows(shift, bias_ref.shape[1] // 2), bias_ref.shape[2])

    def sweep(unit, bias1, bias4, bias16):
        def residue16(r, carry):
            rows = pl.ds(r, SEQ // 16, stride=16)
            unit(rows, load(q_ref, rows), load(k_ref, rows), load(v_ref, rows), bias16[...], True)
            return carry

        lax.fori_loop(0, 16, residue16, 0, unroll=2)

        def residue4(f, carry):
            def block4(us, ks, n_keys, c):
                rows, keys = pl.ds(us * 4 + f, BLOCK, stride=4), pl.ds(ks * 4 + f, n_keys, stride=4)
                unit(rows, load(q_ref, rows), load(k_ref, keys), load(v_ref, keys),
                     bias4[:, 2 * BLOCK - n_keys:], False)
                return c

            return _sweep_blocks(SEQ // 4 // BLOCK, block4, carry)

        lax.fori_loop(0, 4, residue4, 0)

        def block1(qs, ks, n_keys, carry):
            rows, keys = pl.ds(qs, BLOCK), pl.ds(ks, n_keys)
            unit(rows, load(q_ref, rows), load(k_ref, keys), load(v_ref, keys),
                 bias1[:, 2 * BLOCK - n_keys:], False)
            return carry

        _sweep_blocks(SEQ // BLOCK, block1, 0)

    def fast_unit(rows, q, k, v, shifted_bias, first):
        low_r = low if q.shape[0] == BLOCK else _low_half(q.shape[0])
        pv, sum0, sum1 = _fast_unit(q, k, v, shifted_bias, low_r)
        sums = jnp.where(low_r, sum0, sum1)
        if first:
            acc_ref[rows, :] = pv
            l_ref[rows, :] = sums
        else:
            acc_ref[rows, :] += pv
            l_ref[rows, :] += sums

    def exact_unit(rows, q, k, v, bias, first):
        r, n_keys = q.shape[0], k.shape[0]
        low_r = low if r == BLOCK else _low_half(r)
        s = _dot_nt(_stack_heads(q), k) + bias
        m_new = jnp.broadcast_to(jnp.max(s, axis=-1, keepdims=True), (2 * r, LANES))
        if not first:
            m_old = jnp.concatenate([m0_ref[rows, :], m1_ref[rows, :]], axis=0)
            m_new = jnp.maximum(m_old, m_new)
            alpha = jnp.exp2(m_old - m_new)
            alpha = jnp.where(low_r, alpha[:r], alpha[r:])
        p = jnp.exp2(s - _rep(m_new, n_keys))
        sums = jnp.sum(p, axis=-1, keepdims=True)
        sums = jnp.where(low_r, sums[:r], sums[r:])
        pv = jnp.dot(p.astype(BF16), v, preferred_element_type=F32)
        pv = jnp.where(low_r, pv[:r], pv[r:])
        m0_ref[rows, :] = m_new[:r]
        m1_ref[rows, :] = m_new[r:]
        if first:
            acc_ref[rows, :] = pv
            l_ref[rows, :] = sums
        else:
            acc_ref[rows, :] = alpha * acc_ref[rows, :] + pv
            l_ref[rows, :] = alpha * l_ref[rows, :] + sums

    def finish(i, l_min):
        rows = pl.ds(pl.multiple_of(i * BLOCK, BLOCK), BLOCK)
        l = l_ref[rows, :]
        gate = g_ref[0, 0, rows, :].astype(F32)
        o_ref[0, 0, rows, :] = (acc_ref[rows, :] / l * gate).astype(BF16)
        return jnp.minimum(l_min, l)

    no_min = jnp.full((BLOCK, LANES), jnp.inf, F32)
    sweep(fast_unit, sb1_ref, sb4_ref, sb16_ref)
    l_min = lax.fori_loop(0, SEQ // BLOCK, finish, no_min, unroll=BAND_UNROLL)

    @pl.when(jnp.logical_not(jnp.min(l_min) >= UNDERFLOW_GUARD))
    def _():
        sweep(exact_unit, b1_ref.at[0], b4_ref.at[0], b16_ref.at[0])
        lax.fori_loop(0, SEQ // BLOCK, finish, no_min)


def _att_b(qkv, z, b1, b4, b16, stats, nsq):
    seq_spec = lambda slot: pl.BlockSpec((1, 1, SEQ, LANES), lambda b, p: (b, slot + p, 0, 0))
    bias_spec = lambda a: pl.BlockSpec((1,) + a.shape[1:], lambda b, p: (p, 0, 0))
    return pl.pallas_call(
        _att_b_kernel,
        grid=(BATCH, N_PAIRS),
        in_specs=[seq_spec(QB_SLOT), seq_spec(KB_SLOT), seq_spec(VB_SLOT), seq_spec(N_PAIRS),
                  bias_spec(b1), bias_spec(b4), bias_spec(b16),
                  pl.BlockSpec((1, 2, 8, LANES), lambda b, p: (p, 0, 0, 0)),
                  _nsq_spec(lambda p: QB_NORM + p), _nsq_spec(lambda p: KB_NORM + p)],
        out_specs=seq_spec(0),
        out_shape=jax.ShapeDtypeStruct((BATCH, N_PAIRS, SEQ, LANES), BF16),
        scratch_shapes=[pltpu.VMEM((SEQ, LANES), F32)] * 4 + [pltpu.VMEM(b.shape[1:], F32) for b in (b1, b4, b16)],
        compiler_params=_params(2),
        name="att_dilated",
    )(qkv, qkv, qkv, z, b1, b4, b16, stats, nsq, nsq)


def _rope_table_kernel(pos_ref, freq_ref, cos_ref, sin_ref):
    ang = pos_ref[0].astype(F32) * freq_ref[...]
    cos_ref[0] = jnp.cos(ang)
    sin_ref[0] = jnp.sin(ang)


def _rope_tables(positions):
    inv_freq = ROPE_THETA ** (-jnp.arange(0, C_ROPE, 2, dtype=F32) / C_ROPE)
    lane = np.arange(LANES)
    freq = jnp.where(jnp.asarray(lane >= C_NOPE), inv_freq[lane % (C_ROPE // 2)], 0.0).reshape(1, LANES)
    table = jax.ShapeDtypeStruct((BATCH, SEQ, LANES), F32)
    spec = pl.BlockSpec((1, ROW_TILE, LANES), lambda b, i: (b, i, 0))
    return pl.pallas_call(
        _rope_table_kernel,
        grid=(BATCH, SEQ // ROW_TILE),
        in_specs=[pl.BlockSpec((1, ROW_TILE, 1), lambda b, i: (b, i, 0)),
                  pl.BlockSpec((1, LANES), lambda b, i: (0, 0))],
        out_specs=[spec, spec],
        out_shape=[table, table],
        compiler_params=_params(2),
        name="rope_tables",
    )(positions.reshape(BATCH, SEQ, 1), freq)


C_LOWRANK = C_Q_RANK + C_KV_RANK + 2 * LANES


def _inproj_odd_body(x, nw_ref, w_ref, qnw_ref, wq_ref, kvnw_ref, wk_ref, wv_ref, cos_ref, sin_ref,
                     q_ref, k_ref, v_ref, g_ref, nsq_ref):
    hb = _rms(x, nw_ref[...]).astype(BF16)
    c = jnp.dot(hb, w_ref[:, :C_LOWRANK], preferred_element_type=F32)
    for j in range(C_WIDTH // (2 * LANES)):
        lo = C_LOWRANK + j * 2 * LANES
        r = jnp.dot(hb, w_ref[:, lo:lo + 2 * LANES], preferred_element_type=F32)
        for t in range(2):
            g_ref[0, 2 * j + t] = _silu(r[:, t * LANES:(t + 1) * LANES]).astype(BF16)

    qn = _rms(c[:, :C_Q_RANK], qnw_ref[...]).astype(BF16)
    kvn = _rms(c[:, C_Q_RANK:C_Q_RANK + C_KV_RANK], kvnw_ref[...]).astype(BF16)
    k_pe = c[:, C_Q_RANK + C_KV_RANK:C_Q_RANK + C_KV_RANK + LANES]
    k_pe_rot = c[:, C_Q_RANK + C_KV_RANK + LANES:]
    cos, sin = cos_ref[0], sin_ref[0]
    lane = lax.broadcasted_iota(jnp.int32, cos.shape, 1)
    q_table = jnp.where(lane < C_NOPE + C_ROPE, cos, sin) * ((C_NOPE + C_ROPE) ** -0.5 * LOG2E)
    k_rope = k_pe * cos + k_pe_rot * sin

    for j in range(C_HEADS // 2):
        cols = slice(j * 2 * LANES, (j + 1) * 2 * LANES)
        rq = jnp.dot(qn, wq_ref[:, cols], preferred_element_type=F32)
        rk = jnp.dot(kvn, wk_ref[:, cols], preferred_element_type=F32)
        for t in range(2):
            lanes = slice(t * LANES, (t + 1) * LANES)
            q_head, k_head = rq[:, lanes] * q_table, rk[:, lanes] + k_rope
            q_ref[0, 2 * j + t] = q_head.astype(BF16)
            k_ref[0, 2 * j + t] = k_head.astype(BF16)
            _store_norm_stats(nsq_ref, 2 * j + t, q_head, 1)
            _store_norm_stats(nsq_ref, C_HEADS + 2 * j + t, k_head, 1)
    for j in range(C_WIDTH // (2 * LANES)):
        rv = jnp.dot(kvn, wv_ref[:, j * 2 * LANES:(j + 1) * 2 * LANES], preferred_element_type=F32)
        for t in range(2):
            v_ref[0, 2 * j + t] = rv[:, t * LANES:(t + 1) * LANES].astype(BF16)


def _inproj_odd(x, norm_w, w, q_norm, wq, kv_norm, wk, wv, cos, sin, prev=None):
    full = lambda a: pl.BlockSpec(a.shape, lambda b, i: (0,) * a.ndim)
    row = lambda width: pl.BlockSpec((1, ROW_TILE, width), lambda b, i: (b, i, 0))
    slots = lambda n: pl.BlockSpec((1, n, ROW_TILE, LANES), lambda b, i: (b, 0, i, 0))
    out = lambda n: jax.ShapeDtypeStruct((BATCH, n, SEQ, LANES), BF16)
    n_pairs = C_WIDTH // LANES
    vecs = [norm_w.reshape(1, D_MODEL), q_norm.reshape(1, C_Q_RANK), kv_norm.reshape(1, C_KV_RANK)]
    return _projection_call(
        _inproj_odd_body, "inproj_mla", x, prev,
        ins=[vecs[0], w, vecs[1], wq, vecs[2], wk, wv, cos, sin],
        in_specs=[full(vecs[0]), full(w), full(vecs[1]), full(wq), full(vecs[2]), full(wk), full(wv),
                  row(LANES), row(LANES)],
        out_specs=[slots(C_HEADS), slots(C_HEADS), slots(n_pairs), slots(n_pairs),
                   pl.BlockSpec((1, 1, 8, 2 * C_HEADS * LANES), lambda b, i: (b, i, 0, 0))],
        out_shape=[out(C_HEADS), out(C_HEADS), out(n_pairs), out(n_pairs),
                   jax.ShapeDtypeStruct((BATCH, SEQ // ROW_TILE, 8, 2 * C_HEADS * LANES), F32)])


V_ROWS = 80


def _flash_kernel(q_ref, k_ref, v_ref, g_ref, qsq_ref, ksq_ref, o_ref, vt_ref, redo_ref,
                  m0_ref, m1_ref, l0_ref, l1_ref, acc_ref):
    t, half = FLASH_TILE, FLASH_TILE // 2
    n_tiles = SEQ // t
    step = pl.program_id(2)
    low = _low_half(t)
    m_refs, l_refs = (m0_ref, m1_ref), (l0_ref, l1_ref)

    @pl.when(step == 0)
    def _():
        row = lax.broadcasted_iota(jnp.int32, (V_ROWS - HEAD_DIM, half), 0)
        ones_row = jnp.where(row == 0, 1.0, 0.0).astype(BF16)

        def prep_body(c, carry):
            rows = pl.ds(pl.multiple_of(c * half, half), half)
            v_t = v_ref[0, 0, rows, :].astype(F32).T
            for hh in range(2):
                vt_ref[hh, :HEAD_DIM, rows] = v_t[hh * HEAD_DIM:(hh + 1) * HEAD_DIM].astype(BF16)
                vt_ref[hh, HEAD_DIM:, rows] = ones_row
            return carry

        lax.fori_loop(0, SEQ // half, prep_body, 0)

    bound = _max_norms(qsq_ref) * _max_norms(ksq_ref)
    shifts = [bound[:, hh * LANES:hh * LANES + 1] for hh in range(2)]

    def fast_tile(tile):
        qs = tile * t

        def part(hh, ks, width, q_lo, q_n, diagonal):
            s_t = _dot_nt(k_ref[0, hh, ks:ks + width, :], q_ref[0, hh, qs + q_lo:qs + q_lo + q_n, :])
            if diagonal:
                causal = (lax.broadcasted_iota(jnp.int32, (width, q_n), 0)
                          <= lax.broadcasted_iota(jnp.int32, (width, q_n), 1))
                s_t = jnp.where(causal, s_t, NEG_INF)
            p_t = jnp.exp2(s_t - shifts[hh]).astype(BF16)
            return jnp.dot(vt_ref[hh, :, ks:ks + width], p_t, preferred_element_type=F32)

        o_t, sums = [], []
        for hh in range(2):
            acc = part(hh, qs, half, 0, t, True)
            for j in range(tile):
                acc = acc + part(hh, j * t, t, 0, t, False)
            late = acc[:, half:] + part(hh, qs + half, half, half, half, True)
            acc = jnp.concatenate([acc[:, :half], late], axis=1)
            sums.append(acc[HEAD_DIM:HEAD_DIM + 1, :])
            o_t.append(acc[:HEAD_DIM] / sums[hh])
        gate = g_ref[0, 0, qs:qs + t, :].astype(F32)
        o_ref[0, 0, qs:qs + t, :] = (jnp.concatenate(o_t, axis=0).T * gate).astype(BF16)
        return jnp.minimum(jnp.min(sums[0]), jnp.min(sums[1])) >= UNDERFLOW_GUARD

    for first in range(n_tiles // 2):
        @pl.when(step == first)
        def _(first=first):
            healthy = jnp.logical_and(fast_tile(first), fast_tile(n_tiles - 1 - first))
            redo_ref[0] = jnp.where(healthy, 0, 1).astype(jnp.int32)

    def exact_tile(which, carry):
        tile = jnp.where(which == 0, step, n_tiles - 1 - step)
        q_start = pl.multiple_of(tile * t, t)
        q_rows = pl.ds(q_start, t)

        def exact_step(j, c):
            ks = pl.multiple_of(j * half, half)
            v = v_ref[0, 0, pl.ds(ks, half), :]
            visible = (lax.broadcasted_iota(jnp.int32, (t, half), 1) + (ks - q_start)
                       <= lax.broadcasted_iota(jnp.int32, (t, half), 0))
            pvs, alphas = [], []
            for hh in range(2):
                s = jnp.where(visible, _dot_nt(q_ref[0, hh, q_rows, :], k_ref[0, hh, pl.ds(ks, half), :]), NEG_INF)
                m_old = m_refs[hh][...]
                m_new = jnp.maximum(m_old, jnp.max(s, axis=-1, keepdims=True))
                alpha = jnp.exp2(m_old - m_new)
                p = jnp.exp2(s - _rep(m_new, half))
                m_refs[hh][...] = m_new
                l_refs[hh][...] = alpha * l_refs[hh][...] + jnp.sum(p, axis=-1, keepdims=True)
                pvs.append(jnp.dot(p.astype(BF16), v, preferred_element_type=F32))
                alphas.append(alpha)
            acc_ref[...] = jnp.where(low, alphas[0], alphas[1]) * acc_ref[...] + jnp.where(low, pvs[0], pvs[1])
            return c

        for hh in range(2):
            m_refs[hh][...] = jnp.full((t, LANES), NEG_INF, F32)
            l_refs[hh][...] = jnp.zeros((t, LANES), F32)
        acc_ref[...] = jnp.zeros((t, LANES), F32)
        lax.fori_loop(0, 2 * tile + 2, exact_step, 0)
        o = acc_ref[...] / jnp.where(low, l0_ref[...], l1_ref[...])
        o_ref[0, 0, q_rows, :] = (o * g_ref[0, 0, q_rows, :].astype(F32)).astype(BF16)
        return carry

    @pl.when(redo_ref[0] != 0)
    def _():
        lax.fori_loop(0, 2, exact_tile, 0)


def _flash(q, k, v, z, nsq):
    t = FLASH_TILE
    n_pairs = C_WIDTH // LANES
    heads = pl.BlockSpec((1, 2, SEQ, LANES), lambda b, p, i: (b, p, 0, 0))
    pair = pl.BlockSpec((1, 1, SEQ, LANES), lambda b, p, i: (b, p, 0, 0))
    return pl.pallas_call(
        _flash_kernel,
        grid=(BATCH, n_pairs, SEQ // t // 2),
        in_specs=[
            heads, heads, pair, pair,
            pl.BlockSpec((1, SEQ // ROW_TILE, 8, 2 * LANES), lambda b, p, i: (b, 0, 0, p)),
            pl.BlockSpec((1, SEQ // ROW_TILE, 8, 2 * LANES), lambda b, p, i: (b, 0, 0, C_HEADS // 2 + p)),
        ],
        out_specs=pair,
        out_shape=jax.ShapeDtypeStruct((BATCH, n_pairs, SEQ, LANES), BF16),
        scratch_shapes=[pltpu.VMEM((2, V_ROWS, SEQ), BF16), pltpu.SMEM((1,), jnp.int32)]
        + [pltpu.VMEM((t, LANES), F32)] * 5,
        compiler_params=_params(3),
        name="mla_flash",
    )(q, k, v, z, nsq, nsq)


def _pair_order_cols(w):
    rows = w.shape[0]
    return w.reshape(rows, A_KV_HEADS, N_PAIRS, HEAD_DIM).transpose(0, 2, 1, 3).reshape(rows, A_WIDTH)


def _even_weights(w_in, w_out):
    z_lo = A_WIDTH + 2 * A_KV_WIDTH + 3 * B_WIDTH
    w = jnp.concatenate([_pair_order_cols(w_in[:, :A_WIDTH]), w_in[:, A_WIDTH:z_lo],
                         _pair_order_cols(w_in[:, z_lo:z_lo + A_WIDTH]), w_in[:, z_lo + A_WIDTH:]],
                        axis=1).astype(BF16)
    wo_a = w_out[:A_WIDTH].reshape(A_KV_HEADS, N_PAIRS, HEAD_DIM, D_MODEL).transpose(1, 0, 2, 3)
    w_o = jnp.concatenate([wo_a.reshape(A_WIDTH, D_MODEL), w_out[A_WIDTH:]], axis=0).astype(BF16)
    return w, w_o


def _rot_half_cols(w):
    half = w.shape[1] // 2
    return jnp.concatenate([-w[:, half:], w[:, :half]], axis=1)


def _odd_weights(w_in, w_qb, w_kvb):
    splits = np.cumsum([C_Q_RANK, C_KV_RANK, C_ROPE])
    w_cq, w_ckv, w_kpe, w_z = jnp.split(w_in, splits, axis=1)
    pad = jnp.zeros((D_MODEL, C_NOPE), w_in.dtype)
    w_kpe_rot = _rot_half_cols(w_kpe)
    w = jnp.concatenate([w_cq, w_ckv, pad, w_kpe, w_kpe, pad, w_kpe_rot, w_kpe_rot, w_z], axis=1).astype(BF16)

    wq = w_qb.reshape(C_Q_RANK, C_HEADS, C_NOPE + C_ROPE)
    wq_pe = wq[:, :, C_NOPE:]
    wq_rot = jnp.concatenate([-wq_pe[:, :, C_ROPE // 2:], wq_pe[:, :, :C_ROPE // 2]], axis=2)
    wq = jnp.concatenate([wq, wq_rot], axis=2).reshape(C_Q_RANK, C_HEADS * LANES).astype(BF16)

    wkv = w_kvb.reshape(C_KV_RANK, C_HEADS, C_NOPE + C_V)
    wk = jnp.concatenate([wkv[:, :, :C_NOPE], jnp.zeros((C_KV_RANK, C_HEADS, LANES - C_NOPE), w_kvb.dtype)], axis=2)
    wk = wk.reshape(C_KV_RANK, C_HEADS * LANES).astype(BF16)
    wv = wkv[:, :, C_NOPE:].reshape(C_KV_RANK, C_WIDTH).astype(BF16)
    return w, wq, wk, wv


def kernel(x, positions, norm_w, rel_bias, ab_w_in, ab_sinks, ab_w_out, c_w_in, c_q_norm, c_w_qb, c_kv_norm,
           c_w_kvb, c_w_out, final_norm):
    bias_a = _bias_tiles(rel_bias, A_PAIR_HEADS, 1, BLOCK, 2 * BLOCK, BLOCK)
    bias_b1 = _bias_tiles(rel_bias, B_PAIR_HEADS, 1, BLOCK, 2 * BLOCK, BLOCK)
    bias_b4 = _bias_tiles(rel_bias, B_PAIR_HEADS, 4, BLOCK, 2 * BLOCK, BLOCK)
    bias_b16 = _bias_tiles(rel_bias, B_PAIR_HEADS, 16, SEQ // 16, SEQ // 16, 0)
    stats_b = _head_stats(rel_bias, B_PAIR_HEADS)
    cos, sin = _rope_tables(positions)

    prev = None
    for layer in range(DEPTH):
        i = layer // 2
        if layer % 2 == 0:
            w, w_o = _even_weights(ab_w_in[i], ab_w_out[i])
            outs = _inproj_even(x, norm_w[layer], w, prev)
            if prev is not None:
                x, outs = outs[0], outs[1:]
            qkv_a, qkv_b, gate, nsq = outs
            gs = [_att_a(qkv_a, gate, bias_a, _head_stats(rel_bias, A_PAIR_HEADS, ab_sinks[i]), nsq),
                  _att_b(qkv_b, gate, bias_b1, bias_b4, bias_b16, stats_b, nsq)]
        else:
            w, wq, wk, wv = _odd_weights(c_w_in[i], c_w_qb[i], c_w_kvb[i])
            outs = _inproj_odd(x, norm_w[layer], w, c_q_norm[i], wq, c_kv_norm[i], wk, wv, cos, sin, prev)
            if prev is not None:
                x, outs = outs[0], outs[1:]
            q, k, v, gate, nsq = outs
            gs = [_flash(q, k, v, gate, nsq)]
            w_o = c_w_out[i].astype(BF16)
        prev = (gs, w_o)
    return _outproj(prev[0], prev[1], x, final_norm)
```

```python
import functools
import math

import numpy as np
import jax
import jax.numpy as jnp
from jax import lax
from jax.experimental import pallas as pl
from jax.experimental.pallas import tpu as pltpu

D_MODEL = 1024
BATCH = 4
SEQ = 4096
DEPTH = 4
HEAD_DIM = 64
BLOCK = 128
A_HEADS = 8
A_KV_HEADS = 2
A_WINDOW = 128
B_HEADS = 8
B_CONFIGS = ((128, 1), (512, 4), (2048, 16))
NUM_BUCKETS = 32
MAX_DISTANCE = 2048
C_HEADS = 16
C_Q_RANK = 256
C_KV_RANK = 128
C_NOPE = 64
C_ROPE = 32
C_V = 64
ROPE_THETA = 10000.0
EPS = 1e-6

A_WIDTH = A_HEADS * HEAD_DIM
A_KV_WIDTH = A_KV_HEADS * HEAD_DIM
B_WIDTH = B_HEADS * HEAD_DIM
AB_WIDTH = A_WIDTH + B_WIDTH
C_WIDTH = C_HEADS * C_V

LANES = 128
F32 = jnp.float32
BF16 = jnp.bfloat16
NEG_INF = float("-inf")
LOG2E = math.log2(math.e)
UNDERFLOW_GUARD = 2.0 ** -100
BF16_NORM_SLACK = (1.0 + 2.0 ** -8) ** 2

ROW_TILE = 512
FLASH_TILE = 1024
VMEM_LIMIT = 48 * 1024 * 1024

N_PAIRS = 4
KA_SLOT, VA_SLOT, N_A_SLOTS = 4, 5, 6
QB_SLOT, KB_SLOT, VB_SLOT, N_B_SLOTS = 0, 4, 8, 12
BAND_UNROLL = 8
SWA_UNROLL = 2
A_PAIR_HEADS = [h for p in range(N_PAIRS) for h in (p, p + N_PAIRS)]
B_PAIR_HEADS = list(range(A_HEADS, A_HEADS + B_HEADS))


def _params(n_axes):
    return pltpu.CompilerParams(dimension_semantics=("arbitrary",) * n_axes,
                                vmem_limit_bytes=VMEM_LIMIT)


def _t5_bucket_np(dist):
    max_exact = NUM_BUCKETS // 2
    d = np.maximum(dist, 1).astype(np.float64)
    large = max_exact + (np.log(d / max_exact) / math.log(MAX_DISTANCE / max_exact)
                         * (NUM_BUCKETS - max_exact)).astype(np.int64)
    large = np.minimum(large, NUM_BUCKETS - 1)
    return np.where(dist < max_exact, dist, large).astype(np.int32)


def _bias_tiles(rel_bias, heads, dilation, rows, cols, offset):
    buckets = _t5_bucket_np(np.arange(BLOCK + 1) * dilation)
    per_head = jnp.take(rel_bias.astype(F32).T, np.asarray(heads), axis=0) * LOG2E
    dist = (lax.broadcasted_iota(jnp.int32, (rows, cols), 0) + offset
            - lax.broadcasted_iota(jnp.int32, (rows, cols), 1))
    tiles = jnp.full((len(heads), rows, cols), NEG_INF, F32)
    for bucket in np.unique(buckets):
        first = int(np.argmax(buckets == bucket))
        tiles = jnp.where((dist >= first)[None], per_head[:, int(bucket)][:, None, None], tiles)
    tiles = jnp.where(((dist >= 0) & (dist <= BLOCK))[None], tiles, NEG_INF)
    return tiles.reshape(len(heads) // 2, 2 * rows, cols)


def _head_stats(rel_bias, heads, sinks=None):
    order = np.asarray(heads)
    bias_max = jnp.take(jnp.max(rel_bias.astype(F32), axis=0), order) * LOG2E
    sink = jnp.zeros_like(bias_max) if sinks is None else jnp.take(sinks.astype(F32), order) * LOG2E
    stats = jnp.zeros((len(heads), 8), F32).at[:, 0].set(sink).at[:, 1].set(bias_max)
    return jnp.broadcast_to(stats[:, :, None], (len(heads), 8, LANES)).reshape(len(heads) // 2, 2, 8, LANES)


def _rms(x, w):
    return x * lax.rsqrt(jnp.mean(x * x, axis=-1, keepdims=True) + EPS) * w


def _rep(stat, width):
    if width == LANES:
        return stat
    return jnp.concatenate([stat] * (width // LANES), axis=1)


def _silu(z):
    return z / (1.0 + jnp.exp(-z))


def _dot_nt(a, b):
    return lax.dot_general(a, b, (((1,), (1,)), ((), ())), preferred_element_type=F32)


def _low_half(rows):
    return lax.broadcasted_iota(jnp.int32, (rows, LANES), 1) < HEAD_DIM


EVEN_NORM_SLOTS = (tuple(range(N_PAIRS)) + (KA_SLOT,)
                   + tuple(N_A_SLOTS + QB_SLOT + p for p in range(N_PAIRS))
                   + tuple(N_A_SLOTS + KB_SLOT + p for p in range(N_PAIRS)))
QA_NORM, KA_NORM, QB_NORM, KB_NORM = 0, N_PAIRS, N_PAIRS + 1, 2 * N_PAIRS + 1


def _store_norm_stats(nsq_ref, block, val, heads_per_slot):
    sq = val * val
    if heads_per_slot == 1:
        parts = [sq]
    else:
        low = _low_half(val.shape[0])
        parts = [jnp.where(low, sq, 0.0), jnp.where(low, 0.0, sq)]
    for hh, part in enumerate(parts):
        worst = jnp.max(jnp.sum(part, axis=-1, keepdims=True), axis=0, keepdims=True) * BF16_NORM_SLACK
        lanes = slice((block * heads_per_slot + hh) * LANES, (block * heads_per_slot + hh + 1) * LANES)
        nsq_ref[0, 0, :, lanes] = jnp.broadcast_to(worst, (8, LANES))


def _max_norms(nsq_ref):
    worst = nsq_ref[0, 0]
    for tile in range(1, nsq_ref.shape[1]):
        worst = jnp.maximum(worst, nsq_ref[0, tile])
    return jnp.sqrt(worst[:1, :])


def _gated_rows(g_refs):
    return jnp.concatenate([g_ref[0, p] for g_ref in g_refs for p in range(g_ref.shape[1])], axis=1)


def _projection_kernel(*refs, n_g, n_in, body):
    if n_g:
        g_refs, w_out_ref, x_ref, rest = refs[:n_g], refs[n_g], refs[n_g + 1], refs[n_g + 2:]
        ins, x_out_ref, outs = rest[:n_in], rest[n_in], rest[n_in + 1:]
        x = x_ref[0] + jnp.dot(_gated_rows(g_refs), w_out_ref[...], preferred_element_type=F32)
        x_out_ref[0] = x
    else:
        x, ins, outs = refs[0][0], refs[1:1 + n_in], refs[1 + n_in:]
    body(x, *ins, *outs)


def _projection_call(body, name, x, prev, ins, in_specs, out_specs, out_shape):
    x_spec = pl.BlockSpec((1, ROW_TILE, D_MODEL), lambda b, i: (b, i, 0))
    lead, lead_specs = [x], [x_spec]
    if prev is not None:
        gs, w_out = prev
        lead = [*gs, w_out, x]
        lead_specs = [pl.BlockSpec((1, g.shape[1], ROW_TILE, LANES), lambda b, i: (b, 0, i, 0)) for g in gs] + [
            pl.BlockSpec(w_out.shape, lambda b, i: (0, 0)), x_spec]
        out_specs = [x_spec] + out_specs
        out_shape = [jax.ShapeDtypeStruct(x.shape, F32)] + out_shape
    return pl.pallas_call(
        functools.partial(_projection_kernel, n_g=0 if prev is None else len(prev[0]), n_in=len(ins), body=body),
        grid=(BATCH, SEQ // ROW_TILE),
        in_specs=lead_specs + in_specs,
        out_specs=out_specs,
        out_shape=out_shape,
        compiler_params=_params(2),
        name=name if prev is None else "residual_" + name,
    )(*lead, *ins)


def _inproj_even_body(x, nw_ref, w_ref, a_ref, b_ref, g_ref, nsq_ref):
    hb = _rms(x, nw_ref[...]).astype(BF16)
    q_scale = HEAD_DIM ** -0.5 * LOG2E
    n_chunks = w_ref.shape[1] // (2 * LANES)
    for c in range(n_chunks):
        r = jnp.dot(hb, w_ref[:, c * 2 * LANES:(c + 1) * 2 * LANES], preferred_element_type=F32)
        for t in range(2):
            slot = 2 * c + t
            val = r[:, t * LANES:(t + 1) * LANES]
            if slot < N_A_SLOTS:
                val = val * q_scale if slot < N_PAIRS else val
                a_ref[0, slot] = val.astype(BF16)
            elif slot < N_A_SLOTS + N_B_SLOTS:
                val = val * q_scale if slot - N_A_SLOTS < N_PAIRS else val
                b_ref[0, slot - N_A_SLOTS] = val
            else:
                g_ref[0, slot - N_A_SLOTS - N_B_SLOTS] = _silu(val).astype(BF16)
            if slot in EVEN_NORM_SLOTS:
                _store_norm_stats(nsq_ref, EVEN_NORM_SLOTS.index(slot), val, 2)


def _inproj_even(x, norm_w, w, prev=None):
    n_z = AB_WIDTH // LANES
    slots = lambda n: pl.BlockSpec((1, n, ROW_TILE, LANES), lambda b, i: (b, 0, i, 0))
    out = lambda n, dtype: jax.ShapeDtypeStruct((BATCH, n, SEQ, LANES), dtype)
    return _projection_call(
        _inproj_even_body, "inproj_even", x, prev,
        ins=[norm_w.reshape(1, D_MODEL), w],
        in_specs=[pl.BlockSpec((1, D_MODEL), lambda b, i: (0, 0)), pl.BlockSpec(w.shape, lambda b, i: (0, 0))],
        out_specs=[slots(N_A_SLOTS), slots(N_B_SLOTS), slots(n_z),
                   pl.BlockSpec((1, 1, 8, 2 * len(EVEN_NORM_SLOTS) * LANES), lambda b, i: (b, i, 0, 0))],
        out_shape=[out(N_A_SLOTS, BF16), out(N_B_SLOTS, F32), out(n_z, BF16),
                   jax.ShapeDtypeStruct((BATCH, SEQ // ROW_TILE, 8, 2 * len(EVEN_NORM_SLOTS) * LANES), F32)])


def _outproj_kernel(*refs, n_g, final):
    g_refs, (w_ref, x_ref), rest = refs[:n_g], refs[n_g:n_g + 2], refs[n_g + 2:]
    y = x_ref[0] + jnp.dot(_gated_rows(g_refs), w_ref[...], preferred_element_type=F32)
    if final:
        fw_ref, o_ref = rest
        y = _rms(y, fw_ref[...])
    else:
        (o_ref,) = rest
    o_ref[0] = y


def _outproj(gs, w, x, final_w=None):
    final = final_w is not None
    in_specs = [pl.BlockSpec((1, g.shape[1], ROW_TILE, LANES), lambda b, i: (b, 0, i, 0)) for g in gs] + [
        pl.BlockSpec(w.shape, lambda b, i: (0, 0)),
        pl.BlockSpec((1, ROW_TILE, D_MODEL), lambda b, i: (b, i, 0)),
    ]
    args = [*gs, w, x]
    if final:
        in_specs.append(pl.BlockSpec((1, D_MODEL), lambda b, i: (0, 0)))
        args.append(final_w.reshape(1, D_MODEL))
    return pl.pallas_call(
        functools.partial(_outproj_kernel, n_g=len(gs), final=final),
        grid=(BATCH, SEQ // ROW_TILE),
        in_specs=in_specs,
        out_specs=pl.BlockSpec((1, ROW_TILE, D_MODEL), lambda b, i: (b, i, 0)),
        out_shape=jax.ShapeDtypeStruct((BATCH, SEQ, D_MODEL), F32),
        compiler_params=_params(2),
        name="outproj_final" if final else "outproj",
    )(*args)


def _stack_heads(q):
    lane = lax.broadcasted_iota(jnp.int32, (1, LANES), 1)
    keep0 = jnp.where(lane < HEAD_DIM, 1.0, 0.0).astype(q.dtype)
    return jnp.concatenate([q * keep0, q * (1.0 - keep0).astype(q.dtype)], axis=0)


def _pair_shift(qsq_ref, ksq_ref, stat_ref):
    bias_max = jnp.concatenate([stat_ref[0, hh, 1:2, :] for hh in range(2)], axis=1)
    return _max_norms(qsq_ref) * _max_norms(ksq_ref) + bias_max


def _stacked_rows(pair_row, rows):
    return jnp.concatenate([jnp.broadcast_to(pair_row[:, hh * LANES:(hh + 1) * LANES], (rows, LANES))
                            for hh in range(2)], axis=0)


def _fast_unit(q, k, v, shifted_bias, low):
    r = q.shape[0]
    p = jnp.exp2(_dot_nt(_stack_heads(q), k) + shifted_bias)
    sums = jnp.sum(p, axis=-1, keepdims=True)
    pv = jnp.dot(p.astype(BF16), v, preferred_element_type=F32)
    return jnp.where(low, pv[:r], pv[r:]), sums[:r], sums[r:]


def _sweep_blocks(n_blocks, block_fn, carry, unroll=BAND_UNROLL):
    carry = block_fn(0, 0, BLOCK, carry)

    def body(i, c):
        return block_fn(pl.multiple_of(i * BLOCK, BLOCK), pl.multiple_of((i - 1) * BLOCK, BLOCK), 2 * BLOCK, c)

    return lax.fori_loop(1, n_blocks, body, carry, unroll=unroll)


def _att_a_kernel(q_ref, k_ref, v_ref, g_ref, bias_ref, stat_ref, qsq_ref, ksq_ref, o_ref, shifted_bias_ref):
    low = _low_half(BLOCK)
    groups = 2 * N_PAIRS

    def stat_row(r):
        return jnp.concatenate([stat_ref[p, hh, r:r + 1, :] for p in range(N_PAIRS) for hh in range(2)], axis=1)

    def group_rows(row):
        return jnp.concatenate([jnp.broadcast_to(row[:, g * LANES:(g + 1) * LANES], (BLOCK, LANES))
                                for g in range(groups)], axis=0)

    sink = group_rows(stat_row(0))
    bound = _max_norms(qsq_ref) * jnp.concatenate([_max_norms(ksq_ref)] * N_PAIRS, axis=1) + stat_row(1)
    shift = jnp.maximum(group_rows(bound), sink)
    shifted_bias_ref[...] = bias_ref[...] - _rep(shift, 2 * BLOCK)
    sink_term = jnp.exp2(sink - shift)

    def operands(qs, ks, n_keys, bias_of):
        rows, keys = pl.ds(qs, BLOCK), pl.ds(ks, n_keys)
        q = jnp.concatenate([_stack_heads(q_ref[0, p, rows, :]) for p in range(N_PAIRS)], axis=0)
        return rows, q, k_ref[0, 0, keys, :], v_ref[0, 0, keys, :], bias_of[:, 2 * BLOCK - n_keys:]

    def emit(rows, pv, denom):
        for p in range(N_PAIRS):
            h0, h1 = slice(2 * p * BLOCK, (2 * p + 1) * BLOCK), slice((2 * p + 1) * BLOCK, (2 * p + 2) * BLOCK)
            o = jnp.where(low, pv[h0], pv[h1]) / jnp.where(low, denom[h0], denom[h1])
            o_ref[0, p, rows, :] = (o * g_ref[0, p, rows, :].astype(F32)).astype(BF16)

    def fast_block(qs, ks, n_keys, l_min):
        rows, q, k, v, shifted_bias = operands(qs, ks, n_keys, shifted_bias_ref)
        p = jnp.exp2(_dot_nt(q, k) + shifted_bias)
        denom = jnp.sum(p, axis=-1, keepdims=True) + sink_term
        emit(rows, jnp.dot(p.astype(BF16), v, preferred_element_type=F32), denom)
        for g in range(groups):
            l_min = jnp.minimum(l_min, denom[g * BLOCK:(g + 1) * BLOCK])
        return l_min

    def exact_block(qs, ks, n_keys, carry):
        rows, q, k, v, bias = operands(qs, ks, n_keys, bias_ref)
        s = _dot_nt(q, k) + bias
        m = jnp.maximum(jnp.max(s, axis=-1, keepdims=True), sink)
        p = jnp.exp2(s - _rep(m, n_keys))
        denom = jnp.sum(p, axis=-1, keepdims=True) + jnp.exp2(sink - m)
        emit(rows, jnp.dot(p.astype(BF16), v, preferred_element_type=F32), denom)
        return carry

    l_min = _sweep_blocks(SEQ // BLOCK, fast_block, jnp.full((BLOCK, LANES), jnp.inf, F32), SWA_UNROLL)

    @pl.when(jnp.logical_not(jnp.min(l_min) >= UNDERFLOW_GUARD))
    def _():
        _sweep_blocks(SEQ // BLOCK, exact_block, 0, SWA_UNROLL)


def _nsq_spec(block_of):
    return pl.BlockSpec((1, SEQ // ROW_TILE, 8, 2 * LANES), lambda b, p: (b, 0, 0, block_of(p)))


def _att_a(qkv, z, bias, stats, nsq):
    seq_spec = lambda n, slot: pl.BlockSpec((1, n, SEQ, LANES), lambda b: (b, slot, 0, 0))
    tiles = SEQ // ROW_TILE
    return pl.pallas_call(
        _att_a_kernel,
        grid=(BATCH,),
        in_specs=[
            seq_spec(N_PAIRS, 0), seq_spec(1, KA_SLOT), seq_spec(1, VA_SLOT), seq_spec(N_PAIRS, 0),
            pl.BlockSpec((N_PAIRS * 2 * BLOCK, 2 * BLOCK), lambda b: (0, 0)),
            pl.BlockSpec(stats.shape, lambda b: (0, 0, 0, 0)),
            pl.BlockSpec((1, tiles, 8, N_PAIRS * 2 * LANES), lambda b: (b, 0, 0, QA_NORM)),
            pl.BlockSpec((1, tiles, 8, 2 * LANES), lambda b: (b, 0, 0, KA_NORM)),
        ],
        out_specs=seq_spec(N_PAIRS, 0),
        out_shape=jax.ShapeDtypeStruct((BATCH, N_PAIRS, SEQ, LANES), BF16),
        scratch_shapes=[pltpu.VMEM((N_PAIRS * 2 * BLOCK, 2 * BLOCK), F32)],
        compiler_params=_params(1),
        name="att_swa",
    )(qkv, qkv, qkv, z, bias.reshape(N_PAIRS * 2 * BLOCK, 2 * BLOCK), stats, nsq, nsq)


def _att_b_kernel(q_ref, k_ref, v_ref, g_ref, b1_ref, b4_ref, b16_ref, stat_ref, qsq_ref, ksq_ref, o_ref,
                  m0_ref, m1_ref, l_ref, acc_ref, sb1_ref, sb4_ref, sb16_ref):
    low = _low_half(BLOCK)
    load = lambda ref, rows: ref[0, 0, rows, :].astype(BF16)
    shift = _pair_shift(qsq_ref, ksq_ref, stat_ref)
    for bias_ref, shifted_ref in ((b1_ref, sb1_ref), (b4_ref, sb4_ref), (b16_ref, sb16_ref)):
        shifted_ref[...] = bias_ref[0] - _rep(_stacked_rows(shift, bias_ref.shape[1] // 2), bias_ref.shape[2])

    def sweep(unit, bias1, bias4, bias16):
        def residue16(r, carry):
            rows = pl.ds(r, SEQ // 16, stride=16)
            unit(rows, load(q_ref, rows), load(k_ref, rows), load(v_ref, rows), bias16[...], True)
            return carry

        lax.fori_loop(0, 16, residue16, 0, unroll=2)

        def residue4(f, carry):
            def block4(us, ks, n_keys, c):
                rows, keys = pl.ds(us * 4 + f, BLOCK, stride=4), pl.ds(ks * 4 + f, n_keys, stride=4)
                unit(rows, load(q_ref, rows), load(k_ref, keys), load(v_ref, keys),
                     bias4[:, 2 * BLOCK - n_keys:], False)
                return c

            return _sweep_blocks(SEQ // 4 // BLOCK, block4, carry)

        lax.fori_loop(0, 4, residue4, 0)

        def block1(qs, ks, n_keys, carry):
            rows, keys = pl.ds(qs, BLOCK), pl.ds(ks, n_keys)
            unit(rows, load(q_ref, rows), load(k_ref, keys), load(v_ref, keys),
                 bias1[:, 2 * BLOCK - n_keys:], False)
            return carry

        _sweep_blocks(SEQ // BLOCK, block1, 0)

    def fast_unit(rows, q, k, v, shifted_bias, first):
        low_r = low if q.shape[0] == BLOCK else _low_half(q.shape[0])
        pv, sum0, sum1 = _fast_unit(q, k, v, shifted_bias, low_r)
        sums = jnp.where(low_r, sum0, sum1)
        if first:
            acc_ref[rows, :] = pv
            l_ref[rows, :] = sums
        else:
            acc_ref[rows, :] += pv
            l_ref[rows, :] += sums

    def exact_unit(rows, q, k, v, bias, first):
        r, n_keys = q.shape[0], k.shape[0]
        low_r = low if r == BLOCK else _low_half(r)
        s = _dot_nt(_stack_heads(q), k) + bias
        m_new = jnp.broadcast_to(jnp.max(s, axis=-1, keepdims=True), (2 * r, LANES))
        if not first:
            m_old = jnp.concatenate([m0_ref[rows, :], m1_ref[rows, :]], axis=0)
            m_new = jnp.maximum(m_old, m_new)
            alpha = jnp.exp2(m_old - m_new)
            alpha = jnp.where(low_r, alpha[:r], alpha[r:])
        p = jnp.exp2(s - _rep(m_new, n_keys))
        sums = jnp.sum(p, axis=-1, keepdims=True)
        sums = jnp.where(low_r, sums[:r], sums[r:])
        pv = jnp.dot(p.astype(BF16), v, preferred_element_type=F32)
        pv = jnp.where(low_r, pv[:r], pv[r:])
        m0_ref[rows, :] = m_new[:r]
        m1_ref[rows, :] = m_new[r:]
        if first:
            acc_ref[rows, :] = pv
            l_ref[rows, :] = sums
        else:
            acc_ref[rows, :] = alpha * acc_ref[rows, :] + pv
            l_ref[rows, :] = alpha * l_ref[rows, :] + sums

    def finish(i, l_min):
        rows = pl.ds(pl.multiple_of(i * BLOCK, BLOCK), BLOCK)
        l = l_ref[rows, :]
        gate = g_ref[0, 0, rows, :].astype(F32)
        o_ref[0, 0, rows, :] = (acc_ref[rows, :] / l * gate).astype(BF16)
        return jnp.minimum(l_min, l)

    no_min = jnp.full((BLOCK, LANES), jnp.inf, F32)
    sweep(fast_unit, sb1_ref, sb4_ref, sb16_ref)
    l_min = lax.fori_loop(0, SEQ // BLOCK, finish, no_min, unroll=BAND_UNROLL)

    @pl.when(jnp.logical_not(jnp.min(l_min) >= UNDERFLOW_GUARD))
    def _():
        sweep(exact_unit, b1_ref.at[0], b4_ref.at[0], b16_ref.at[0])
        lax.fori_loop(0, SEQ // BLOCK, finish, no_min)


def _att_b(qkv, z, b1, b4, b16, stats, nsq):
    seq_spec = lambda slot: pl.BlockSpec((1, 1, SEQ, LANES), lambda b, p: (b, slot + p, 0, 0))
    bias_spec = lambda a: pl.BlockSpec((1,) + a.shape[1:], lambda b, p: (p, 0, 0))
    return pl.pallas_call(
        _att_b_kernel,
        grid=(BATCH, N_PAIRS),
        in_specs=[seq_spec(QB_SLOT), seq_spec(KB_SLOT), seq_spec(VB_SLOT), seq_spec(N_PAIRS),
                  bias_spec(b1), bias_spec(b4), bias_spec(b16),
                  pl.BlockSpec((1, 2, 8, LANES), lambda b, p: (p, 0, 0, 0)),
                  _nsq_spec(lambda p: QB_NORM + p), _nsq_spec(lambda p: KB_NORM + p)],
        out_specs=seq_spec(0),
        out_shape=jax.ShapeDtypeStruct((BATCH, N_PAIRS, SEQ, LANES), BF16),
        scratch_shapes=[pltpu.VMEM((SEQ, LANES), F32)] * 4 + [pltpu.VMEM(b.shape[1:], F32) for b in (b1, b4, b16)],
        compiler_params=_params(2),
        name="att_dilated",
    )(qkv, qkv, qkv, z, b1, b4, b16, stats, nsq, nsq)


def _rope_table_kernel(pos_ref, freq_ref, cos_ref, sin_ref):
    ang = pos_ref[0].astype(F32) * freq_ref[...]
    cos_ref[0] = jnp.cos(ang)
    sin_ref[0] = jnp.sin(ang)


def _rope_tables(positions):
    inv_freq = ROPE_THETA ** (-jnp.arange(0, C_ROPE, 2, dtype=F32) / C_ROPE)
    lane = np.arange(LANES)
    freq = jnp.where(jnp.asarray(lane >= C_NOPE), inv_freq[lane % (C_ROPE // 2)], 0.0).reshape(1, LANES)
    table = jax.ShapeDtypeStruct((BATCH, SEQ, LANES), F32)
    spec = pl.BlockSpec((1, ROW_TILE, LANES), lambda b, i: (b, i, 0))
    return pl.pallas_call(
        _rope_table_kernel,
        grid=(BATCH, SEQ // ROW_TILE),
        in_specs=[pl.BlockSpec((1, ROW_TILE, 1), lambda b, i: (b, i, 0)),
                  pl.BlockSpec((1, LANES), lambda b, i: (0, 0))],
        out_specs=[spec, spec],
        out_shape=[table, table],
        compiler_params=_params(2),
        name="rope_tables",
    )(positions.reshape(BATCH, SEQ, 1), freq)


C_LOWRANK = C_Q_RANK + C_KV_RANK + 2 * LANES


def _inproj_odd_body(x, nw_ref, w_ref, qnw_ref, wq_ref, kvnw_ref, wk_ref, wv_ref, cos_ref, sin_ref,
                     q_ref, k_ref, v_ref, g_ref, nsq_ref):
    hb = _rms(x, nw_ref[...]).astype(BF16)
    c = jnp.dot(hb, w_ref[:, :C_LOWRANK], preferred_element_type=F32)
    for j in range(C_WIDTH // (2 * LANES)):
        lo = C_LOWRANK + j * 2 * LANES
        r = jnp.dot(hb, w_ref[:, lo:lo + 2 * LANES], preferred_element_type=F32)
        for t in range(2):
            g_ref[0, 2 * j + t] = _silu(r[:, t * LANES:(t + 1) * LANES]).astype(BF16)

    qn = _rms(c[:, :C_Q_RANK], qnw_ref[...]).astype(BF16)
    kvn = _rms(c[:, C_Q_RANK:C_Q_RANK + C_KV_RANK], kvnw_ref[...]).astype(BF16)
    k_pe = c[:, C_Q_RANK + C_KV_RANK:C_Q_RANK + C_KV_RANK + LANES]
    k_pe_rot = c[:, C_Q_RANK + C_KV_RANK + LANES:]
    cos, sin = cos_ref[0], sin_ref[0]
    lane = lax.broadcasted_iota(jnp.int32, cos.shape, 1)
    q_table = jnp.where(lane < C_NOPE + C_ROPE, cos, sin) * ((C_NOPE + C_ROPE) ** -0.5 * LOG2E)
    k_rope = k_pe * cos + k_pe_rot * sin

    for j in range(C_HEADS // 2):
        cols = slice(j * 2 * LANES, (j + 1) * 2 * LANES)
        rq = jnp.dot(qn, wq_ref[:, cols], preferred_element_type=F32)
        rk = jnp.dot(kvn, wk_ref[:, cols], preferred_element_type=F32)
        for t in range(2):
            lanes = slice(t * LANES, (t + 1) * LANES)
            q_head, k_head = rq[:, lanes] * q_table, rk[:, lanes] + k_rope
            q_ref[0, 2 * j + t] = q_head.astype(BF16)
            k_ref[0, 2 * j + t] = k_head.astype(BF16)
            _store_norm_stats(nsq_ref, 2 * j + t, q_head, 1)
            _store_norm_stats(nsq_ref, C_HEADS + 2 * j + t, k_head, 1)
    for j in range(C_WIDTH // (2 * LANES)):
        rv = jnp.dot(kvn, wv_ref[:, j * 2 * LANES:(j + 1) * 2 * LANES], preferred_element_type=F32)
        for t in range(2):
            v_ref[0, 2 * j + t] = rv[:, t * LANES:(t + 1) * LANES].astype(BF16)


def _inproj_odd(x, norm_w, w, q_norm, wq, kv_norm, wk, wv, cos, sin, prev=None):
    full = lambda a: pl.BlockSpec(a.shape, lambda b, i: (0,) * a.ndim)
    row = lambda width: pl.BlockSpec((1, ROW_TILE, width), lambda b, i: (b, i, 0))
    slots = lambda n: pl.BlockSpec((1, n, ROW_TILE, LANES), lambda b, i: (b, 0, i, 0))
    out = lambda n: jax.ShapeDtypeStruct((BATCH, n, SEQ, LANES), BF16)
    n_pairs = C_WIDTH // LANES
    vecs = [norm_w.reshape(1, D_MODEL), q_norm.reshape(1, C_Q_RANK), kv_norm.reshape(1, C_KV_RANK)]
    return _projection_call(
        _inproj_odd_body, "inproj_mla", x, prev,
        ins=[vecs[0], w, vecs[1], wq, vecs[2], wk, wv, cos, sin],
        in_specs=[full(vecs[0]), full(w), full(vecs[1]), full(wq), full(vecs[2]), full(wk), full(wv),
                  row(LANES), row(LANES)],
        out_specs=[slots(C_HEADS), slots(C_HEADS), slots(n_pairs), slots(n_pairs),
                   pl.BlockSpec((1, 1, 8, 2 * C_HEADS * LANES), lambda b, i: (b, i, 0, 0))],
        out_shape=[out(C_HEADS), out(C_HEADS), out(n_pairs), out(n_pairs),
                   jax.ShapeDtypeStruct((BATCH, SEQ // ROW_TILE, 8, 2 * C_HEADS * LANES), F32)])


def _flash_kernel(q_ref, k_ref, v_ref, g_ref, qsq_ref, ksq_ref, o_ref, vt_ref, redo_ref,
                  m0_ref, m1_ref, l0_ref, l1_ref, acc_ref):
    t, half = FLASH_TILE, FLASH_TILE // 2
    n_tiles = SEQ // t
    step = pl.program_id(2)
    low = _low_half(t)
    m_refs, l_refs = (m0_ref, m1_ref), (l0_ref, l1_ref)

    @pl.when(step == 0)
    def _():
        def prep_body(c, carry):
            rows = pl.ds(pl.multiple_of(c * half, half), half)
            v_t = v_ref[0, 0, rows, :].astype(F32).T
            for hh in range(2):
                vt_ref[hh, :, rows] = v_t[hh * HEAD_DIM:(hh + 1) * HEAD_DIM].astype(BF16)
            return carry

        lax.fori_loop(0, SEQ // half, prep_body, 0)

    bound = _max_norms(qsq_ref) * _max_norms(ksq_ref)
    shifts = [bound[:, hh * LANES:hh * LANES + 1] for hh in range(2)]

    def fast_tile(tile):
        qs = tile * t

        def part(hh, ks, width, q_lo, q_n, diagonal):
            s_t = _dot_nt(k_ref[0, hh, ks:ks + width, :], q_ref[0, hh, qs + q_lo:qs + q_lo + q_n, :])
            if diagonal:
                causal = (lax.broadcasted_iota(jnp.int32, (width, q_n), 0)
                          <= lax.broadcasted_iota(jnp.int32, (width, q_n), 1))
                s_t = jnp.where(causal, s_t, NEG_INF)
            p_t = jnp.exp2(s_t - shifts[hh])
            return (jnp.dot(vt_ref[hh, :, ks:ks + width], p_t.astype(BF16), preferred_element_type=F32),
                    jnp.sum(p_t, axis=0, keepdims=True))

        def add_late(total, late):
            return jnp.concatenate([total[:, :half], total[:, half:] + late], axis=1)

        o_t, sums = [], []
        for hh in range(2):
            acc, l = part(hh, qs, half, 0, t, True)
            for j in range(tile):
                acc_j, l_j = part(hh, j * t, t, 0, t, False)
                acc, l = acc + acc_j, l + l_j
            acc_late, l_late = part(hh, qs + half, half, half, half, True)
            sums.append(add_late(l, l_late))
            o_t.append(add_late(acc, acc_late) / sums[hh])
        gate = g_ref[0, 0, qs:qs + t, :].astype(F32)
        o_ref[0, 0, qs:qs + t, :] = (jnp.concatenate(o_t, axis=0).T * gate).astype(BF16)
        return jnp.minimum(jnp.min(sums[0]), jnp.min(sums[1])) >= UNDERFLOW_GUARD

    for first in range(n_tiles // 2):
        @pl.when(step == first)
        def _(first=first):
            healthy = jnp.logical_and(fast_tile(first), fast_tile(n_tiles - 1 - first))
            redo_ref[0] = jnp.where(healthy, 0, 1).astype(jnp.int32)

    def exact_tile(which, carry):
        tile = jnp.where(which == 0, step, n_tiles - 1 - step)
        q_start = pl.multiple_of(tile * t, t)
        q_rows = pl.ds(q_start, t)

        def exact_step(j, c):
            ks = pl.multiple_of(j * half, half)
            v = v_ref[0, 0, pl.ds(ks, half), :]
            visible = (lax.broadcasted_iota(jnp.int32, (t, half), 1) + (ks - q_start)
                       <= lax.broadcasted_iota(jnp.int32, (t, half), 0))
            pvs, alphas = [], []
            for hh in range(2):
                s = jnp.where(visible, _dot_nt(q_ref[0, hh, q_rows, :], k_ref[0, hh, pl.ds(ks, half), :]), NEG_INF)
                m_old = m_refs[hh][...]
                m_new = jnp.maximum(m_old, jnp.max(s, axis=-1, keepdims=True))
                alpha = jnp.exp2(m_old - m_new)
                p = jnp.exp2(s - _rep(m_new, half))
                m_refs[hh][...] = m_new
                l_refs[hh][...] = alpha * l_refs[hh][...] + jnp.sum(p, axis=-1, keepdims=True)
                pvs.append(jnp.dot(p.astype(BF16), v, preferred_element_type=F32))
                alphas.append(alpha)
            acc_ref[...] = jnp.where(low, alphas[0], alphas[1]) * acc_ref[...] + jnp.where(low, pvs[0], pvs[1])
            return c

        for hh in range(2):
            m_refs[hh][...] = jnp.full((t, LANES), NEG_INF, F32)
            l_refs[hh][...] = jnp.zeros((t, LANES), F32)
        acc_ref[...] = jnp.zeros((t, LANES), F32)
        lax.fori_loop(0, 2 * tile + 2, exact_step, 0)
        o = acc_ref[...] / jnp.where(low, l0_ref[...], l1_ref[...])
        o_ref[0, 0, q_rows, :] = (o * g_ref[0, 0, q_rows, :].astype(F32)).astype(BF16)
        return carry

    @pl.when(redo_ref[0] != 0)
    def _():
        lax.fori_loop(0, 2, exact_tile, 0)


def _flash(q, k, v, z, nsq):
    t = FLASH_TILE
    n_pairs = C_WIDTH // LANES
    heads = pl.BlockSpec((1, 2, SEQ, LANES), lambda b, p, i: (b, p, 0, 0))
    pair = pl.BlockSpec((1, 1, SEQ, LANES), lambda b, p, i: (b, p, 0, 0))
    return pl.pallas_call(
        _flash_kernel,
        grid=(BATCH, n_pairs, SEQ // t // 2),
        in_specs=[
            heads, heads, pair, pair,
            pl.BlockSpec((1, SEQ // ROW_TILE, 8, 2 * LANES), lambda b, p, i: (b, 0, 0, p)),
            pl.BlockSpec((1, SEQ // ROW_TILE, 8, 2 * LANES), lambda b, p, i: (b, 0, 0, C_HEADS // 2 + p)),
        ],
        out_specs=pair,
        out_shape=jax.ShapeDtypeStruct((BATCH, n_pairs, SEQ, LANES), BF16),
        scratch_shapes=[pltpu.VMEM((2, HEAD_DIM, SEQ), BF16), pltpu.SMEM((1,), jnp.int32)]
        + [pltpu.VMEM((t, LANES), F32)] * 5,
        compiler_params=_params(3),
        name="mla_flash",
    )(q, k, v, z, nsq, nsq)


def _pair_order_cols(w):
    rows = w.shape[0]
    return w.reshape(rows, A_KV_HEADS, N_PAIRS, HEAD_DIM).transpose(0, 2, 1, 3).reshape(rows, A_WIDTH)


def _even_weights(w_in, w_out):
    z_lo = A_WIDTH + 2 * A_KV_WIDTH + 3 * B_WIDTH
    w = jnp.concatenate([_pair_order_cols(w_in[:, :A_WIDTH]), w_in[:, A_WIDTH:z_lo],
                         _pair_order_cols(w_in[:, z_lo:z_lo + A_WIDTH]), w_in[:, z_lo + A_WIDTH:]],
                        axis=1).astype(BF16)
    wo_a = w_out[:A_WIDTH].reshape(A_KV_HEADS, N_PAIRS, HEAD_DIM, D_MODEL).transpose(1, 0, 2, 3)
    w_o = jnp.concatenate([wo_a.reshape(A_WIDTH, D_MODEL), w_out[A_WIDTH:]], axis=0).astype(BF16)
    return w, w_o


def _rot_half_cols(w):
    half = w.shape[1] // 2
    return jnp.concatenate([-w[:, half:], w[:, :half]], axis=1)


def _odd_weights(w_in, w_qb, w_kvb):
    splits = np.cumsum([C_Q_RANK, C_KV_RANK, C_ROPE])
    w_cq, w_ckv, w_kpe, w_z = jnp.split(w_in, splits, axis=1)
    pad = jnp.zeros((D_MODEL, C_NOPE), w_in.dtype)
    w_kpe_rot = _rot_half_cols(w_kpe)
    w = jnp.concatenate([w_cq, w_ckv, pad, w_kpe, w_kpe, pad, w_kpe_rot, w_kpe_rot, w_z], axis=1).astype(BF16)

    wq = w_qb.reshape(C_Q_RANK, C_HEADS, C_NOPE + C_ROPE)
    wq_pe = wq[:, :, C_NOPE:]
    wq_rot = jnp.concatenate([-wq_pe[:, :, C_ROPE // 2:], wq_pe[:, :, :C_ROPE // 2]], axis=2)
    wq = jnp.concatenate([wq, wq_rot], axis=2).reshape(C_Q_RANK, C_HEADS * LANES).astype(BF16)

    wkv = w_kvb.reshape(C_KV_RANK, C_HEADS, C_NOPE + C_V)
    wk = jnp.concatenate([wkv[:, :, :C_NOPE], jnp.zeros((C_KV_RANK, C_HEADS, LANES - C_NOPE), w_kvb.dtype)], axis=2)
    wk = wk.reshape(C_KV_RANK, C_HEADS * LANES).astype(BF16)
    wv = wkv[:, :, C_NOPE:].reshape(C_KV_RANK, C_WIDTH).astype(BF16)
    return w, wq, wk, wv


def kernel(x, positions, norm_w, rel_bias, ab_w_in, ab_sinks, ab_w_out, c_w_in, c_q_norm, c_w_qb, c_kv_norm,
           c_w_kvb, c_w_out, final_norm):
    bias_a = _bias_tiles(rel_bias, A_PAIR_HEADS, 1, BLOCK, 2 * BLOCK, BLOCK)
    bias_b1 = _bias_tiles(rel_bias, B_PAIR_HEADS, 1, BLOCK, 2 * BLOCK, BLOCK)
    bias_b4 = _bias_tiles(rel_bias, B_PAIR_HEADS, 4, BLOCK, 2 * BLOCK, BLOCK)
    bias_b16 = _bias_tiles(rel_bias, B_PAIR_HEADS, 16, SEQ // 16, SEQ // 16, 0)
    stats_b = _head_stats(rel_bias, B_PAIR_HEADS)
    cos, sin = _rope_tables(positions)

    prev = None
    for layer in range(DEPTH):
        i = layer // 2
        if layer % 2 == 0:
            w, w_o = _even_weights(ab_w_in[i], ab_w_out[i])
            outs = _inproj_even(x, norm_w[layer], w, prev)
            if prev is not None:
                x, outs = outs[0], outs[1:]
            qkv_a, qkv_b, gate, nsq = outs
            gs = [_att_a(qkv_a, gate, bias_a, _head_stats(rel_bias, A_PAIR_HEADS, ab_sinks[i]), nsq),
                  _att_b(qkv_b, gate, bias_b1, bias_b4, bias_b16, stats_b, nsq)]
        else:
            w, wq, wk, wv = _odd_weights(c_w_in[i], c_w_qb[i], c_w_kvb[i])
            outs = _inproj_odd(x, norm_w[layer], w, c_q_norm[i], wq, c_kv_norm[i], wk, wv, cos, sin, prev)
            if prev is not None:
                x, outs = outs[0], outs[1:]
            q, k, v, gate, nsq = outs
            gs = [_flash(q, k, v, gate, nsq)]
            w_o = c_w_out[i].astype(BF16)
        prev = (gs, w_o)
    return _outproj(prev[0], prev[1], x, final_norm)
```

```python
import functools
import math

import numpy as np
import jax
import jax.numpy as jnp
from jax import lax
from jax.experimental import pallas as pl
from jax.experimental.pallas import tpu as pltpu

D_MODEL = 1024
BATCH = 4
SEQ = 4096
DEPTH = 4
HEAD_DIM = 64
BLOCK = 128
A_HEADS = 8
A_KV_HEADS = 2
A_WINDOW = 128
B_HEADS = 8
B_CONFIGS = ((128, 1), (512, 4), (2048, 16))
NUM_BUCKETS = 32
MAX_DISTANCE = 2048
C_HEADS = 16
C_Q_RANK = 256
C_KV_RANK = 128
C_NOPE = 64
C_ROPE = 32
C_V = 64
ROPE_THETA = 10000.0
EPS = 1e-6

A_WIDTH = A_HEADS * HEAD_DIM
A_KV_WIDTH = A_KV_HEADS * HEAD_DIM
B_WIDTH = B_HEADS * HEAD_DIM
AB_WIDTH = A_WIDTH + B_WIDTH
C_WIDTH = C_HEADS * C_V

LANES = 128
F32 = jnp.float32
BF16 = jnp.bfloat16
NEG_INF = float("-inf")
LOG2E = math.log2(math.e)
UNDERFLOW_GUARD = 2.0 ** -100
BF16_NORM_SLACK = (1.0 + 2.0 ** -8) ** 2

ROW_TILE = 512
FLASH_TILE = 1024
VMEM_LIMIT = 48 * 1024 * 1024

N_PAIRS = 4
KA_SLOT, VA_SLOT, N_A_SLOTS = 4, 5, 6
QB_SLOT, KB_SLOT, VB_SLOT, N_B_SLOTS = 0, 4, 8, 12
BAND_UNROLL = 8
SWA_UNROLL = 2
A_PAIR_HEADS = [h for p in range(N_PAIRS) for h in (p, p + N_PAIRS)]
B_PAIR_HEADS = list(range(A_HEADS, A_HEADS + B_HEADS))


def _params(n_axes):
    return pltpu.CompilerParams(dimension_semantics=("arbitrary",) * n_axes,
                                vmem_limit_bytes=VMEM_LIMIT)


def _t5_bucket_np(dist):
    max_exact = NUM_BUCKETS // 2
    d = np.maximum(dist, 1).astype(np.float64)
    large = max_exact + (np.log(d / max_exact) / math.log(MAX_DISTANCE / max_exact)
                         * (NUM_BUCKETS - max_exact)).astype(np.int64)
    large = np.minimum(large, NUM_BUCKETS - 1)
    return np.where(dist < max_exact, dist, large).astype(np.int32)


def _bias_tiles(rel_bias, heads, dilation, rows, cols, offset):
    buckets = _t5_bucket_np(np.arange(BLOCK + 1) * dilation)
    per_head = jnp.take(rel_bias.astype(F32).T, np.asarray(heads), axis=0) * LOG2E
    dist = (lax.broadcasted_iota(jnp.int32, (rows, cols), 0) + offset
            - lax.broadcasted_iota(jnp.int32, (rows, cols), 1))
    tiles = jnp.full((len(heads), rows, cols), NEG_INF, F32)
    for bucket in np.unique(buckets):
        first = int(np.argmax(buckets == bucket))
        tiles = jnp.where((dist >= first)[None], per_head[:, int(bucket)][:, None, None], tiles)
    tiles = jnp.where(((dist >= 0) & (dist <= BLOCK))[None], tiles, NEG_INF)
    return tiles.reshape(len(heads) // 2, 2 * rows, cols)


def _head_stats(rel_bias, heads, sinks=None):
    order = np.asarray(heads)
    bias_max = jnp.take(jnp.max(rel_bias.astype(F32), axis=0), order) * LOG2E
    sink = jnp.zeros_like(bias_max) if sinks is None else jnp.take(sinks.astype(F32), order) * LOG2E
    stats = jnp.zeros((len(heads), 8), F32).at[:, 0].set(sink).at[:, 1].set(bias_max)
    return jnp.broadcast_to(stats[:, :, None], (len(heads), 8, LANES)).reshape(len(heads) // 2, 2, 8, LANES)


def _rms(x, w):
    return x * lax.rsqrt(jnp.mean(x * x, axis=-1, keepdims=True) + EPS) * w


def _rep(stat, width):
    if width == LANES:
        return stat
    return jnp.concatenate([stat] * (width // LANES), axis=1)


def _silu(z):
    return z / (1.0 + jnp.exp(-z))


def _dot_nt(a, b):
    return lax.dot_general(a, b, (((1,), (1,)), ((), ())), preferred_element_type=F32)


def _low_half(rows):
    return lax.broadcasted_iota(jnp.int32, (rows, LANES), 1) < HEAD_DIM


EVEN_NORM_SLOTS = (tuple(range(N_PAIRS)) + (KA_SLOT,)
                   + tuple(N_A_SLOTS + QB_SLOT + p for p in range(N_PAIRS))
                   + tuple(N_A_SLOTS + KB_SLOT + p for p in range(N_PAIRS)))
QA_NORM, KA_NORM, QB_NORM, KB_NORM = 0, N_PAIRS, N_PAIRS + 1, 2 * N_PAIRS + 1


def _store_norm_stats(nsq_ref, block, val, heads_per_slot):
    sq = val * val
    if heads_per_slot == 1:
        parts = [sq]
    else:
        low = _low_half(val.shape[0])
        parts = [jnp.where(low, sq, 0.0), jnp.where(low, 0.0, sq)]
    for hh, part in enumerate(parts):
        worst = jnp.max(jnp.sum(part, axis=-1, keepdims=True), axis=0, keepdims=True) * BF16_NORM_SLACK
        lanes = slice((block * heads_per_slot + hh) * LANES, (block * heads_per_slot + hh + 1) * LANES)
        nsq_ref[0, 0, :, lanes] = jnp.broadcast_to(worst, (8, LANES))


def _max_norms(nsq_ref):
    worst = nsq_ref[0, 0]
    for tile in range(1, nsq_ref.shape[1]):
        worst = jnp.maximum(worst, nsq_ref[0, tile])
    return jnp.sqrt(worst[:1, :])


def _gated_rows(g_refs):
    return jnp.concatenate([g_ref[0, p] for g_ref in g_refs for p in range(g_ref.shape[1])], axis=1)


def _projection_kernel(*refs, n_g, n_in, body):
    if n_g:
        g_refs, w_out_ref, x_ref, rest = refs[:n_g], refs[n_g], refs[n_g + 1], refs[n_g + 2:]
        ins, x_out_ref, outs = rest[:n_in], rest[n_in], rest[n_in + 1:]
        x = x_ref[0] + jnp.dot(_gated_rows(g_refs), w_out_ref[...], preferred_element_type=F32)
        x_out_ref[0] = x
    else:
        x, ins, outs = refs[0][0], refs[1:1 + n_in], refs[1 + n_in:]
    body(x, *ins, *outs)


def _projection_call(body, name, x, prev, ins, in_specs, out_specs, out_shape):
    x_spec = pl.BlockSpec((1, ROW_TILE, D_MODEL), lambda b, i: (b, i, 0))
    lead, lead_specs = [x], [x_spec]
    if prev is not None:
        gs, w_out = prev
        lead = [*gs, w_out, x]
        lead_specs = [pl.BlockSpec((1, g.shape[1], ROW_TILE, LANES), lambda b, i: (b, 0, i, 0)) for g in gs] + [
            pl.BlockSpec(w_out.shape, lambda b, i: (0, 0)), x_spec]
        out_specs = [x_spec] + out_specs
        out_shape = [jax.ShapeDtypeStruct(x.shape, F32)] + out_shape
    return pl.pallas_call(
        functools.partial(_projection_kernel, n_g=0 if prev is None else len(prev[0]), n_in=len(ins), body=body),
        grid=(BATCH, SEQ // ROW_TILE),
        in_specs=lead_specs + in_specs,
        out_specs=out_specs,
        out_shape=out_shape,
        compiler_params=_params(2),
        name=name if prev is None else "residual_" + name,
    )(*lead, *ins)


def _inproj_even_body(x, nw_ref, w_ref, a_ref, b_ref, g_ref, nsq_ref):
    hb = _rms(x, nw_ref[...]).astype(BF16)
    q_scale = HEAD_DIM ** -0.5 * LOG2E
    n_chunks = w_ref.shape[1] // (2 * LANES)
    for c in range(n_chunks):
        r = jnp.dot(hb, w_ref[:, c * 2 * LANES:(c + 1) * 2 * LANES], preferred_element_type=F32)
        for t in range(2):
            slot = 2 * c + t
            val = r[:, t * LANES:(t + 1) * LANES]
            if slot < N_A_SLOTS:
                val = val * q_scale if slot < N_PAIRS else val
                a_ref[0, slot] = val.astype(BF16)
            elif slot < N_A_SLOTS + N_B_SLOTS:
                val = val * q_scale if slot - N_A_SLOTS < N_PAIRS else val
                b_ref[0, slot - N_A_SLOTS] = val
            else:
                g_ref[0, slot - N_A_SLOTS - N_B_SLOTS] = _silu(val).astype(BF16)
            if slot in EVEN_NORM_SLOTS:
                _store_norm_stats(nsq_ref, EVEN_NORM_SLOTS.index(slot), val, 2)


def _inproj_even(x, norm_w, w, prev=None):
    n_z = AB_WIDTH // LANES
    slots = lambda n: pl.BlockSpec((1, n, ROW_TILE, LANES), lambda b, i: (b, 0, i, 0))
    out = lambda n, dtype: jax.ShapeDtypeStruct((BATCH, n, SEQ, LANES), dtype)
    return _projection_call(
        _inproj_even_body, "inproj_even", x, prev,
        ins=[norm_w.reshape(1, D_MODEL), w],
        in_specs=[pl.BlockSpec((1, D_MODEL), lambda b, i: (0, 0)), pl.BlockSpec(w.shape, lambda b, i: (0, 0))],
        out_specs=[slots(N_A_SLOTS), slots(N_B_SLOTS), slots(n_z),
                   pl.BlockSpec((1, 1, 8, 2 * len(EVEN_NORM_SLOTS) * LANES), lambda b, i: (b, i, 0, 0))],
        out_shape=[out(N_A_SLOTS, BF16), out(N_B_SLOTS, F32), out(n_z, BF16),
                   jax.ShapeDtypeStruct((BATCH, SEQ // ROW_TILE, 8, 2 * len(EVEN_NORM_SLOTS) * LANES), F32)])


def _outproj_kernel(*refs, n_g, final):
    g_refs, (w_ref, x_ref), rest = refs[:n_g], refs[n_g:n_g + 2], refs[n_g + 2:]
    y = x_ref[0] + jnp.dot(_gated_rows(g_refs), w_ref[...], preferred_element_type=F32)
    if final:
        fw_ref, o_ref = rest
        y = _rms(y, fw_ref[...])
    else:
        (o_ref,) = rest
    o_ref[0] = y


def _outproj(gs, w, x, final_w=None):
    final = final_w is not None
    in_specs = [pl.BlockSpec((1, g.shape[1], ROW_TILE, LANES), lambda b, i: (b, 0, i, 0)) for g in gs] + [
        pl.BlockSpec(w.shape, lambda b, i: (0, 0)),
        pl.BlockSpec((1, ROW_TILE, D_MODEL), lambda b, i: (b, i, 0)),
    ]
    args = [*gs, w, x]
    if final:
        in_specs.append(pl.BlockSpec((1, D_MODEL), lambda b, i: (0, 0)))
        args.append(final_w.reshape(1, D_MODEL))
    return pl.pallas_call(
        functools.partial(_outproj_kernel, n_g=len(gs), final=final),
        grid=(BATCH, SEQ // ROW_TILE),
        in_specs=in_specs,
        out_specs=pl.BlockSpec((1, ROW_TILE, D_MODEL), lambda b, i: (b, i, 0)),
        out_shape=jax.ShapeDtypeStruct((BATCH, SEQ, D_MODEL), F32),
        compiler_params=_params(2),
        name="outproj_final" if final else "outproj",
    )(*args)


def _stack_heads(q):
    lane = lax.broadcasted_iota(jnp.int32, (1, LANES), 1)
    keep0 = jnp.where(lane < HEAD_DIM, 1.0, 0.0).astype(q.dtype)
    return jnp.concatenate([q * keep0, q * (1.0 - keep0).astype(q.dtype)], axis=0)


def _pair_shift(qsq_ref, ksq_ref, stat_ref):
    bias_max = jnp.concatenate([stat_ref[0, hh, 1:2, :] for hh in range(2)], axis=1)
    return _max_norms(qsq_ref) * _max_norms(ksq_ref) + bias_max


def _stacked_rows(pair_row, rows):
    return jnp.concatenate([jnp.broadcast_to(pair_row[:, hh * LANES:(hh + 1) * LANES], (rows, LANES))
                            for hh in range(2)], axis=0)


def _fast_unit(q, k, v, shifted_bias, low):
    r = q.shape[0]
    p = jnp.exp2(_dot_nt(_stack_heads(q), k) + shifted_bias)
    sums = jnp.sum(p, axis=-1, keepdims=True)
    pv = jnp.dot(p.astype(BF16), v, preferred_element_type=F32)
    return jnp.where(low, pv[:r], pv[r:]), sums[:r], sums[r:]


def _sweep_blocks(n_blocks, block_fn, carry, unroll=BAND_UNROLL):
    carry = block_fn(0, 0, BLOCK, carry)

    def body(i, c):
        return block_fn(pl.multiple_of(i * BLOCK, BLOCK), pl.multiple_of((i - 1) * BLOCK, BLOCK), 2 * BLOCK, c)

    return lax.fori_loop(1, n_blocks, body, carry, unroll=unroll)


def _att_a_kernel(q_ref, k_ref, v_ref, g_ref, bias_ref, stat_ref, qsq_ref, ksq_ref, o_ref, shifted_bias_ref):
    low = _low_half(BLOCK)
    groups = 2 * N_PAIRS

    def stat_row(r):
        return jnp.concatenate([stat_ref[p, hh, r:r + 1, :] for p in range(N_PAIRS) for hh in range(2)], axis=1)

    def group_rows(row):
        return jnp.concatenate([jnp.broadcast_to(row[:, g * LANES:(g + 1) * LANES], (BLOCK, LANES))
                                for g in range(groups)], axis=0)

    sink = group_rows(stat_row(0))
    bound = _max_norms(qsq_ref) * jnp.concatenate([_max_norms(ksq_ref)] * N_PAIRS, axis=1) + stat_row(1)
    shift = jnp.maximum(group_rows(bound), sink)
    shifted_bias_ref[...] = bias_ref[...] - _rep(shift, 2 * BLOCK)
    sink_term = jnp.exp2(sink - shift)

    def operands(qs, ks, n_keys, bias_of):
        rows, keys = pl.ds(qs, BLOCK), pl.ds(ks, n_keys)
        q = jnp.concatenate([_stack_heads(q_ref[0, p, rows, :]) for p in range(N_PAIRS)], axis=0)
        return rows, q, k_ref[0, 0, keys, :], v_ref[0, 0, keys, :], bias_of[:, 2 * BLOCK - n_keys:]

    def emit(rows, pv, denom):
        for p in range(N_PAIRS):
            h0, h1 = slice(2 * p * BLOCK, (2 * p + 1) * BLOCK), slice((2 * p + 1) * BLOCK, (2 * p + 2) * BLOCK)
            o = jnp.where(low, pv[h0], pv[h1]) / jnp.where(low, denom[h0], denom[h1])
            o_ref[0, p, rows, :] = (o * g_ref[0, p, rows, :].astype(F32)).astype(BF16)

    def fast_block(qs, ks, n_keys, l_min):
        rows, q, k, v, shifted_bias = operands(qs, ks, n_keys, shifted_bias_ref)
        p = jnp.exp2(_dot_nt(q, k) + shifted_bias)
        denom = jnp.sum(p, axis=-1, keepdims=True) + sink_term
        emit(rows, jnp.dot(p.astype(BF16), v, preferred_element_type=F32), denom)
        for g in range(groups):
            l_min = jnp.minimum(l_min, denom[g * BLOCK:(g + 1) * BLOCK])
        return l_min

    def exact_block(qs, ks, n_keys, carry):
        rows, q, k, v, bias = operands(qs, ks, n_keys, bias_ref)
        s = _dot_nt(q, k) + bias
        m = jnp.maximum(jnp.max(s, axis=-1, keepdims=True), sink)
        p = jnp.exp2(s - _rep(m, n_keys))
        denom = jnp.sum(p, axis=-1, keepdims=True) + jnp.exp2(sink - m)
        emit(rows, jnp.dot(p.astype(BF16), v, preferred_element_type=F32), denom)
        return carry

    l_min = _sweep_blocks(SEQ // BLOCK, fast_block, jnp.full((BLOCK, LANES), jnp.inf, F32), SWA_UNROLL)

    @pl.when(jnp.logical_not(jnp.min(l_min) >= UNDERFLOW_GUARD))
    def _():
        _sweep_blocks(SEQ // BLOCK, exact_block, 0, SWA_UNROLL)


def _nsq_spec(block_of):
    return pl.BlockSpec((1, SEQ // ROW_TILE, 8, 2 * LANES), lambda b, p: (b, 0, 0, block_of(p)))


def _att_a(qkv, z, bias, stats, nsq):
    seq_spec = lambda n, slot: pl.BlockSpec((1, n, SEQ, LANES), lambda b: (b, slot, 0, 0))
    tiles = SEQ // ROW_TILE
    return pl.pallas_call(
        _att_a_kernel,
        grid=(BATCH,),
        in_specs=[
            seq_spec(N_PAIRS, 0), seq_spec(1, KA_SLOT), seq_spec(1, VA_SLOT), seq_spec(N_PAIRS, 0),
            pl.BlockSpec((N_PAIRS * 2 * BLOCK, 2 * BLOCK), lambda b: (0, 0)),
            pl.BlockSpec(stats.shape, lambda b: (0, 0, 0, 0)),
            pl.BlockSpec((1, tiles, 8, N_PAIRS * 2 * LANES), lambda b: (b, 0, 0, QA_NORM)),
            pl.BlockSpec((1, tiles, 8, 2 * LANES), lambda b: (b, 0, 0, KA_NORM)),
        ],
        out_specs=seq_spec(N_PAIRS, 0),
        out_shape=jax.ShapeDtypeStruct((BATCH, N_PAIRS, SEQ, LANES), BF16),
        scratch_shapes=[pltpu.VMEM((N_PAIRS * 2 * BLOCK, 2 * BLOCK), F32)],
        compiler_params=_params(1),
        name="att_swa",
    )(qkv, qkv, qkv, z, bias.reshape(N_PAIRS * 2 * BLOCK, 2 * BLOCK), stats, nsq, nsq)


def _att_b_kernel(q_ref, k_ref, v_ref, g_ref, b1_ref, b4_ref, b16_ref, stat_ref, qsq_ref, ksq_ref, o_ref,
                  m0_ref, m1_ref, l_ref, acc_ref, sb1_ref, sb4_ref, sb16_ref):
    low = _low_half(BLOCK)
    load = lambda ref, rows: ref[0, 0, rows, :].astype(BF16)
    shift = _pair_shift(qsq_ref, ksq_ref, stat_ref)
    for bias_ref, shifted_ref in ((b1_ref, sb1_ref), (b4_ref, sb4_ref), (b16_ref, sb16_ref)):
        shifted_ref[...] = bias_ref[0] - _rep(_stacked_rows(shift, bias_ref.shape[1] // 2), bias_ref.shape[2])

    def sweep(unit, bias1, bias4, bias16):
        def residue16(r, carry):
            rows = pl.ds(r, SEQ // 16, stride=16)
            unit(rows, load(q_ref, rows), load(k_ref, rows), load(v_ref, rows), bias16[...], True)
            return carry

        lax.fori_loop(0, 16, residue16, 0, unroll=4)

        def residue4(f, carry):
            def block4(us, ks, n_keys, c):
                rows, keys = pl.ds(us * 4 + f, BLOCK, stride=4), pl.ds(ks * 4 + f, n_keys, stride=4)
                unit(rows, load(q_ref, rows), load(k_ref, keys), load(v_ref, keys),
                     bias4[:, 2 * BLOCK - n_keys:], False)
                return c

            return _sweep_blocks(SEQ // 4 // BLOCK, block4, carry)

        lax.fori_loop(0, 4, residue4, 0)

        def block1(qs, ks, n_keys, carry):
            rows, keys = pl.ds(qs, BLOCK), pl.ds(ks, n_keys)
            unit(rows, load(q_ref, rows), load(k_ref, keys), load(v_ref, keys),
                 bias1[:, 2 * BLOCK - n_keys:], False)
            return carry

        _sweep_blocks(SEQ // BLOCK, block1, 0)

    def fast_unit(rows, q, k, v, shifted_bias, first):
        low_r = low if q.shape[0] == BLOCK else _low_half(q.shape[0])
        pv, sum0, sum1 = _fast_unit(q, k, v, shifted_bias, low_r)
        sums = jnp.where(low_r, sum0, sum1)
        if first:
            acc_ref[rows, :] = pv
            l_ref[rows, :] = sums
        else:
            acc_ref[rows, :] += pv
            l_ref[rows, :] += sums

    def exact_unit(rows, q, k, v, bias, first):
        r, n_keys = q.shape[0], k.shape[0]
        low_r = low if r == BLOCK else _low_half(r)
        s = _dot_nt(_stack_heads(q), k) + bias
        m_new = jnp.broadcast_to(jnp.max(s, axis=-1, keepdims=True), (2 * r, LANES))
        if not first:
            m_old = jnp.concatenate([m0_ref[rows, :], m1_ref[rows, :]], axis=0)
            m_new = jnp.maximum(m_old, m_new)
            alpha = jnp.exp2(m_old - m_new)
            alpha = jnp.where(low_r, alpha[:r], alpha[r:])
        p = jnp.exp2(s - _rep(m_new, n_keys))
        sums = jnp.sum(p, axis=-1, keepdims=True)
        sums = jnp.where(low_r, sums[:r], sums[r:])
        pv = jnp.dot(p.astype(BF16), v, preferred_element_type=F32)
        pv = jnp.where(low_r, pv[:r], pv[r:])
        m0_ref[rows, :] = m_new[:r]
        m1_ref[rows, :] = m_new[r:]
        if first:
            acc_ref[rows, :] = pv
            l_ref[rows, :] = sums
        else:
            acc_ref[rows, :] = alpha * acc_ref[rows, :] + pv
            l_ref[rows, :] = alpha * l_ref[rows, :] + sums

    def finish(i, l_min):
        rows = pl.ds(pl.multiple_of(i * BLOCK, BLOCK), BLOCK)
        l = l_ref[rows, :]
        gate = g_ref[0, 0, rows, :].astype(F32)
        o_ref[0, 0, rows, :] = (acc_ref[rows, :] / l * gate).astype(BF16)
        return jnp.minimum(l_min, l)

    no_min = jnp.full((BLOCK, LANES), jnp.inf, F32)
    sweep(fast_unit, sb1_ref, sb4_ref, sb16_ref)
    l_min = lax.fori_loop(0, SEQ // BLOCK, finish, no_min, unroll=BAND_UNROLL)

    @pl.when(jnp.logical_not(jnp.min(l_min) >= UNDERFLOW_GUARD))
    def _():
        sweep(exact_unit, b1_ref.at[0], b4_ref.at[0], b16_ref.at[0])
        lax.fori_loop(0, SEQ // BLOCK, finish, no_min)


def _att_b(qkv, z, b1, b4, b16, stats, nsq):
    seq_spec = lambda slot: pl.BlockSpec((1, 1, SEQ, LANES), lambda b, p: (b, slot + p, 0, 0))
    bias_spec = lambda a: pl.BlockSpec((1,) + a.shape[1:], lambda b, p: (p, 0, 0))
    return pl.pallas_call(
        _att_b_kernel,
        grid=(BATCH, N_PAIRS),
        in_specs=[seq_spec(QB_SLOT), seq_spec(KB_SLOT), seq_spec(VB_SLOT), seq_spec(N_PAIRS),
                  bias_spec(b1), bias_spec(b4), bias_spec(b16),
                  pl.BlockSpec((1, 2, 8, LANES), lambda b, p: (p, 0, 0, 0)),
                  _nsq_spec(lambda p: QB_NORM + p), _nsq_spec(lambda p: KB_NORM + p)],
        out_specs=seq_spec(0),
        out_shape=jax.ShapeDtypeStruct((BATCH, N_PAIRS, SEQ, LANES), BF16),
        scratch_shapes=[pltpu.VMEM((SEQ, LANES), F32)] * 4 + [pltpu.VMEM(b.shape[1:], F32) for b in (b1, b4, b16)],
        compiler_params=_params(2),
        name="att_dilated",
    )(qkv, qkv, qkv, z, b1, b4, b16, stats, nsq, nsq)


def _rope_table_kernel(pos_ref, freq_ref, cos_ref, sin_ref):
    ang = pos_ref[0].astype(F32) * freq_ref[...]
    cos_ref[0] = jnp.cos(ang)
    sin_ref[0] = jnp.sin(ang)


def _rope_tables(positions):
    inv_freq = ROPE_THETA ** (-jnp.arange(0, C_ROPE, 2, dtype=F32) / C_ROPE)
    lane = np.arange(LANES)
    freq = jnp.where(jnp.asarray(lane >= C_NOPE), inv_freq[lane % (C_ROPE // 2)], 0.0).reshape(1, LANES)
    table = jax.ShapeDtypeStruct((BATCH, SEQ, LANES), F32)
    spec = pl.BlockSpec((1, ROW_TILE, LANES), lambda b, i: (b, i, 0))
    return pl.pallas_call(
        _rope_table_kernel,
        grid=(BATCH, SEQ // ROW_TILE),
        in_specs=[pl.BlockSpec((1, ROW_TILE, 1), lambda b, i: (b, i, 0)),
                  pl.BlockSpec((1, LANES), lambda b, i: (0, 0))],
        out_specs=[spec, spec],
        out_shape=[table, table],
        compiler_params=_params(2),
        name="rope_tables",
    )(positions.reshape(BATCH, SEQ, 1), freq)


C_LOWRANK = C_Q_RANK + C_KV_RANK + 2 * LANES


def _inproj_odd_body(x, nw_ref, w_ref, qnw_ref, wq_ref, kvnw_ref, wk_ref, wv_ref, cos_ref, sin_ref,
                     q_ref, k_ref, v_ref, g_ref, nsq_ref):
    hb = _rms(x, nw_ref[...]).astype(BF16)
    c = jnp.dot(hb, w_ref[:, :C_LOWRANK], preferred_element_type=F32)
    for j in range(C_WIDTH // (2 * LANES)):
        lo = C_LOWRANK + j * 2 * LANES
        r = jnp.dot(hb, w_ref[:, lo:lo + 2 * LANES], preferred_element_type=F32)
        for t in range(2):
            g_ref[0, 2 * j + t] = _silu(r[:, t * LANES:(t + 1) * LANES]).astype(BF16)

    qn = _rms(c[:, :C_Q_RANK], qnw_ref[...]).astype(BF16)
    kvn = _rms(c[:, C_Q_RANK:C_Q_RANK + C_KV_RANK], kvnw_ref[...]).astype(BF16)
    k_pe = c[:, C_Q_RANK + C_KV_RANK:C_Q_RANK + C_KV_RANK + LANES]
    k_pe_rot = c[:, C_Q_RANK + C_KV_RANK + LANES:]
    cos, sin = cos_ref[0], sin_ref[0]
    lane = lax.broadcasted_iota(jnp.int32, cos.shape, 1)
    q_table = jnp.where(lane < C_NOPE + C_ROPE, cos, sin) * ((C_NOPE + C_ROPE) ** -0.5 * LOG2E)
    k_rope = k_pe * cos + k_pe_rot * sin

    for j in range(C_HEADS // 2):
        cols = slice(j * 2 * LANES, (j + 1) * 2 * LANES)
        rq = jnp.dot(qn, wq_ref[:, cols], preferred_element_type=F32)
        rk = jnp.dot(kvn, wk_ref[:, cols], preferred_element_type=F32)
        for t in range(2):
            lanes = slice(t * LANES, (t + 1) * LANES)
            q_head, k_head = rq[:, lanes] * q_table, rk[:, lanes] + k_rope
            q_ref[0, 2 * j + t] = q_head.astype(BF16)
            k_ref[0, 2 * j + t] = k_head.astype(BF16)
            _store_norm_stats(nsq_ref, 2 * j + t, q_head, 1)
            _store_norm_stats(nsq_ref, C_HEADS + 2 * j + t, k_head, 1)
    for j in range(C_WIDTH // (2 * LANES)):
        rv = jnp.dot(kvn, wv_ref[:, j * 2 * LANES:(j + 1) * 2 * LANES], preferred_element_type=F32)
        for t in range(2):
            v_ref[0, 2 * j + t] = rv[:, t * LANES:(t + 1) * LANES].astype(BF16)


def _inproj_odd(x, norm_w, w, q_norm, wq, kv_norm, wk, wv, cos, sin, prev=None):
    full = lambda a: pl.BlockSpec(a.shape, lambda b, i: (0,) * a.ndim)
    row = lambda width: pl.BlockSpec((1, ROW_TILE, width), lambda b, i: (b, i, 0))
    slots = lambda n: pl.BlockSpec((1, n, ROW_TILE, LANES), lambda b, i: (b, 0, i, 0))
    out = lambda n: jax.ShapeDtypeStruct((BATCH, n, SEQ, LANES), BF16)
    n_pairs = C_WIDTH // LANES
    vecs = [norm_w.reshape(1, D_MODEL), q_norm.reshape(1, C_Q_RANK), kv_norm.reshape(1, C_KV_RANK)]
    return _projection_call(
        _inproj_odd_body, "inproj_mla", x, prev,
        ins=[vecs[0], w, vecs[1], wq, vecs[2], wk, wv, cos, sin],
        in_specs=[full(vecs[0]), full(w), full(vecs[1]), full(wq), full(vecs[2]), full(wk), full(wv),
                  row(LANES), row(LANES)],
        out_specs=[slots(C_HEADS), slots(C_HEADS), slots(n_pairs), slots(n_pairs),
                   pl.BlockSpec((1, 1, 8, 2 * C_HEADS * LANES), lambda b, i: (b, i, 0, 0))],
        out_shape=[out(C_HEADS), out(C_HEADS), out(n_pairs), out(n_pairs),
                   jax.ShapeDtypeStruct((BATCH, SEQ // ROW_TILE, 8, 2 * C_HEADS * LANES), F32)])


def _flash_kernel(q_ref, k_ref, v_ref, g_ref, qsq_ref, ksq_ref, o_ref, vt_ref, redo_ref,
                  m0_ref, m1_ref, l0_ref, l1_ref, acc_ref):
    t, half = FLASH_TILE, FLASH_TILE // 2
    n_tiles = SEQ // t
    step = pl.program_id(2)
    low = _low_half(t)
    m_refs, l_refs = (m0_ref, m1_ref), (l0_ref, l1_ref)

    @pl.when(step == 0)
    def _():
        def prep_body(c, carry):
            rows = pl.ds(pl.multiple_of(c * half, half), half)
            v_t = v_ref[0, 0, rows, :].astype(F32).T
            for hh in range(2):
                vt_ref[hh, :, rows] = v_t[hh * HEAD_DIM:(hh + 1) * HEAD_DIM].astype(BF16)
            return carry

        lax.fori_loop(0, SEQ // half, prep_body, 0)

    bound = _max_norms(qsq_ref) * _max_norms(ksq_ref)
    shifts = [bound[:, hh * LANES:hh * LANES + 1] for hh in range(2)]

    def fast_tile(tile):
        qs = tile * t

        def part(hh, ks, width, q_lo, q_n, diagonal):
            s_t = _dot_nt(k_ref[0, hh, ks:ks + width, :], q_ref[0, hh, qs + q_lo:qs + q_lo + q_n, :])
            if diagonal:
                causal = (lax.broadcasted_iota(jnp.int32, (width, q_n), 0)
                          <= lax.broadcasted_iota(jnp.int32, (width, q_n), 1))
                s_t = jnp.where(causal, s_t, NEG_INF)
            p_t = jnp.exp2(s_t - shifts[hh])
            return (jnp.dot(vt_ref[hh, :, ks:ks + width], p_t.astype(BF16), preferred_element_type=F32),
                    jnp.sum(p_t, axis=0, keepdims=True))

        def add_late(total, late):
            return jnp.concatenate([total[:, :half], total[:, half:] + late], axis=1)

        o_t, sums = [], []
        for hh in range(2):
            acc, l = part(hh, qs, half, 0, t, True)
            for j in range(tile):
                acc_j, l_j = part(hh, j * t, t, 0, t, False)
                acc, l = acc + acc_j, l + l_j
            acc_late, l_late = part(hh, qs + half, half, half, half, True)
            sums.append(add_late(l, l_late))
            o_t.append(add_late(acc, acc_late) / sums[hh])
        gate = g_ref[0, 0, qs:qs + t, :].astype(F32)
        o_ref[0, 0, qs:qs + t, :] = (jnp.concatenate(o_t, axis=0).T * gate).astype(BF16)
        return jnp.minimum(jnp.min(sums[0]), jnp.min(sums[1])) >= UNDERFLOW_GUARD

    for first in range(n_tiles // 2):
        @pl.when(step == first)
        def _(first=first):
            healthy = jnp.logical_and(fast_tile(first), fast_tile(n_tiles - 1 - first))
            redo_ref[0] = jnp.where(healthy, 0, 1).astype(jnp.int32)

    def exact_tile(which, carry):
        tile = jnp.where(which == 0, step, n_tiles - 1 - step)
        q_start = pl.multiple_of(tile * t, t)
        q_rows = pl.ds(q_start, t)

        def exact_step(j, c):
            ks = pl.multiple_of(j * half, half)
            v = v_ref[0, 0, pl.ds(ks, half), :]
            visible = (lax.broadcasted_iota(jnp.int32, (t, half), 1) + (ks - q_start)
                       <= lax.broadcasted_iota(jnp.int32, (t, half), 0))
            pvs, alphas = [], []
            for hh in range(2):
                s = jnp.where(visible, _dot_nt(q_ref[0, hh, q_rows, :], k_ref[0, hh, pl.ds(ks, half), :]), NEG_INF)
                m_old = m_refs[hh][...]
                m_new = jnp.maximum(m_old, jnp.max(s, axis=-1, keepdims=True))
                alpha = jnp.exp2(m_old - m_new)
                p = jnp.exp2(s - _rep(m_new, half))
                m_refs[hh][...] = m_new
                l_refs[hh][...] = alpha * l_refs[hh][...] + jnp.sum(p, axis=-1, keepdims=True)
                pvs.append(jnp.dot(p.astype(BF16), v, preferred_element_type=F32))
                alphas.append(alpha)
            acc_ref[...] = jnp.where(low, alphas[0], alphas[1]) * acc_ref[...] + jnp.where(low, pvs[0], pvs[1])
            return c

        for hh in range(2):
            m_refs[hh][...] = jnp.full((t, LANES), NEG_INF, F32)
            l_refs[hh][...] = jnp.zeros((t, LANES), F32)
        acc_ref[...] = jnp.zeros((t, LANES), F32)
        lax.fori_loop(0, 2 * tile + 2, exact_step, 0)
        o = acc_ref[...] / jnp.where(low, l0_ref[...], l1_ref[...])
        o_ref[0, 0, q_rows, :] = (o * g_ref[0, 0, q_rows, :].astype(F32)).astype(BF16)
        return carry

    @pl.when(redo_ref[0] != 0)
    def _():
        lax.fori_loop(0, 2, exact_tile, 0)


def _flash(q, k, v, z, nsq):
    t = FLASH_TILE
    n_pairs = C_WIDTH // LANES
    heads = pl.BlockSpec((1, 2, SEQ, LANES), lambda b, p, i: (b, p, 0, 0))
    pair = pl.BlockSpec((1, 1, SEQ, LANES), lambda b, p, i: (b, p, 0, 0))
    return pl.pallas_call(
        _flash_kernel,
        grid=(BATCH, n_pairs, SEQ // t // 2),
        in_specs=[
            heads, heads, pair, pair,
            pl.BlockSpec((1, SEQ // ROW_TILE, 8, 2 * LANES), lambda b, p, i: (b, 0, 0, p)),
            pl.BlockSpec((1, SEQ // ROW_TILE, 8, 2 * LANES), lambda b, p, i: (b, 0, 0, C_HEADS // 2 + p)),
        ],
        out_specs=pair,
        out_shape=jax.ShapeDtypeStruct((BATCH, n_pairs, SEQ, LANES), BF16),
        scratch_shapes=[pltpu.VMEM((2, HEAD_DIM, SEQ), BF16), pltpu.SMEM((1,), jnp.int32)]
        + [pltpu.VMEM((t, LANES), F32)] * 5,
        compiler_params=_params(3),
        name="mla_flash",
    )(q, k, v, z, nsq, nsq)


def _pair_order_cols(w):
    rows = w.shape[0]
    return w.reshape(rows, A_KV_HEADS, N_PAIRS, HEAD_DIM).transpose(0, 2, 1, 3).reshape(rows, A_WIDTH)


def _even_weights(w_in, w_out):
    z_lo = A_WIDTH + 2 * A_KV_WIDTH + 3 * B_WIDTH
    w = jnp.concatenate([_pair_order_cols(w_in[:, :A_WIDTH]), w_in[:, A_WIDTH:z_lo],
                         _pair_order_cols(w_in[:, z_lo:z_lo + A_WIDTH]), w_in[:, z_lo + A_WIDTH:]],
                        axis=1).astype(BF16)
    wo_a = w_out[:A_WIDTH].reshape(A_KV_HEADS, N_PAIRS, HEAD_DIM, D_MODEL).transpose(1, 0, 2, 3)
    w_o = jnp.concatenate([wo_a.reshape(A_WIDTH, D_MODEL), w_out[A_WIDTH:]], axis=0).astype(BF16)
    return w, w_o


def _rot_half_cols(w):
    half = w.shape[1] // 2
    return jnp.concatenate([-w[:, half:], w[:, :half]], axis=1)


def _odd_weights(w_in, w_qb, w_kvb):
    splits = np.cumsum([C_Q_RANK, C_KV_RANK, C_ROPE])
    w_cq, w_ckv, w_kpe, w_z = jnp.split(w_in, splits, axis=1)
    pad = jnp.zeros((D_MODEL, C_NOPE), w_in.dtype)
    w_kpe_rot = _rot_half_cols(w_kpe)
    w = jnp.concatenate([w_cq, w_ckv, pad, w_kpe, w_kpe, pad, w_kpe_rot, w_kpe_rot, w_z], axis=1).astype(BF16)

    wq = w_qb.reshape(C_Q_RANK, C_HEADS, C_NOPE + C_ROPE)
    wq_pe = wq[:, :, C_NOPE:]
    wq_rot = jnp.concatenate([-wq_pe[:, :, C_ROPE // 2:], wq_pe[:, :, :C_ROPE // 2]], axis=2)
    wq = jnp.concatenate([wq, wq_rot], axis=2).reshape(C_Q_RANK, C_HEADS * LANES).astype(BF16)

    wkv = w_kvb.reshape(C_KV_RANK, C_HEADS, C_NOPE + C_V)
    wk = jnp.concatenate([wkv[:, :, :C_NOPE], jnp.zeros((C_KV_RANK, C_HEADS, LANES - C_NOPE), w_kvb.dtype)], axis=2)
    wk = wk.reshape(C_KV_RANK, C_HEADS * LANES).astype(BF16)
    wv = wkv[:, :, C_NOPE:].reshape(C_KV_RANK, C_WIDTH).astype(BF16)
    return w, wq, wk, wv


def kernel(x, positions, norm_w, rel_bias, ab_w_in, ab_sinks, ab_w_out, c_w_in, c_q_norm, c_w_qb, c_kv_norm,
           c_w_kvb, c_w_out, final_norm):
    bias_a = _bias_tiles(rel_bias, A_PAIR_HEADS, 1, BLOCK, 2 * BLOCK, BLOCK)
    bias_b1 = _bias_tiles(rel_bias, B_PAIR_HEADS, 1, BLOCK, 2 * BLOCK, BLOCK)
    bias_b4 = _bias_tiles(rel_bias, B_PAIR_HEADS, 4, BLOCK, 2 * BLOCK, BLOCK)
    bias_b16 = _bias_tiles(rel_bias, B_PAIR_HEADS, 16, SEQ // 16, SEQ // 16, 0)
    stats_b = _head_stats(rel_bias, B_PAIR_HEADS)
    cos, sin = _rope_tables(positions)

    prev = None
    for layer in range(DEPTH):
        i = layer // 2
        if layer % 2 == 0:
            w, w_o = _even_weights(ab_w_in[i], ab_w_out[i])
            outs = _inproj_even(x, norm_w[layer], w, prev)
            if prev is not None:
                x, outs = outs[0], outs[1:]
            qkv_a, qkv_b, gate, nsq = outs
            gs = [_att_a(qkv_a, gate, bias_a, _head_stats(rel_bias, A_PAIR_HEADS, ab_sinks[i]), nsq),
                  _att_b(qkv_b, gate, bias_b1, bias_b4, bias_b16, stats_b, nsq)]
        else:
            w, wq, wk, wv = _odd_weights(c_w_in[i], c_w_qb[i], c_w_kvb[i])
            outs = _inproj_odd(x, norm_w[layer], w, c_q_norm[i], wq, c_kv_norm[i], wk, wv, cos, sin, prev)
            if prev is not None:
                x, outs = outs[0], outs[1:]
            q, k, v, gate, nsq = outs
            gs = [_flash(q, k, v, gate, nsq)]
            w_o = c_w_out[i].astype(BF16)
        prev = (gs, w_o)
    return _outproj(prev[0], prev[1], x, final_norm)
```
